```python
import jax, jax.numpy as jnp
from jax import lax
import numpy as np

D_MODEL = 2048
BATCH = 1
SEQ = 16384
DEPTH = 2

HEAD_DIM = 128
D_MIX = D_MODEL
SGU_WIDTH = D_MIX // 2
SGU_GROUPS = SGU_WIDTH // HEAD_DIM
SGU_CHUNK = 128
ATT_WIDTH = D_MIX - SGU_WIDTH
ATT_HEADS = ATT_WIDTH // HEAD_DIM
DILATED_PATTERNS = ((128, 1), (512, 4), (2048, 16))
ATT_BLOCK = 128
ROPE_THETA = 500000.0
ROPE_DIM = HEAD_DIM // 4
D_IN = 2 * SGU_WIDTH + 3 * ATT_WIDTH
N_EXPERTS = 64
TOP_K = 8
N_GROUPS = 8
TOPK_GROUPS = 4
EXPERT_DIM = 512
SHARED_DIM = 512
ROUTED_SCALE = 2.5
MOE_BLOCK = 256
NORM_EPS = 1e-6
N_MOD = 6

kernel_name = "hybrid_sgu_dilated_moe_block"


def rms_norm(x, gain):
    x32 = x.astype(jnp.float32)
    y = x32 * lax.rsqrt(jnp.mean(x32 * x32, axis=-1, keepdims=True) + NORM_EPS)
    return (y * gain.astype(jnp.float32)).astype(x.dtype)


def partial_rope(x, pos):
    half = ROPE_DIM // 2
    inv = ROPE_THETA ** (-jnp.arange(half, dtype=jnp.float32) * 2.0 / ROPE_DIM)
    ang = pos.astype(jnp.float32)[:, None] * inv[None, :]
    cos, sin = jnp.cos(ang), jnp.sin(ang)
    x32 = x.astype(jnp.float32)
    x1, x2, rest = x32[..., :half], x32[..., half:ROPE_DIM], x32[..., ROPE_DIM:]
    out = jnp.concatenate([x1 * cos - x2 * sin, x2 * cos + x1 * sin, rest], axis=-1)
    return out.astype(x.dtype)


def chunked_spatial_gating(uv, norm_gain, w_s, b_s):
    b, s, _ = uv.shape
    z = jax.nn.gelu(uv, approximate=False)
    u, v = jnp.split(z, 2, axis=-1)
    v = rms_norm(v.reshape(b, s, SGU_GROUPS, HEAD_DIM), norm_gain.reshape(SGU_GROUPS, HEAD_DIM))
    v = v.reshape(b, s // SGU_CHUNK, SGU_CHUNK, SGU_GROUPS, HEAD_DIM)
    causal = jnp.tril(jnp.ones((SGU_CHUNK, SGU_CHUNK), dtype=bool))
    w = jnp.where(causal[None], w_s, 0).astype(v.dtype)
    mixed = jnp.einsum('gij,bnjgc->bnigc', w, v) + b_s.T[None, None, :, :, None].astype(v.dtype)
    return u * mixed.reshape(b, s, SGU_WIDTH)


def dilated_window_attention(q, k, v, window, dilation):
    b, h, s, dh = q.shape
    span = window // dilation
    sub_len = s // dilation
    nb = -(-sub_len // ATT_BLOCK)
    pad = nb * ATT_BLOCK - sub_len

    def to_blocks(t):
        t = t.reshape(b, h, sub_len, dilation, dh).transpose(0, 1, 3, 2, 4)
        t = jnp.pad(t, ((0, 0), (0, 0), (0, 0), (0, pad), (0, 0)))
        return t.reshape(b, h, dilation, nb, ATT_BLOCK, dh)

    def with_prev(t):
        prev = jnp.pad(t[:, :, :, :-1], ((0, 0), (0, 0), (0, 0), (1, 0), (0, 0), (0, 0)))
        return jnp.concatenate([prev, t], axis=-2)

    qb = to_blocks(q)
    kk = with_prev(to_blocks(k))
    vv = with_prev(to_blocks(v))
    scores = jnp.einsum('bhrnqd,bhrnkd->bhrnqk', qb, kk,
                        preferred_element_type=jnp.float32) * (dh ** -0.5)
    qi = jnp.arange(ATT_BLOCK)[:, None]
    kj = jnp.arange(2 * ATT_BLOCK)[None, :]
    dist = qi + ATT_BLOCK - kj
    blk = jnp.arange(nb)[:, None, None]
    valid = (dist >= 0) & (dist <= span) & (blk * ATT_BLOCK - ATT_BLOCK + kj >= 0)
    scores = jnp.where(valid, scores, -jnp.inf)
    m = jnp.max(scores, axis=-1, keepdims=True)
    p = jnp.exp(scores - m)
    l = jnp.sum(p, axis=-1, keepdims=True)
    o = jnp.einsum('bhrnqk,bhrnkd->bhrnqd', p, vv.astype(jnp.float32)) / l
    lse = (m + jnp.log(l))[..., 0]

    def from_blocks(t):
        tail = t.shape[5:]
        t = t.reshape((b, h, dilation, nb * ATT_BLOCK) + tail)[:, :, :, :sub_len]
        t = jnp.moveaxis(t, 2, 3)
        return t.reshape((b, h, s) + tail)

    return from_blocks(o), from_blocks(lse)


def dilated_mixture_attention(q, k, v, q_gain, k_gain, pos):
    q = partial_rope(rms_norm(q, q_gain), pos)
    k = partial_rope(rms_norm(k, k_gain), pos)
    outs, lses = [], []
    for window, dilation in DILATED_PATTERNS:
        o, lse = dilated_window_attention(q, k, v, window, dilation)
        outs.append(o)
        lses.append(lse)
    alpha = jax.nn.softmax(jnp.stack(lses, axis=0), axis=0)
    o = jnp.einsum('pbhs,pbhsd->bhsd', alpha, jnp.stack(outs, axis=0))
    return o.astype(v.dtype)


def route(h_flat, router_w, router_bias):
    t = h_flat.shape[0]
    scores = jax.nn.sigmoid(jnp.dot(h_flat.astype(jnp.float32), router_w.astype(jnp.float32)))
    sel = scores + router_bias.astype(jnp.float32)
    grp = sel.reshape(t, N_GROUPS, N_EXPERTS // N_GROUPS)
    grp_score = jnp.sum(lax.top_k(grp, 2)[0], axis=-1)
    _, gidx = lax.top_k(grp_score, TOPK_GROUPS)
    gmask = jnp.sum(jax.nn.one_hot(gidx, N_GROUPS, dtype=jnp.float32), axis=1)
    emask = jnp.repeat(gmask, N_EXPERTS // N_GROUPS, axis=1) > 0
    _, idx = lax.top_k(jnp.where(emask, sel, -jnp.inf), TOP_K)
    w = jnp.take_along_axis(scores, idx, axis=1)
    w = w / jnp.sum(w, axis=-1, keepdims=True) * ROUTED_SCALE
    return idx, w


def routed_experts(h_flat, idx, w, w_gate, w_up, w_down):
    t, d = h_flat.shape
    m = t * TOP_K
    flat_e = idx.reshape(m).astype(jnp.int32)
    flat_tok = jnp.repeat(jnp.arange(t, dtype=jnp.int32), TOP_K)
    flat_w = w.reshape(m)
    order = jnp.argsort(flat_e)
    sorted_e, sorted_tok, sorted_w = flat_e[order], flat_tok[order], flat_w[order]
    counts = jnp.bincount(flat_e, length=N_EXPERTS).astype(jnp.int32)
    padded = (counts + MOE_BLOCK - 1) // MOE_BLOCK * MOE_BLOCK
    padded_end = jnp.cumsum(padded)
    padded_start = padded_end - padded
    group_start = jnp.cumsum(counts) - counts
    dest = padded_start[sorted_e] + jnp.arange(m, dtype=jnp.int32) - group_start[sorted_e]
    n_blocks = -(-m // MOE_BLOCK) + N_EXPERTS
    buf = n_blocks * MOE_BLOCK
    tok_buf = jnp.full((buf,), t, jnp.int32).at[dest].set(sorted_tok)
    w_buf = jnp.zeros((buf,), h_flat.dtype).at[dest].set(sorted_w.astype(h_flat.dtype))
    block_e = jnp.minimum(
        jnp.searchsorted(padded_end, jnp.arange(n_blocks, dtype=jnp.int32) * MOE_BLOCK, side='right'),
        N_EXPERTS - 1)
    h_ext = jnp.concatenate([h_flat, jnp.zeros((1, d), h_flat.dtype)], axis=0)

    def step(acc, blk):
        tok, gw, e = blk
        xb = h_ext[tok]
        hid = jax.nn.silu(xb @ w_gate[e]) * (xb @ w_up[e])
        y = (hid @ w_down[e]) * gw[:, None]
        return acc.at[tok].add(y), None

    acc0 = jnp.zeros((t + 1, d), h_flat.dtype)
    acc, _ = lax.scan(step, acc0, (tok_buf.reshape(n_blocks, MOE_BLOCK),
                                   w_buf.reshape(n_blocks, MOE_BLOCK), block_e))
    return acc[:t]


def setup_inputs(seed: int = 0) -> dict:
    key = jax.random.key(seed)
    ks = jax.random.split(key, 24)
    f32 = jnp.float32

    def nrm(k, shape, scale):
        return jax.random.normal(k, shape, f32) * scale

    def gain(k, shape):
        return 1.0 + 0.05 * jax.random.normal(k, shape, f32)

    L, D, E = DEPTH, D_MODEL, N_EXPERTS
    return {
        'x': nrm(ks[0], (BATCH, SEQ, D), 1.0),
        'c': nrm(ks[1], (BATCH, D), 1.0),
        'ada_w': nrm(ks[2], (L, D, N_MOD * D), 0.5 * D ** -0.5),
        'ada_b': nrm(ks[3], (L, N_MOD * D), 0.01),
        'mix_norm': gain(ks[4], (L, D)),
        'w_in': nrm(ks[5], (L, D, D_IN), D ** -0.5),
        'sgu_norm': gain(ks[6], (L, SGU_WIDTH)),
        'sgu_w': nrm(ks[7], (L, SGU_GROUPS, SGU_CHUNK, SGU_CHUNK), SGU_CHUNK ** -0.5),
        'sgu_b': gain(ks[8], (L, SGU_GROUPS, SGU_CHUNK)),
        'q_norm': gain(ks[9], (L, HEAD_DIM)),
        'k_norm': gain(ks[10], (L, HEAD_DIM)),
        'out_norm_sgu': gain(ks[11], (L, SGU_WIDTH)),
        'out_norm_att': gain(ks[12], (L, ATT_WIDTH)),
        'w_out': nrm(ks[13], (L, D_MIX, D), D_MIX ** -0.5),
        'ffn_norm': gain(ks[14], (L, D)),
        'router_w': nrm(ks[15], (L, D, E), D ** -0.5),
        'router_bias': nrm(ks[16], (L, E), 0.01),
        'exp_gate': nrm(ks[17], (L, E, D, EXPERT_DIM), D ** -0.5),
        'exp_up': nrm(ks[18], (L, E, D, EXPERT_DIM), D ** -0.5),
        'exp_down': nrm(ks[19], (L, E, EXPERT_DIM, D), EXPERT_DIM ** -0.5),
        'shared_gate': nrm(ks[20], (L, D, SHARED_DIM), D ** -0.5),
        'shared_up': nrm(ks[21], (L, D, SHARED_DIM), D ** -0.5),
        'shared_down': nrm(ks[22], (L, SHARED_DIM, D), SHARED_DIM ** -0.5),
    }


def reference(x, c, ada_w, ada_b, mix_norm, w_in, sgu_norm, sgu_w, sgu_b, q_norm, k_norm,
              out_norm_sgu, out_norm_att, w_out, ffn_norm, router_w, router_bias,
              exp_gate, exp_up, exp_down, shared_gate, shared_up, shared_down):
    b, s, d = x.shape
    pos = jnp.arange(s, dtype=jnp.int32)
    cond = jax.nn.silu(c)
    splits = [2 * SGU_WIDTH, 2 * SGU_WIDTH + ATT_WIDTH, 2 * SGU_WIDTH + 2 * ATT_WIDTH]
    for l in range(DEPTH):
        mod = (cond @ ada_w[l] + ada_b[l])[:, None, :]
        sh_m, sc_m, g_m, sh_f, sc_f, g_f = jnp.split(mod, N_MOD, axis=-1)

        h = rms_norm(x, mix_norm[l]) * (1 + sc_m) + sh_m
        proj = h @ w_in[l]
        uv, q, k, v = jnp.split(proj, splits, axis=-1)
        y_sgu = chunked_spatial_gating(uv, sgu_norm[l], sgu_w[l], sgu_b[l])
        y_sgu = rms_norm(y_sgu.reshape(b, s, SGU_GROUPS, HEAD_DIM),
                         out_norm_sgu[l].reshape(SGU_GROUPS, HEAD_DIM)).reshape(b, s, SGU_WIDTH)

        def heads(t):
            return t.reshape(b, s, ATT_HEADS, HEAD_DIM).transpose(0, 2, 1, 3)

        y_att = dilated_mixture_attention(heads(q), heads(k), heads(v), q_norm[l], k_norm[l], pos)
        y_att = rms_norm(y_att.transpose(0, 2, 1, 3),
                         out_norm_att[l].reshape(ATT_HEADS, HEAD_DIM)).reshape(b, s, ATT_WIDTH)
        mixed = jnp.concatenate([y_sgu, y_att], axis=-1) @ w_out[l]
        x = x + g_m * mixed

        h = rms_norm(x, ffn_norm[l]) * (1 + sc_f) + sh_f
        h_flat = h.reshape(b * s, d)
        idx, gw = route(h_flat, router_w[l], router_bias[l])
        y_routed = routed_experts(h_flat, idx, gw, exp_gate[l], exp_up[l], exp_down[l])
        y_shared = (jax.nn.silu(h_flat @ shared_gate[l]) * (h_flat @ shared_up[l])) @ shared_down[l]
        x = x + g_f * (y_routed + y_shared).reshape(b, s, d)
    return x
```

```python
import functools

import jax
import jax.numpy as jnp
from jax import lax
from jax.experimental import pallas as pl
from jax.experimental.pallas import tpu as pltpu

F32 = jnp.float32
BF16 = jnp.bfloat16
U32 = jnp.uint32
I32 = jnp.int32

HEAD_DIM = 128
N_HEADS = 8
SGU_CHUNK = 128
DILATIONS = (1, 4, 16)
ATT_SPAN = 128
ATT_TILE = 2048
ROPE_THETA = 500000.0
ROPE_DIM = HEAD_DIM // 4
N_EXPERTS = 64
TOP_K = 8
N_GROUPS = 8
TOPK_GROUPS = 4
ROUTED_SCALE = 2.5
NORM_EPS = 1e-6
N_MOD = 6

MOE_BLOCK = 256
TOKEN_TILE = 256
HI_MASK = 0xFFFF0000

VMEM_LIMIT = 56 * 1024 * 1024

NT_DIMS = (((1,), (1,)), ((), ()))


def _cparams(sem):
    return pltpu.CompilerParams(dimension_semantics=sem, vmem_limit_bytes=VMEM_LIMIT)


def _rms(x):
    return x * lax.rsqrt(jnp.mean(x * x, axis=-1, keepdims=True) + NORM_EPS)


def _gelu(x):
    return 0.5 * x * (1.0 + lax.erf(x * 0.7071067811865476))


def _pack_pair(lo_f32, hi_f32):
    lo = pltpu.bitcast(lo_f32.astype(BF16).astype(F32), U32)
    hi = pltpu.bitcast(hi_f32.astype(BF16).astype(F32), U32)
    return (lo >> 16) | (hi & jnp.uint32(HI_MASK))


def _unpack_pair(u):
    lo = pltpu.bitcast(u << 16, F32)
    hi = pltpu.bitcast(u & jnp.uint32(HI_MASK), F32)
    return lo, hi


def _ada_kernel(c_ref, w_ref, b_ref, o_ref):
    d, tn = w_ref.shape

    def body(i, acc):
        r = pl.multiple_of(i * 8, 8)
        cc = c_ref[pl.ds(r, 8), :]
        return acc + (cc * jax.nn.sigmoid(cc)) * w_ref[pl.ds(r, 8), :]

    acc = lax.fori_loop(0, d // 8, body, jnp.zeros((8, tn), F32), unroll=8)
    o_ref[...] = jnp.sum(acc, axis=0, keepdims=True) + b_ref[...]


def _ada(c, ada_w, ada_b):
    nl, d, n = ada_w.shape
    tn = 1536
    return pl.pallas_call(
        _ada_kernel,
        grid=(nl, n // tn),
        in_specs=[pl.BlockSpec((d, 1), lambda l, j: (0, 0)),
                  pl.BlockSpec((None, d, tn), lambda l, j: (l, 0, j)),
                  pl.BlockSpec((None, 1, tn), lambda l, j: (l, 0, j))],
        out_specs=pl.BlockSpec((None, 1, tn), lambda l, j: (l, 0, j)),
        out_shape=jax.ShapeDtypeStruct((nl, 1, n), F32),
        compiler_params=_cparams(("arbitrary", "arbitrary")),
        name="ada",
    )(c.reshape(d, 1), ada_w, ada_b.reshape(nl, 1, n))


def _proj_kernel(x_ref, g_ref, sc_ref, sh_ref, w_ref, qg_ref, kg_ref, rc_ref, ra_ref, rb_ref,
                 uv_ref, q1_ref, q4_ref, q16_ref, k1_ref, k4_ref, k16_ref, v1_ref, v4_ref, v16_ref,
                 h_s, st_s):
    j = pl.program_id(1)
    tm = x_ref.shape[0]

    @pl.when(j == 0)
    def _():
        y = _rms(x_ref[...]) * g_ref[...]
        h_s[...] = (y * (1.0 + sc_ref[...]) + sh_ref[...]).astype(BF16)

    acc = jnp.dot(h_s[...], w_ref[...], preferred_element_type=F32)

    def norm_rope(gain_ref, scale):
        for h in range(N_HEADS):
            a = _rms(acc[:, h * HEAD_DIM:(h + 1) * HEAD_DIM]) * gain_ref[...]
            a = (a * rc_ref[...] + pltpu.roll(a, ROPE_DIM // 2, 1) * ra_ref[...]
                 + pltpu.roll(a, HEAD_DIM - ROPE_DIM // 2, 1) * rb_ref[...])
            st_s[h] = a * scale

    def emit(o1, o4, o16):
        for h in range(N_HEADS):
            o1[:, h * HEAD_DIM:(h + 1) * HEAD_DIM] = st_s[h].astype(BF16)
        for d, o in ((4, o4), (16, o16)):
            for h in range(N_HEADS):
                for r in range(d):
                    c = (h * d + r) * HEAD_DIM
                    o[:, c:c + HEAD_DIM] = st_s[h, pl.ds(r, tm // d, stride=d), :].astype(BF16)

    @pl.when(j < 2)
    def _():
        uv_ref[...] = acc.astype(BF16)

    @pl.when(j == 2)
    def _():
        norm_rope(qg_ref, HEAD_DIM ** -0.5)
        emit(q1_ref, q4_ref, q16_ref)

    @pl.when(j == 3)
    def _():
        norm_rope(kg_ref, 1.0)
        emit(k1_ref, k4_ref, k16_ref)

    @pl.when(j == 4)
    def _():
        for h in range(N_HEADS):
            st_s[h] = acc[:, h * HEAD_DIM:(h + 1) * HEAD_DIM]
        emit(v1_ref, v4_ref, v16_ref)


def _proj(x, gain, sc, sh, w_bf, q_gain, k_gain, rope):
    s, d = x.shape
    tm, tn = 512, N_HEADS * HEAD_DIM
    aw = N_HEADS * HEAD_DIM
    row = lambda i, j: (i, 0)
    vec = lambda i, j: (0, 0)
    out_shape = [jax.ShapeDtypeStruct((s, 2 * tn), BF16)]
    out_specs = [pl.BlockSpec((tm, tn), lambda i, j: (i, jnp.minimum(j, 1)))]
    for _ in range(3):
        for dil in DILATIONS:
            out_shape.append(jax.ShapeDtypeStruct((s // dil, dil * aw), BF16))
            out_specs.append(pl.BlockSpec((tm // dil, dil * aw), row))
    return pl.pallas_call(
        _proj_kernel,
        grid=(s // tm, 5),
        in_specs=[pl.BlockSpec((tm, d), row), pl.BlockSpec((1, d), vec), pl.BlockSpec((1, d), vec),
                  pl.BlockSpec((1, d), vec), pl.BlockSpec((d, tn), lambda i, j: (0, j)),
                  pl.BlockSpec((1, HEAD_DIM), vec), pl.BlockSpec((1, HEAD_DIM), vec),
                  pl.BlockSpec((tm, HEAD_DIM), row), pl.BlockSpec((tm, HEAD_DIM), row),
                  pl.BlockSpec((tm, HEAD_DIM), row)],
        out_specs=out_specs,
        out_shape=out_shape,
        scratch_shapes=[pltpu.VMEM((tm, d), BF16), pltpu.VMEM((N_HEADS, tm, HEAD_DIM), F32)],
        compiler_params=_cparams(("arbitrary", "arbitrary")),
        name="proj",
    )(x, gain, sc, sh, w_bf, q_gain, k_gain, *rope)


def _sgu_kernel(uv_ref, gn_ref, w_ref, b_ref, go_ref, o_ref):
    tm = uv_ref.shape[0]
    width = N_HEADS * HEAD_DIM
    ii = lax.broadcasted_iota(I32, (SGU_CHUNK, SGU_CHUNK), 0)
    jj = lax.broadcasted_iota(I32, (SGU_CHUNK, SGU_CHUNK), 1)
    causal = jj <= ii
    for g in range(N_HEADS):
        cols = slice(g * HEAD_DIM, (g + 1) * HEAD_DIM)
        wg = jnp.where(causal, w_ref[g], 0.0).astype(BF16)
        u = _gelu(uv_ref[:, cols].astype(F32))
        v = _gelu(uv_ref[:, width + g * HEAD_DIM:width + (g + 1) * HEAD_DIM].astype(F32))
        vn = (_rms(v) * gn_ref[:, cols]).astype(BF16)
        bias = b_ref[:, g:g + 1]
        for n in range(tm // SGU_CHUNK):
            rows = slice(n * SGU_CHUNK, (n + 1) * SGU_CHUNK)
            mixed = jnp.dot(wg, vn[rows], preferred_element_type=F32) + bias
            y = u[rows] * mixed
            o_ref[rows, cols] = (_rms(y) * go_ref[:, cols]).astype(BF16)


def _sgu(uv, sgu_norm, sgu_w, sgu_b_t, out_norm):
    s = uv.shape[0]
    width = N_HEADS * HEAD_DIM
    tm = 512
    vec = lambda i: (0, 0)
    return pl.pallas_call(
        _sgu_kernel,
        grid=(s // tm,),
        in_specs=[pl.BlockSpec((tm, 2 * width), lambda i: (i, 0)),
                  pl.BlockSpec((1, width), vec),
                  pl.BlockSpec((N_HEADS, SGU_CHUNK, SGU_CHUNK), lambda i: (0, 0, 0)),
                  pl.BlockSpec((SGU_CHUNK, N_HEADS), vec),
                  pl.BlockSpec((1, width), vec)],
        out_specs=pl.BlockSpec((tm, width), lambda i: (i, 0)),
        out_shape=jax.ShapeDtypeStruct((s, width), BF16),
        compiler_params=_cparams(("arbitrary",)),
        name="sgu",
    )(uv, sgu_norm, sgu_w, sgu_b_t, out_norm)


def _attn_kernel(q1, q4, q16, k1, k4, k16, k1p, k4p, k16p, v1, v4, v16, v1p, v4p, v16p, go_ref,
                 o_ref, kb1, kb4, kb16, vb1, vb4, vb16, os1, os4, os16, ls1, ls4, ls16):
    first = pl.program_id(0) == 0
    blk = ATT_SPAN
    for buf, prev, cur in ((kb1, k1p, k1), (kb4, k4p, k4), (kb16, k16p, k16),
                           (vb1, v1p, v1), (vb4, v4p, v4), (vb16, v16p, v16)):
        buf[0:blk, :] = prev[...]
        buf[blk:, :] = cur[...]

    qi = lax.broadcasted_iota(I32, (blk, 2 * blk), 0)
    kj = lax.broadcasted_iota(I32, (blk, 2 * blk), 1)
    dist = qi + blk - kj
    band = (dist >= 0) & (dist <= ATT_SPAN)
    neg = jnp.where(band, 0.0, -jnp.inf)
    neg0 = jnp.where(first, jnp.where(band & (kj >= blk), 0.0, -jnp.inf), neg)

    for d, q_ref, kb, vb, o_s, l_s in ((1, q1, kb1, vb1, os1, ls1), (4, q4, kb4, vb4, os4, ls4),
                                       (16, q16, kb16, vb16, os16, ls16)):
        nb = ATT_TILE // (blk * d)
        for r in range(d):
            cols = slice(r * HEAD_DIM, (r + 1) * HEAD_DIM)
            for b in range(nb):
                q = q_ref[b * blk:(b + 1) * blk, cols]
                kk = kb[b * blk:(b + 2) * blk, cols]
                vv = vb[b * blk:(b + 2) * blk, cols]
                s = lax.dot_general(q, kk, NT_DIMS, preferred_element_type=F32)
                s = s + (neg0 if b == 0 else neg)
                m = jnp.max(s, axis=-1, keepdims=True)
                e = jnp.exp(s - m)
                l = jnp.sum(e, axis=-1, keepdims=True)
                o = jnp.dot(e.astype(BF16), vv, preferred_element_type=F32) / l
                lse = jnp.broadcast_to(m + jnp.log(l), (blk, HEAD_DIM))
                if d == 1:
                    o_s[b * blk:(b + 1) * blk, :] = o
                    l_s[b * blk:(b + 1) * blk, :] = lse
                else:
                    o_s[pl.ds(d * b * blk + r, blk, stride=d), :] = o
                    l_s[pl.ds(d * b * blk + r, blk, stride=d), :] = lse

    step = 256
    for c in range(ATT_TILE // step):
        rows = slice(c * step, (c + 1) * step)
        l1, l4, l16 = ls1[rows, :], ls4[rows, :], ls16[rows, :]
        mx = jnp.maximum(l1, jnp.maximum(l4, l16))
        w1, w4, w16 = jnp.exp(l1 - mx), jnp.exp(l4 - mx), jnp.exp(l16 - mx)
        o = (w1 * os1[rows, :] + w4 * os4[rows, :] + w16 * os16[rows, :]) / (w1 + w4 + w16)
        o_ref[rows, :] = (_rms(o) * go_ref[...]).astype(BF16)


def _attn(qkv, out_norm):
    q1, q4, q16, k1, k4, k16, v1, v4, v16 = qkv
    s = q1.shape[0]
    nt = s // ATT_TILE
    blk = ATT_SPAN

    def cur(d):
        return pl.BlockSpec((ATT_TILE // d, d * HEAD_DIM), lambda i, h: (i, h))

    def prev(d):
        per = ATT_TILE // (d * blk)
        return pl.BlockSpec((blk, d * HEAD_DIM), lambda i, h: (jnp.maximum(i * per - 1, 0), h))

    in_specs = ([cur(d) for d in DILATIONS] + [cur(d) for d in DILATIONS] + [prev(d) for d in DILATIONS]
                + [cur(d) for d in DILATIONS] + [prev(d) for d in DILATIONS]
                + [pl.BlockSpec((1, HEAD_DIM), lambda i, h: (0, h))])
    kv_bufs = [pltpu.VMEM((blk + ATT_TILE // d, d * HEAD_DIM), BF16) for d in DILATIONS]
    acc_bufs = [pltpu.VMEM((ATT_TILE, HEAD_DIM), F32) for _ in DILATIONS]
    return pl.pallas_call(
        _attn_kernel,
        grid=(nt, N_HEADS),
        in_specs=in_specs,
        out_specs=pl.BlockSpec((ATT_TILE, HEAD_DIM), lambda i, h: (i, h)),
        out_shape=jax.ShapeDtypeStruct((s, N_HEADS * HEAD_DIM), BF16),
        scratch_shapes=kv_bufs + kv_bufs + acc_bufs + acc_bufs,
        compiler_params=_cparams(("arbitrary", "arbitrary")),
        name="attn",
    )(q1, q4, q16, k1, k4, k16, k1, k4, k16, v1, v4, v16, v1, v4, v16, out_norm)


def _wout_kernel(a1_ref, a2_ref, w_ref, x_ref, g_ref, o_ref):
    half = a1_ref.shape[1]
    acc = jnp.dot(a1_ref[...], w_ref[0:half, :], preferred_element_type=F32)
    acc = acc + jnp.dot(a2_ref[...], w_ref[half:2 * half, :], preferred_element_type=F32)
    o_ref[...] = x_ref[...] + g_ref[...] * acc


def _wout(y_sgu, y_att, w_bf, x, gate):
    s, d = x.shape
    half = y_sgu.shape[1]
    tm, tn = 1024, 1024
    return pl.pallas_call(
        _wout_kernel,
        grid=(s // tm, d // tn),
        in_specs=[pl.BlockSpec((tm, half), lambda i, j: (i, 0)),
                  pl.BlockSpec((tm, half), lambda i, j: (i, 0)),
                  pl.BlockSpec((2 * half, tn), lambda i, j: (0, j)),
                  pl.BlockSpec((tm, tn), lambda i, j: (i, j)),
                  pl.BlockSpec((1, tn), lambda i, j: (0, j))],
        out_specs=pl.BlockSpec((tm, tn), lambda i, j: (i, j)),
        out_shape=jax.ShapeDtypeStruct((s, d), F32),
        compiler_params=_cparams(("arbitrary", "arbitrary")),
        name="wout",
    )(y_sgu, y_att, w_bf, x, gate)


def _router_kernel(x_ref, g_ref, sc_ref, sh_ref, rwt_ref, rb_ref,
                   h_ref, idx_ref, wt_ref, rank_ref, cnt_ref, carry_s):
    tm = x_ref.shape[0]
    gsz = N_EXPERTS // N_GROUPS

    @pl.when(pl.program_id(0) == 0)
    def _():
        carry_s[...] = jnp.zeros_like(carry_s)

    h = (_rms(x_ref[...]) * g_ref[...]) * (1.0 + sc_ref[...]) + sh_ref[...]
    h_ref[...] = h.astype(BF16)

    hh = h.astype(BF16)
    hl = (h - hh.astype(F32)).astype(BF16)
    rw = rwt_ref[...]
    rh = rw.astype(BF16)
    rl = (rw - rh.astype(F32)).astype(BF16)
    dg = functools.partial(lax.dot_general, dimension_numbers=NT_DIMS, preferred_element_type=F32)
    logits = dg(rh, hh) + dg(rl, hh) + dg(rh, hl)
    scores = jax.nn.sigmoid(logits)
    sel = scores + rb_ref[...]

    io8 = lax.broadcasted_iota(I32, (gsz, tm), 0)
    grp, gscore = [], []
    for g in range(N_GROUPS):
        sg = sel[g * gsz:(g + 1) * gsz, :]
        m1 = jnp.max(sg, axis=0, keepdims=True)
        i1 = jnp.min(jnp.where(sg == m1, io8, gsz), axis=0, keepdims=True)
        m2 = jnp.max(jnp.where(io8 == i1, -jnp.inf, sg), axis=0, keepdims=True)
        grp.append(sg)
        gscore.append(m1 + m2)

    parts = []
    for g in range(N_GROUPS):
        beaten = jnp.zeros((1, tm), I32)
        for g2 in range(N_GROUPS):
            if g2 == g:
                continue
            b = (gscore[g2] >= gscore[g]) if g2 < g else (gscore[g2] > gscore[g])
            beaten = beaten + b.astype(I32)
        parts.append(jnp.where(beaten < TOPK_GROUPS, grp[g], -jnp.inf))
    masked = jnp.concatenate(parts, axis=0)

    io = lax.broadcasted_iota(I32, (N_EXPERTS, tm), 0)
    chosen = jnp.zeros((N_EXPERTS, tm), F32)
    idxs, wts = [], []
    for _ in range(TOP_K):
        m = jnp.max(masked, axis=0, keepdims=True)
        am = jnp.min(jnp.where(masked == m, io, N_EXPERTS), axis=0, keepdims=True)
        hit = io == am
        wts.append(jnp.sum(jnp.where(hit, scores, 0.0), axis=0, keepdims=True))
        idxs.append(am)
        chosen = chosen + hit.astype(F32)
        masked = jnp.where(hit, -jnp.inf, masked)
    wsum = wts[0]
    for k in range(1, TOP_K):
        wsum = wsum + wts[k]
    for k in range(TOP_K):
        idx_ref[k:k + 1, :] = idxs[k]
        wt_ref[k:k + 1, :] = wts[k] / wsum * ROUTED_SCALE

    t0 = lax.broadcasted_iota(I32, (tm, tm), 0)
    t1 = lax.broadcasted_iota(I32, (tm, tm), 1)
    before = (t0 < t1).astype(BF16)
    pre = jnp.dot(chosen.astype(BF16), before, preferred_element_type=F32) + carry_s[...]
    for k in range(TOP_K):
        rank_ref[k:k + 1, :] = jnp.sum(jnp.where(io == idxs[k], pre, 0.0), axis=0,
                                       keepdims=True).astype(I32)
    carry_s[...] = carry_s[...] + jnp.sum(chosen, axis=1, keepdims=True)
    cnt_ref[...] = jnp.broadcast_to(carry_s[...], cnt_ref.shape)


def _router(x, gain, sc, sh, rw_t, rbias):
    t, d = x.shape
    tm = 512
    vec = lambda i: (0, 0)
    tok = lambda i: (0, i)
    return pl.pallas_call(
        _router_kernel,
        grid=(t // tm,),
        in_specs=[pl.BlockSpec((tm, d), lambda i: (i, 0)), pl.BlockSpec((1, d), vec),
                  pl.BlockSpec((1, d), vec), pl.BlockSpec((1, d), vec),
                  pl.BlockSpec((N_EXPERTS, d), vec), pl.BlockSpec((N_EXPERTS, 1), vec)],
        out_specs=[pl.BlockSpec((tm, d), lambda i: (i, 0)), pl.BlockSpec((TOP_K, tm), tok),
                   pl.BlockSpec((TOP_K, tm), tok), pl.BlockSpec((TOP_K, tm), tok),
                   pl.BlockSpec((N_EXPERTS, HEAD_DIM), vec)],
        out_shape=[jax.ShapeDtypeStruct((t, d), BF16), jax.ShapeDtypeStruct((TOP_K, t), I32),
                   jax.ShapeDtypeStruct((TOP_K, t), F32), jax.ShapeDtypeStruct((TOP_K, t), I32),
                   jax.ShapeDtypeStruct((N_EXPERTS, HEAD_DIM), F32)],
        scratch_shapes=[pltpu.VMEM((N_EXPERTS, 1), F32)],
        compiler_params=_cparams(("arbitrary",)),
        name="router",
    )(x, gain, sc, sh, rw_t, rbias)


def _dest_kernel(idx_ref, rank_ref, start_ref, o_ref):
    tm = idx_ref.shape[1]
    io = lax.broadcasted_iota(I32, (N_EXPERTS, tm), 0)
    for k in range(TOP_K):
        seg = jnp.sum(jnp.where(io == idx_ref[k:k + 1, :], start_ref[...], 0), axis=0, keepdims=True)
        o_ref[k:k + 1, :] = seg + rank_ref[k:k + 1, :]


def _dest(idx_t, rank_t, seg_start):
    t = idx_t.shape[1]
    tm = 2048
    tok = lambda i: (0, i)
    return pl.pallas_call(
        _dest_kernel,
        grid=(t // tm,),
        in_specs=[pl.BlockSpec((TOP_K, tm), tok), pl.BlockSpec((TOP_K, tm), tok),
                  pl.BlockSpec((N_EXPERTS, 1), lambda i: (0, 0))],
        out_specs=pl.BlockSpec((TOP_K, tm), tok),
        out_shape=jax.ShapeDtypeStruct((TOP_K, t), I32),
        compiler_params=_cparams(("arbitrary",)),
        name="dest",
    )(idx_t, rank_t, seg_start.reshape(N_EXPERTS, 1))


def _dispatch_kernel(dest_hbm, h_ref, xs_hbm, slot_s, pk_s, sem_idx, sem_row):
    tm, d = h_ref.shape
    n = TOP_K * tm
    base = pl.multiple_of(pl.program_id(0) * n, n)
    idx_copy = pltpu.make_async_copy(dest_hbm.at[pl.ds(base, n)], slot_s, sem_idx)
    idx_copy.start()
    pk_s[...] = _pack_pair(h_ref[:, 0:d // 2].astype(F32), h_ref[:, d // 2:d].astype(F32))
    idx_copy.wait()

    def row_copy(j):
        t = j & (tm - 1)
        return pltpu.make_async_copy(pk_s.at[pl.ds(t, 1)], xs_hbm.at[pl.ds(slot_s[j], 1)], sem_row)

    def start(j, c):
        row_copy(j).start()
        return c

    def wait(j, c):
        row_copy(j).wait()
        return c

    lax.fori_loop(0, n, start, 0, unroll=8)
    lax.fori_loop(0, n, wait, 0, unroll=8)


def _dispatch(dest_flat, h_bf):
    t, d = h_bf.shape
    tm = TOKEN_TILE
    return pl.pallas_call(
        _dispatch_kernel,
        grid=(t // tm,),
        in_specs=[pl.BlockSpec(memory_space=pl.ANY), pl.BlockSpec((tm, d), lambda i: (i, 0))],
        out_specs=pl.BlockSpec(memory_space=pl.ANY),
        out_shape=jax.ShapeDtypeStruct((t * TOP_K, d // 2), U32),
        scratch_shapes=[pltpu.SMEM((TOP_K * tm,), I32), pltpu.VMEM((tm, d // 2), U32),
                        pltpu.SemaphoreType.DMA, pltpu.SemaphoreType.DMA],
        compiler_params=_cparams(("arbitrary",)),
        name="dispatch",
    )(dest_flat, h_bf)


def _moe_kernel(blk_s, exp_s, lo_s, hi_s, first_s, newe_s, x_ref, wg_ref, wu_ref, wd_ref, y_ref,
                acc_s, wg_s, wu_s, wd_s):
    w = pl.program_id(0)
    bm = x_ref.shape[0]
    half = x_ref.shape[1]

    @pl.when(newe_s[w] == 1)
    def _():
        wg_s[...] = wg_ref[...].astype(BF16)
        wu_s[...] = wu_ref[...].astype(BF16)
        wd_s[...] = wd_ref[...].astype(BF16)

    @pl.when(first_s[w] == 1)
    def _():
        acc_s[...] = jnp.zeros_like(acc_s)

    lo, hi = lo_s[w], hi_s[w]

    @pl.when(hi > lo)
    def _():
        xl, xh = _unpack_pair(x_ref[...])
        xl, xh = xl.astype(BF16), xh.astype(BF16)
        gate = (jnp.dot(xl, wg_s[0:half, :], preferred_element_type=F32)
                + jnp.dot(xh, wg_s[half:2 * half, :], preferred_element_type=F32))
        up = (jnp.dot(xl, wu_s[0:half, :], preferred_element_type=F32)
              + jnp.dot(xh, wu_s[half:2 * half, :], preferred_element_type=F32))
        hid = (gate * jax.nn.sigmoid(gate) * up).astype(BF16)
        y = jnp.dot(hid, wd_s[...], preferred_element_type=F32)
        rows = lax.broadcasted_iota(I32, (bm, 1), 0)
        acc_s[...] = acc_s[...] + jnp.where((rows >= lo) & (rows < hi), y, 0.0)

    y_ref[...] = _pack_pair(acc_s[:, 0:half], acc_s[:, half:2 * half])


def _moe(items, xs, layer, w_gate, w_up, w_down):
    m, half = xs.shape
    _, _, d, f = w_gate.shape
    n_items = items[0].shape[0]
    bm = MOE_BLOCK
    wmap = lambda w, blk, ex, lo, hi, fi, ne: (layer, ex[w], 0, 0)
    grid_spec = pltpu.PrefetchScalarGridSpec(
        num_scalar_prefetch=6,
        grid=(n_items,),
        in_specs=[pl.BlockSpec((bm, half), lambda w, blk, ex, lo, hi, fi, ne: (blk[w], 0)),
                  pl.BlockSpec((None, None, d, f), wmap),
                  pl.BlockSpec((None, None, d, f), wmap),
                  pl.BlockSpec((None, None, f, d), wmap)],
        out_specs=pl.BlockSpec((bm, half), lambda w, blk, ex, lo, hi, fi, ne: (blk[w], 0)),
        scratch_shapes=[pltpu.VMEM((bm, d), F32), pltpu.VMEM((d, f), BF16),
                        pltpu.VMEM((d, f), BF16), pltpu.VMEM((f, d), BF16)])
    return pl.pallas_call(
        _moe_kernel,
        grid_spec=grid_spec,
        out_shape=jax.ShapeDtypeStruct((m, half), U32),
        compiler_params=_cparams(("arbitrary",)),
        name="moe",
    )(*items, xs, w_gate, w_up, w_down)


def _work_items(counts, m):
    bm = MOE_BLOCK
    nb = m // bm
    seg_end = jnp.cumsum(counts)
    seg_start = seg_end - counts
    starts = jnp.sort(jnp.concatenate([jnp.arange(nb, dtype=I32) * bm, seg_start]))
    ends = jnp.concatenate([starts[1:], jnp.full((1,), m, I32)])
    length = ends - starts
    blk = jnp.minimum(starts // bm, nb - 1)
    ex = jnp.minimum(jnp.searchsorted(seg_end, starts, side="right").astype(I32), N_EXPERTS - 1)
    ex = lax.cummax(jnp.where(length > 0, ex, 0))
    lo = starts - blk * bm
    hi = lo + length
    one = jnp.ones((1,), I32)
    first = jnp.concatenate([one, (blk[1:] != blk[:-1]).astype(I32)])
    newe = jnp.concatenate([one, (ex[1:] != ex[:-1]).astype(I32)])
    return (blk, ex, lo, hi, first, newe), seg_start


def _combine_kernel(dest_hbm, ys_hbm, h_ref, wt_ref, x_ref, g_ref, sg_ref, su_ref, sd_ref, o_ref,
                    slot_s, y_s, sem_idx, sem_row):
    tm, d = h_ref.shape
    half = d // 2
    n = TOP_K * tm
    base = pl.multiple_of(pl.program_id(0) * n, n)
    idx_copy = pltpu.make_async_copy(dest_hbm.at[pl.ds(base, n)], slot_s, sem_idx)
    idx_copy.start()
    idx_copy.wait()

    def row_copy(j):
        return pltpu.make_async_copy(ys_hbm.at[pl.ds(slot_s[j], 1)], y_s.at[pl.ds(j, 1)], sem_row)

    def start(j, c):
        row_copy(j).start()
        return c

    def wait(j, c):
        row_copy(j).wait()
        return c

    lax.fori_loop(0, n, start, 0, unroll=8)

    h = h_ref[...]
    gate = jnp.dot(h, sg_ref[...], preferred_element_type=F32)
    up = jnp.dot(h, su_ref[...], preferred_element_type=F32)
    hid = (gate * jax.nn.sigmoid(gate) * up).astype(BF16)
    shared = jnp.dot(hid, sd_ref[...], preferred_element_type=F32)

    lax.fori_loop(0, n, wait, 0, unroll=8)

    r_lo = jnp.zeros((tm, half), F32)
    r_hi = jnp.zeros((tm, half), F32)
    for k in range(TOP_K):
        lo, hi = _unpack_pair(y_s[k * tm:(k + 1) * tm, :])
        wk = wt_ref[:, k:k + 1]
        r_lo = r_lo + wk * lo
        r_hi = r_hi + wk * hi
    o_ref[:, 0:half] = x_ref[:, 0:half] + g_ref[:, 0:half] * (r_lo + shared[:, 0:half])
    o_ref[:, half:d] = x_ref[:, half:d] + g_ref[:, half:d] * (r_hi + shared[:, half:d])


def _combine(dest_flat, ys, h_bf, wt, x, gate, sg_bf, su_bf, sd_bf):
    t, d = x.shape
    f = sg_bf.shape[1]
    tm = TOKEN_TILE
    row = lambda i: (i, 0)
    vec = lambda i: (0, 0)
    return pl.pallas_call(
        _combine_kernel,
        grid=(t // tm,),
        in_specs=[pl.BlockSpec(memory_space=pl.ANY), pl.BlockSpec(memory_space=pl.ANY),
                  pl.BlockSpec((tm, d), row), pl.BlockSpec((tm, TOP_K), row),
                  pl.BlockSpec((tm, d), row), pl.BlockSpec((1, d), vec),
                  pl.BlockSpec((d, f), vec), pl.BlockSpec((d, f), vec), pl.BlockSpec((f, d), vec)],
        out_specs=pl.BlockSpec((tm, d), row),
        out_shape=jax.ShapeDtypeStruct((t, d), F32),
        scratch_shapes=[pltpu.SMEM((TOP_K * tm,), I32), pltpu.VMEM((TOP_K * tm, d // 2), U32),
                        pltpu.SemaphoreType.DMA, pltpu.SemaphoreType.DMA],
        compiler_params=_cparams(("arbitrary",)),
        name="combine",
    )(dest_flat, ys, h_bf, wt, x, gate, sg_bf, su_bf, sd_bf)


def _rope_tables(s):
    half = ROPE_DIM // 2
    inv = ROPE_THETA ** (-jnp.arange(half, dtype=F32) * 2.0 / ROPE_DIM)
    ang = jnp.arange(s, dtype=I32).astype(F32)[:, None] * inv[None, :]
    cos, sin = jnp.cos(ang), jnp.sin(ang)
    pad = HEAD_DIM - ROPE_DIM
    rc = jnp.concatenate([cos, cos, jnp.ones((s, pad), F32)], axis=1)
    ra = jnp.concatenate([jnp.zeros((s, half), F32), sin, jnp.zeros((s, pad), F32)], axis=1)
    rb = jnp.concatenate([-sin, jnp.zeros((s, half + pad), F32)], axis=1)
    return rc, ra, rb


def _mixer(x, mod, rope, mix_norm, w_in, sgu_norm, sgu_w, sgu_b, q_norm, k_norm,
           out_norm_sgu, out_norm_att, w_out):
    d = x.shape[1]
    sh_m, sc_m, g_m = mod[:, 0:d], mod[:, d:2 * d], mod[:, 2 * d:3 * d]
    outs = _proj(x, mix_norm.reshape(1, d), sc_m, sh_m, w_in.astype(BF16),
                 q_norm.reshape(1, HEAD_DIM), k_norm.reshape(1, HEAD_DIM), rope)
    y_sgu = _sgu(outs[0], sgu_norm.reshape(1, -1), sgu_w, sgu_b.T, out_norm_sgu.reshape(1, -1))
    y_att = _attn(outs[1:], out_norm_att.reshape(1, -1))
    return _wout(y_sgu, y_att, w_out.astype(BF16), x, g_m)


def _ffn(x, mod, layer, ffn_norm, router_w, router_bias, exp_gate, exp_up, exp_down,
         shared_gate, shared_up, shared_down):
    t, d = x.shape
    sh_f, sc_f, g_f = mod[:, 3 * d:4 * d], mod[:, 4 * d:5 * d], mod[:, 5 * d:6 * d]
    h_bf, idx_t, wt_t, rank_t, cnt = _router(x, ffn_norm.reshape(1, d), sc_f, sh_f, router_w.T,
                                             router_bias.reshape(N_EXPERTS, 1))
    counts = cnt[:, 0].astype(I32)
    items, seg_start = _work_items(counts, t * TOP_K)
    dest_t = _dest(idx_t, rank_t, seg_start)
    nt = t // TOKEN_TILE
    dest_flat = dest_t.reshape(TOP_K, nt, TOKEN_TILE).transpose(1, 0, 2).reshape(-1)
    xs = _dispatch(dest_flat, h_bf)
    ys = _moe(items, xs, layer, exp_gate, exp_up, exp_down)
    return _combine(dest_flat, ys, h_bf, wt_t.T, x, g_f, shared_gate.astype(BF16),
                    shared_up.astype(BF16), shared_down.astype(BF16))


def kernel(x, c, ada_w, ada_b, mix_norm, w_in, sgu_norm, sgu_w, sgu_b, q_norm, k_norm, out_norm_sgu,
           out_norm_att, w_out, ffn_norm, router_w, router_bias, exp_gate, exp_up, exp_down,
           shared_gate, shared_up, shared_down):
    b, s, d = x.shape
    assert b == 1 and s % ATT_TILE == 0
    mods = _ada(c, ada_w, ada_b)
    rope = _rope_tables(s)
    xf = x.reshape(s, d)
    for l in range(ada_w.shape[0]):
        xf = _mixer(xf, mods[l], rope, mix_norm[l], w_in[l], sgu_norm[l], sgu_w[l], sgu_b[l], q_norm[l],
                    k_norm[l], out_norm_sgu[l], out_norm_att[l], w_out[l])
        xf = _ffn(xf, mods[l], l, ffn_norm[l], router_w[l], router_bias[l], exp_gate, exp_up,
                  exp_down, shared_gate[l], shared_up[l], shared_down[l])
    return xf.reshape(b, s, d)
```

```python
import functools

import jax
import jax.numpy as jnp
from jax import lax
from jax.experimental import pallas as pl
from jax.experimental.pallas import tpu as pltpu

F32 = jnp.float32
BF16 = jnp.bfloat16
U32 = jnp.uint32
I32 = jnp.int32

HEAD_DIM = 128
N_HEADS = 8
SGU_CHUNK = 128
DILATIONS = (1, 4, 16)
ATT_SPAN = 128
ATT_TILE = 2048
ROPE_THETA = 500000.0
ROPE_DIM = HEAD_DIM // 4
N_EXPERTS = 64
TOP_K = 8
N_GROUPS = 8
TOPK_GROUPS = 4
ROUTED_SCALE = 2.5
NORM_EPS = 1e-6
N_MOD = 6

MOE_BLOCK = 256
TOKEN_TILE = 256
HI_MASK = 0xFFFF0000
LANES = 128
ROW_SUB = 8

VMEM_LIMIT = 56 * 1024 * 1024

NT_DIMS = (((1,), (1,)), ((), ()))


def _cparams(sem):
    return pltpu.CompilerParams(dimension_semantics=sem, vmem_limit_bytes=VMEM_LIMIT)


def _rms(x):
    return x * lax.rsqrt(jnp.mean(x * x, axis=-1, keepdims=True) + NORM_EPS)


def _gelu(x):
    return 0.5 * x * (1.0 + lax.erf(x * 0.7071067811865476))


def _pack_pair(lo_f32, hi_f32):
    lo = pltpu.bitcast(lo_f32.astype(BF16).astype(F32), U32)
    hi = pltpu.bitcast(hi_f32.astype(BF16).astype(F32), U32)
    return (lo >> 16) | (hi & jnp.uint32(HI_MASK))


def _unpack_pair(u):
    lo = pltpu.bitcast(u << 16, F32)
    hi = pltpu.bitcast(u & jnp.uint32(HI_MASK), F32)
    return lo, hi


def _store_rows(ref, base, n, lo_f32, hi_f32):
    for c in range(ROW_SUB):
        cols = slice(c * LANES, (c + 1) * LANES)
        ref[pl.ds(base + c, n, stride=ROW_SUB), :] = _pack_pair(lo_f32[:, cols], hi_f32[:, cols])


def _load_rows(ref, base, n):
    los, his = [], []
    for c in range(ROW_SUB):
        lo, hi = _unpack_pair(ref[pl.ds(base + c, n, stride=ROW_SUB), :])
        los.append(lo)
        his.append(hi)
    return jnp.concatenate(los, axis=1), jnp.concatenate(his, axis=1)


def _ada_kernel(c_ref, w_ref, b_ref, o_ref):
    d, tn = w_ref.shape

    def body(i, acc):
        r = pl.multiple_of(i * 8, 8)
        cc = c_ref[pl.ds(r, 8), :]
        return acc + (cc * jax.nn.sigmoid(cc)) * w_ref[pl.ds(r, 8), :]

    acc = lax.fori_loop(0, d // 8, body, jnp.zeros((8, tn), F32), unroll=8)
    o_ref[...] = jnp.sum(acc, axis=0, keepdims=True) + b_ref[...]


def _ada(c, ada_w, ada_b):
    nl, d, n = ada_w.shape
    tn = 1536
    return pl.pallas_call(
        _ada_kernel,
        grid=(nl, n // tn),
        in_specs=[pl.BlockSpec((d, 1), lambda l, j: (0, 0)),
                  pl.BlockSpec((None, d, tn), lambda l, j: (l, 0, j)),
                  pl.BlockSpec((None, 1, tn), lambda l, j: (l, 0, j))],
        out_specs=pl.BlockSpec((None, 1, tn), lambda l, j: (l, 0, j)),
        out_shape=jax.ShapeDtypeStruct((nl, 1, n), F32),
        compiler_params=_cparams(("arbitrary", "arbitrary")),
        name="ada",
    )(c.reshape(d, 1), ada_w, ada_b.reshape(nl, 1, n))


def _proj_kernel(x_ref, g_ref, sc_ref, sh_ref, w_ref, qg_ref, kg_ref, rc_ref, ra_ref, rb_ref,
                 uv_ref, q1_ref, q4_ref, q16_ref, k1_ref, k4_ref, k16_ref, v1_ref, v4_ref, v16_ref,
                 h_s, st_s):
    tm = x_ref.shape[0]
    pair = 2 * HEAD_DIM
    aw = N_HEADS * HEAD_DIM
    y = _rms(x_ref[...]) * g_ref[...]
    h_s[...] = (y * (1.0 + sc_ref[...]) + sh_ref[...]).astype(BF16)

    for p in range(uv_ref.shape[1] // pair):
        cols = slice(p * pair, (p + 1) * pair)
        uv_ref[:, cols] = jnp.dot(h_s[...], w_ref[:, cols], preferred_element_type=F32).astype(BF16)

    groups = ((q1_ref, q4_ref, q16_ref, qg_ref, HEAD_DIM ** -0.5),
              (k1_ref, k4_ref, k16_ref, kg_ref, 1.0),
              (v1_ref, v4_ref, v16_ref, None, 1.0))
    for gi, (o1, o4, o16, gain_ref, scale) in enumerate(groups):
        for p in range(N_HEADS // 2):
            c0 = uv_ref.shape[1] + gi * aw + p * pair
            acc = jnp.dot(h_s[...], w_ref[:, c0:c0 + pair], preferred_element_type=F32)
            for hh in range(2):
                h = 2 * p + hh
                a = acc[:, hh * HEAD_DIM:(hh + 1) * HEAD_DIM]
                if gain_ref is not None:
                    a = _rms(a) * gain_ref[...]
                    a = (a * rc_ref[...] + pltpu.roll(a, ROPE_DIM // 2, 1) * ra_ref[...]
                         + pltpu.roll(a, HEAD_DIM - ROPE_DIM // 2, 1) * rb_ref[...]) * scale
                o1[:, h * HEAD_DIM:(h + 1) * HEAD_DIM] = a.astype(BF16)
                stage = st_s.at[gi * N_HEADS + h]
                stage[...] = a
                for d, o in ((4, o4), (16, o16)):
                    for r in range(d):
                        c = (h * d + r) * HEAD_DIM
                        o[:, c:c + HEAD_DIM] = stage[pl.ds(r, tm // d, stride=d), :].astype(BF16)


def _proj(x, gain, sc, sh, w_bf, q_gain, k_gain, rope):
    s, d = x.shape
    n = w_bf.shape[1]
    tm = 256
    aw = N_HEADS * HEAD_DIM
    row = lambda i: (i, 0)
    vec = lambda i: (0, 0)
    out_shape = [jax.ShapeDtypeStruct((s, 2 * aw), BF16)]
    out_specs = [pl.BlockSpec((tm, 2 * aw), row)]
    for _ in range(3):
        for dil in DILATIONS:
            out_shape.append(jax.ShapeDtypeStruct((s // dil, dil * aw), BF16))
            out_specs.append(pl.BlockSpec((tm // dil, dil * aw), row))
    return pl.pallas_call(
        _proj_kernel,
        grid=(s // tm,),
        in_specs=[pl.BlockSpec((tm, d), row), pl.BlockSpec((1, d), vec), pl.BlockSpec((1, d), vec),
                  pl.BlockSpec((1, d), vec),
                  pl.BlockSpec((d, n), vec, pipeline_mode=pl.Buffered(1)),
                  pl.BlockSpec((1, HEAD_DIM), vec), pl.BlockSpec((1, HEAD_DIM), vec),
                  pl.BlockSpec((tm, HEAD_DIM), row), pl.BlockSpec((tm, HEAD_DIM), row),
                  pl.BlockSpec((tm, HEAD_DIM), row)],
        out_specs=out_specs,
        out_shape=out_shape,
        scratch_shapes=[pltpu.VMEM((tm, d), BF16), pltpu.VMEM((3 * N_HEADS, tm, HEAD_DIM), F32)],
        compiler_params=_cparams(("arbitrary",)),
        name="proj",
    )(x, gain, sc, sh, w_bf, q_gain, k_gain, *rope)


def _sgu_kernel(uv_ref, gn_ref, w_ref, b_ref, go_ref, o_ref):
    tm = uv_ref.shape[0]
    width = N_HEADS * HEAD_DIM
    ii = lax.broadcasted_iota(I32, (SGU_CHUNK, SGU_CHUNK), 0)
    jj = lax.broadcasted_iota(I32, (SGU_CHUNK, SGU_CHUNK), 1)
    causal = jj <= ii
    for g in range(N_HEADS):
        cols = slice(g * HEAD_DIM, (g + 1) * HEAD_DIM)
        wg = jnp.where(causal, w_ref[g], 0.0).astype(BF16)
        u = _gelu(uv_ref[:, cols].astype(F32))
        v = _gelu(uv_ref[:, width + g * HEAD_DIM:width + (g + 1) * HEAD_DIM].astype(F32))
        vn = (_rms(v) * gn_ref[:, cols]).astype(BF16)
        bias = b_ref[:, g:g + 1]
        for n in range(tm // SGU_CHUNK):
            rows = slice(n * SGU_CHUNK, (n + 1) * SGU_CHUNK)
            mixed = jnp.dot(wg, vn[rows], preferred_element_type=F32) + bias
            y = u[rows] * mixed
            o_ref[rows, cols] = (_rms(y) * go_ref[:, cols]).astype(BF16)


def _sgu(uv, sgu_norm, sgu_w, sgu_b_t, out_norm):
    s = uv.shape[0]
    width = N_HEADS * HEAD_DIM
    tm = 512
    vec = lambda i: (0, 0)
    return pl.pallas_call(
        _sgu_kernel,
        grid=(s // tm,),
        in_specs=[pl.BlockSpec((tm, 2 * width), lambda i: (i, 0)),
                  pl.BlockSpec((1, width), vec),
                  pl.BlockSpec((N_HEADS, SGU_CHUNK, SGU_CHUNK), lambda i: (0, 0, 0)),
                  pl.BlockSpec((SGU_CHUNK, N_HEADS), vec),
                  pl.BlockSpec((1, width), vec)],
        out_specs=pl.BlockSpec((tm, width), lambda i: (i, 0)),
        out_shape=jax.ShapeDtypeStruct((s, width), BF16),
        compiler_params=_cparams(("arbitrary",)),
        name="sgu",
    )(uv, sgu_norm, sgu_w, sgu_b_t, out_norm)


def _attn_kernel(q1, q4, q16, k1, k4, k16, k1p, k4p, k16p, v1, v4, v16, v1p, v4p, v16p, go_ref,
                 o_ref, kb1, kb4, kb16, vb1, vb4, vb16, os1, os4, os16, ls1, ls4, ls16):
    first = pl.program_id(0) == 0
    blk = ATT_SPAN
    for buf, prev, cur in ((kb1, k1p, k1), (kb4, k4p, k4), (kb16, k16p, k16),
                           (vb1, v1p, v1), (vb4, v4p, v4), (vb16, v16p, v16)):
        buf[0:blk, :] = prev[...]
        buf[blk:, :] = cur[...]

    qi = lax.broadcasted_iota(I32, (blk, 2 * blk), 0)
    kj = lax.broadcasted_iota(I32, (blk, 2 * blk), 1)
    dist = qi + blk - kj
    band = (dist >= 0) & (dist <= ATT_SPAN)
    neg = jnp.where(band, 0.0, -jnp.inf)
    neg0 = jnp.where(first, jnp.where(band & (kj >= blk), 0.0, -jnp.inf), neg)

    for d, q_ref, kb, vb, o_s, l_s in ((1, q1, kb1, vb1, os1, ls1), (4, q4, kb4, vb4, os4, ls4),
                                       (16, q16, kb16, vb16, os16, ls16)):
        nb = ATT_TILE // (blk * d)
        for r in range(d):
            cols = slice(r * HEAD_DIM, (r + 1) * HEAD_DIM)
            for b in range(nb):
                q = q_ref[b * blk:(b + 1) * blk, cols]
                kk = kb[b * blk:(b + 2) * blk, cols]
                vv = vb[b * blk:(b + 2) * blk, cols]
                s = lax.dot_general(q, kk, NT_DIMS, preferred_element_type=F32)
                s = s + (neg0 if b == 0 else neg)
                m = jnp.max(s, axis=-1, keepdims=True)
                e = jnp.exp(s - m)
                l = jnp.sum(e, axis=-1, keepdims=True)
                o = jnp.dot(e.astype(BF16), vv, preferred_element_type=F32) / l
                lse = jnp.broadcast_to(m + jnp.log(l), (blk, HEAD_DIM))
                if d == 1:
                    o_s[b * blk:(b + 1) * blk, :] = o
                    l_s[b * blk:(b + 1) * blk, :] = lse
                else:
                    o_s[pl.ds(d * b * blk + r, blk, stride=d), :] = o
                    l_s[pl.ds(d * b * blk + r, blk, stride=d), :] = lse

    step = 256
    for c in range(ATT_TILE // step):
        rows = slice(c * step, (c + 1) * step)
        l1, l4, l16 = ls1[rows, :], ls4[rows, :], ls16[rows, :]
        mx = jnp.maximum(l1, jnp.maximum(l4, l16))
        w1, w4, w16 = jnp.exp(l1 - mx), jnp.exp(l4 - mx), jnp.exp(l16 - mx)
        o = (w1 * os1[rows, :] + w4 * os4[rows, :] + w16 * os16[rows, :]) / (w1 + w4 + w16)
        o_ref[rows, :] = (_rms(o) * go_ref[...]).astype(BF16)


def _attn(qkv, out_norm):
    q1, q4, q16, k1, k4, k16, v1, v4, v16 = qkv
    s = q1.shape[0]
    nt = s // ATT_TILE
    blk = ATT_SPAN

    def cur(d):
        return pl.BlockSpec((ATT_TILE // d, d * HEAD_DIM), lambda i, h: (i, h))

    def prev(d):
        per = ATT_TILE // (d * blk)
        return pl.BlockSpec((blk, d * HEAD_DIM), lambda i, h: (jnp.maximum(i * per - 1, 0), h))

    in_specs = ([cur(d) for d in DILATIONS] + [cur(d) for d in DILATIONS] + [prev(d) for d in DILATIONS]
                + [cur(d) for d in DILATIONS] + [prev(d) for d in DILATIONS]
                + [pl.BlockSpec((1, HEAD_DIM), lambda i, h: (0, h))])
    kv_bufs = [pltpu.VMEM((blk + ATT_TILE // d, d * HEAD_DIM), BF16) for d in DILATIONS]
    acc_bufs = [pltpu.VMEM((ATT_TILE, HEAD_DIM), F32) for _ in DILATIONS]
    return pl.pallas_call(
        _attn_kernel,
        grid=(nt, N_HEADS),
        in_specs=in_specs,
        out_specs=pl.BlockSpec((ATT_TILE, HEAD_DIM), lambda i, h: (i, h)),
        out_shape=jax.ShapeDtypeStruct((s, N_HEADS * HEAD_DIM), BF16),
        scratch_shapes=kv_bufs + kv_bufs + acc_bufs + acc_bufs,
        compiler_params=_cparams(("arbitrary", "arbitrary")),
        name="attn",
    )(q1, q4, q16, k1, k4, k16, k1, k4, k16, v1, v4, v16, v1, v4, v16, out_norm)


def _wout_kernel(a1_ref, a2_ref, w_ref, x_ref, g_ref, o_ref):
    half = a1_ref.shape[1]
    acc = jnp.dot(a1_ref[...], w_ref[0:half, :], preferred_element_type=F32)
    acc = acc + jnp.dot(a2_ref[...], w_ref[half:2 * half, :], preferred_element_type=F32)
    o_ref[...] = x_ref[...] + g_ref[...] * acc


def _wout(y_sgu, y_att, w_bf, x, gate):
    s, d = x.shape
    half = y_sgu.shape[1]
    tm, tn = 1024, 1024
    return pl.pallas_call(
        _wout_kernel,
        grid=(s // tm, d // tn),
        in_specs=[pl.BlockSpec((tm, half), lambda i, j: (i, 0)),
                  pl.BlockSpec((tm, half), lambda i, j: (i, 0)),
                  pl.BlockSpec((2 * half, tn), lambda i, j: (0, j)),
                  pl.BlockSpec((tm, tn), lambda i, j: (i, j)),
                  pl.BlockSpec((1, tn), lambda i, j: (0, j))],
        out_specs=pl.BlockSpec((tm, tn), lambda i, j: (i, j)),
        out_shape=jax.ShapeDtypeStruct((s, d), F32),
        compiler_params=_cparams(("arbitrary", "arbitrary")),
        name="wout",
    )(y_sgu, y_att, w_bf, x, gate)


def _router_kernel(x_ref, g_ref, sc_ref, sh_ref, rwt_ref, rb_ref,
                   h_ref, idx_ref, wt_ref, rank_ref, cnt_ref, carry_s):
    tm = x_ref.shape[0]
    gsz = N_EXPERTS // N_GROUPS

    @pl.when(pl.program_id(0) == 0)
    def _():
        carry_s[...] = jnp.zeros_like(carry_s)

    h = (_rms(x_ref[...]) * g_ref[...]) * (1.0 + sc_ref[...]) + sh_ref[...]
    h_ref[...] = h.astype(BF16)

    hh = h.astype(BF16)
    hl = (h - hh.astype(F32)).astype(BF16)
    rw = rwt_ref[...]
    rh = rw.astype(BF16)
    rl = (rw - rh.astype(F32)).astype(BF16)
    dg = functools.partial(lax.dot_general, dimension_numbers=NT_DIMS, preferred_element_type=F32)
    logits = dg(rh, hh) + dg(rl, hh) + dg(rh, hl)
    scores = jax.nn.sigmoid(logits)
    sel = scores + rb_ref[...]

    io8 = lax.broadcasted_iota(I32, (gsz, tm), 0)
    grp, gscore = [], []
    for g in range(N_GROUPS):
        sg = sel[g * gsz:(g + 1) * gsz, :]
        m1 = jnp.max(sg, axis=0, keepdims=True)
        i1 = jnp.min(jnp.where(sg == m1, io8, gsz), axis=0, keepdims=True)
        m2 = jnp.max(jnp.where(io8 == i1, -jnp.inf, sg), axis=0, keepdims=True)
        grp.append(sg)
        gscore.append(m1 + m2)

    parts = []
    for g in range(N_GROUPS):
        beaten = jnp.zeros((1, tm), I32)
        for g2 in range(N_GROUPS):
            if g2 == g:
                continue
            b = (gscore[g2] >= gscore[g]) if g2 < g else (gscore[g2] > gscore[g])
            beaten = beaten + b.astype(I32)
        parts.append(jnp.where(beaten < TOPK_GROUPS, grp[g], -jnp.inf))
    masked = jnp.concatenate(parts, axis=0)

    io = lax.broadcasted_iota(I32, (N_EXPERTS, tm), 0)
    chosen = jnp.zeros((N_EXPERTS, tm), F32)
    idxs, wts = [], []
    for _ in range(TOP_K):
        m = jnp.max(masked, axis=0, keepdims=True)
        am = jnp.min(jnp.where(masked == m, io, N_EXPERTS), axis=0, keepdims=True)
        hit = io == am
        wts.append(jnp.sum(jnp.where(hit, scores, 0.0), axis=0, keepdims=True))
        idxs.append(am)
        chosen = chosen + hit.astype(F32)
        masked = jnp.where(hit, -jnp.inf, masked)
    wsum = wts[0]
    for k in range(1, TOP_K):
        wsum = wsum + wts[k]
    for k in range(TOP_K):
        idx_ref[k:k + 1, :] = idxs[k]
        wt_ref[k:k + 1, :] = wts[k] / wsum * ROUTED_SCALE

    t0 = lax.broadcasted_iota(I32, (tm, tm), 0)
    t1 = lax.broadcasted_iota(I32, (tm, tm), 1)
    before = (t0 < t1).astype(BF16)
    pre = jnp.dot(chosen.astype(BF16), before, preferred_element_type=F32) + carry_s[...]
    for k in range(TOP_K):
        rank_ref[k:k + 1, :] = jnp.sum(jnp.where(io == idxs[k], pre, 0.0), axis=0,
                                       keepdims=True).astype(I32)
    carry_s[...] = carry_s[...] + jnp.sum(chosen, axis=1, keepdims=True)
    cnt_ref[...] = jnp.broadcast_to(carry_s[...], cnt_ref.shape)


def _router(x, gain, sc, sh, rw_t, rbias):
    t, d = x.shape
    tm = 512
    vec = lambda i: (0, 0)
    tok = lambda i: (0, i)
    return pl.pallas_call(
        _router_kernel,
        grid=(t // tm,),
        in_specs=[pl.BlockSpec((tm, d), lambda i: (i, 0)), pl.BlockSpec((1, d), vec),
                  pl.BlockSpec((1, d), vec), pl.BlockSpec((1, d), vec),
                  pl.BlockSpec((N_EXPERTS, d), vec), pl.BlockSpec((N_EXPERTS, 1), vec)],
        out_specs=[pl.BlockSpec((tm, d), lambda i: (i, 0)), pl.BlockSpec((TOP_K, tm), tok),
                   pl.BlockSpec((TOP_K, tm), tok), pl.BlockSpec((TOP_K, tm), tok),
                   pl.BlockSpec((N_EXPERTS, HEAD_DIM), vec)],
        out_shape=[jax.ShapeDtypeStruct((t, d), BF16), jax.ShapeDtypeStruct((TOP_K, t), I32),
                   jax.ShapeDtypeStruct((TOP_K, t), F32), jax.ShapeDtypeStruct((TOP_K, t), I32),
                   jax.ShapeDtypeStruct((N_EXPERTS, HEAD_DIM), F32)],
        scratch_shapes=[pltpu.VMEM((N_EXPERTS, 1), F32)],
        compiler_params=_cparams(("arbitrary",)),
        name="router",
    )(x, gain, sc, sh, rw_t, rbias)


def _dest_kernel(idx_ref, rank_ref, start_ref, o_ref):
    tm = idx_ref.shape[1]
    io = lax.broadcasted_iota(I32, (N_EXPERTS, tm), 0)
    for k in range(TOP_K):
        seg = jnp.sum(jnp.where(io == idx_ref[k:k + 1, :], start_ref[...], 0), axis=0, keepdims=True)
        o_ref[k:k + 1, :] = (seg + rank_ref[k:k + 1, :]) * ROW_SUB


def _dest(idx_t, rank_t, seg_start):
    t = idx_t.shape[1]
    tm = 2048
    tok = lambda i: (0, i)
    return pl.pallas_call(
        _dest_kernel,
        grid=(t // tm,),
        in_specs=[pl.BlockSpec((TOP_K, tm), tok), pl.BlockSpec((TOP_K, tm), tok),
                  pl.BlockSpec((N_EXPERTS, 1), lambda i: (0, 0))],
        out_specs=pl.BlockSpec((TOP_K, tm), tok),
        out_shape=jax.ShapeDtypeStruct((TOP_K, t), I32),
        compiler_params=_cparams(("arbitrary",)),
        name="dest",
    )(idx_t, rank_t, seg_start.reshape(N_EXPERTS, 1))


def _dispatch_kernel(dest_hbm, h_ref, xs_hbm, slot_s, pk_s, sem_idx, sem_row):
    tm, d = h_ref.shape
    n = TOP_K * tm
    base = pl.multiple_of(pl.program_id(0) * n, n)
    idx_copy = pltpu.make_async_copy(dest_hbm.at[pl.ds(base, n)], slot_s, sem_idx)
    idx_copy.start()
    _store_rows(pk_s, 0, tm, h_ref[:, 0:d // 2].astype(F32), h_ref[:, d // 2:d].astype(F32))
    idx_copy.wait()

    def row_copy(j):
        src = pl.multiple_of((j & (tm - 1)) * ROW_SUB, ROW_SUB)
        dst = pl.multiple_of(slot_s[j], ROW_SUB)
        return pltpu.make_async_copy(pk_s.at[pl.ds(src, ROW_SUB)], xs_hbm.at[pl.ds(dst, ROW_SUB)], sem_row)

    def start(i, c):
        row_copy(2 * i).start(priority=0)
        row_copy(2 * i + 1).start(priority=1)
        return c

    def wait(j, c):
        row_copy(j).wait()
        return c

    lax.fori_loop(0, n // 2, start, 0, unroll=4)
    lax.fori_loop(0, n, wait, 0, unroll=8)


def _dispatch(dest_flat, h_bf):
    t, d = h_bf.shape
    tm = TOKEN_TILE
    return pl.pallas_call(
        _dispatch_kernel,
        grid=(t // tm,),
        in_specs=[pl.BlockSpec(memory_space=pl.ANY), pl.BlockSpec((tm, d), lambda i: (i, 0))],
        out_specs=pl.BlockSpec(memory_space=pl.ANY),
        out_shape=jax.ShapeDtypeStruct((t * TOP_K * ROW_SUB, LANES), U32),
        scratch_shapes=[pltpu.SMEM((TOP_K * tm,), I32), pltpu.VMEM((tm * ROW_SUB, LANES), U32),
                        pltpu.SemaphoreType.DMA, pltpu.SemaphoreType.DMA],
        compiler_params=_cparams(("arbitrary",)),
        name="dispatch",
    )(dest_flat, h_bf)


def _moe_kernel(blk_s, wsel_s, lo_s, hi_s, fresh_s, newe_s, x_ref, wg_ref, wu_ref, wd_ref, y_ref,
                acc_s, wg_s, wu_s, wd_s):
    w = pl.program_id(0)
    bm, d = acc_s.shape
    half = d // 2

    @pl.when(w == 0)
    def _():
        acc_s[...] = jnp.zeros_like(acc_s)

    @pl.when(newe_s[w] == 1)
    def _():
        wg_s[...] = wg_ref[...].astype(BF16)
        wu_s[...] = wu_ref[...].astype(BF16)
        wd_s[...] = wd_ref[...].astype(BF16)

    lo, hi = lo_s[w], hi_s[w]

    @pl.when(hi > lo)
    def _():
        xl, xh = _load_rows(x_ref, 0, bm)
        xl, xh = xl.astype(BF16), xh.astype(BF16)
        gate = (jnp.dot(xl, wg_s[0:half, :], preferred_element_type=F32)
                + jnp.dot(xh, wg_s[half:2 * half, :], preferred_element_type=F32))
        up = (jnp.dot(xl, wu_s[0:half, :], preferred_element_type=F32)
              + jnp.dot(xh, wu_s[half:2 * half, :], preferred_element_type=F32))
        hid = (gate * jax.nn.sigmoid(gate) * up).astype(BF16)
        y = jnp.dot(hid, wd_s[...], preferred_element_type=F32)
        rows = lax.broadcasted_iota(I32, (bm, 1), 0)
        keep = (rows >= lo) & (rows < hi)
        acc = jnp.where(fresh_s[w] == 1, 0.0, acc_s[...]) + jnp.where(keep, y, 0.0)
        acc_s[...] = acc
        _store_rows(y_ref, 0, bm, acc[:, 0:half], acc[:, half:2 * half])


def _moe(items, xs, layer, w_gate, w_up, w_down):
    _, _, d, f = w_gate.shape
    n_items = items[0].shape[0]
    bm = MOE_BLOCK
    wmap = lambda w, blk, ex, lo, hi, fi, ne: (layer, ex[w], 0, 0)
    rows = pl.BlockSpec((bm * ROW_SUB, LANES), lambda w, blk, ex, lo, hi, fi, ne: (blk[w], 0))
    grid_spec = pltpu.PrefetchScalarGridSpec(
        num_scalar_prefetch=6,
        grid=(n_items,),
        in_specs=[rows,
                  pl.BlockSpec((None, None, d, f), wmap),
                  pl.BlockSpec((None, None, d, f), wmap),
                  pl.BlockSpec((None, None, f, d), wmap)],
        out_specs=rows,
        scratch_shapes=[pltpu.VMEM((bm, d), F32), pltpu.VMEM((d, f), BF16),
                        pltpu.VMEM((d, f), BF16), pltpu.VMEM((f, d), BF16)])
    return pl.pallas_call(
        _moe_kernel,
        grid_spec=grid_spec,
        out_shape=jax.ShapeDtypeStruct(xs.shape, U32),
        compiler_params=_cparams(("arbitrary",)),
        name="moe",
    )(*items, xs, w_gate, w_up, w_down)


def _work_items(counts, m):
    bm = MOE_BLOCK
    nb = m // bm
    seg_end = jnp.cumsum(counts)
    seg_start = seg_end - counts
    starts = jnp.sort(jnp.concatenate([jnp.arange(nb, dtype=I32) * bm, seg_start]))
    ends = jnp.concatenate([starts[1:], jnp.full((1,), m, I32)])
    length = ends - starts
    blk = jnp.minimum(starts // bm, nb - 1)
    ex = jnp.sum((seg_end[None, :] <= starts[:, None]).astype(I32), axis=1)
    ex = jnp.minimum(ex, N_EXPERTS - 1)
    ex = lax.cummax(jnp.where(length > 0, ex, 0))
    lo = starts - blk * bm
    hi = lo + length
    one = jnp.ones((1,), I32)
    newe = jnp.concatenate([one, (ex[1:] != ex[:-1]).astype(I32)])
    last_blk = lax.cummax(jnp.where(length > 0, blk, -1))
    prev_blk = jnp.concatenate([jnp.full((1,), -1, I32), last_blk[:-1]])
    fresh = ((length > 0) & (blk != prev_blk)).astype(I32)
    cand = jnp.where(newe == 1, ex, N_EXPERTS)
    nxt = jnp.concatenate([lax.cummin(cand[::-1])[::-1][1:], jnp.full((1,), N_EXPERTS, I32)])
    wsel = jnp.where(newe == 1, ex, jnp.where(nxt < N_EXPERTS, nxt, ex))
    return (blk, wsel, lo, hi, fresh, newe), seg_start


def _combine_kernel(dest_hbm, ys_hbm, h_ref, wt_ref, x_ref, g_ref, sg_ref, su_ref, sd_ref, o_ref,
                    slot_s, y_s, sem_idx, sem_row):
    tm, d = h_ref.shape
    half = d // 2
    n = TOP_K * tm
    base = pl.multiple_of(pl.program_id(0) * n, n)
    idx_copy = pltpu.make_async_copy(dest_hbm.at[pl.ds(base, n)], slot_s, sem_idx)
    idx_copy.start()
    idx_copy.wait()

    def row_copy(j):
        src = pl.multiple_of(slot_s[j], ROW_SUB)
        dst = pl.multiple_of(j * ROW_SUB, ROW_SUB)
        return pltpu.make_async_copy(ys_hbm.at[pl.ds(src, ROW_SUB)], y_s.at[pl.ds(dst, ROW_SUB)], sem_row)

    def start(i, c):
        row_copy(2 * i).start(priority=0)
        row_copy(2 * i + 1).start(priority=1)
        return c

    def wait(j, c):
        row_copy(j).wait()
        return c

    lax.fori_loop(0, n // 2, start, 0, unroll=4)

    h = h_ref[...]
    gate = jnp.dot(h, sg_ref[...], preferred_element_type=F32)
    up = jnp.dot(h, su_ref[...], preferred_element_type=F32)
    hid = (gate * jax.nn.sigmoid(gate) * up).astype(BF16)
    shared = jnp.dot(hid, sd_ref[...], preferred_element_type=F32)

    lax.fori_loop(0, n, wait, 0, unroll=8)

    r_lo = jnp.zeros((tm, half), F32)
    r_hi = jnp.zeros((tm, half), F32)
    for k in range(TOP_K):
        lo, hi = _load_rows(y_s, k * tm * ROW_SUB, tm)
        wk = wt_ref[:, k:k + 1]
        r_lo = r_lo + wk * lo
        r_hi = r_hi + wk * hi
    o_ref[:, 0:half] = x_ref[:, 0:half] + g_ref[:, 0:half] * (r_lo + shared[:, 0:half])
    o_ref[:, half:d] = x_ref[:, half:d] + g_ref[:, half:d] * (r_hi + shared[:, half:d])


def _combine(dest_flat, ys, h_bf, wt, x, gate, sg_bf, su_bf, sd_bf):
    t, d = x.shape
    f = sg_bf.shape[1]
    tm = TOKEN_TILE
    row = lambda i: (i, 0)
    vec = lambda i: (0, 0)
    return pl.pallas_call(
        _combine_kernel,
        grid=(t // tm,),
        in_specs=[pl.BlockSpec(memory_space=pl.ANY), pl.BlockSpec(memory_space=pl.ANY),
                  pl.BlockSpec((tm, d), row), pl.BlockSpec((tm, TOP_K), row),
                  pl.BlockSpec((tm, d), row), pl.BlockSpec((1, d), vec),
                  pl.BlockSpec((d, f), vec), pl.BlockSpec((d, f), vec), pl.BlockSpec((f, d), vec)],
        out_specs=pl.BlockSpec((tm, d), row),
        out_shape=jax.ShapeDtypeStruct((t, d), F32),
        scratch_shapes=[pltpu.SMEM((TOP_K * tm,), I32), pltpu.VMEM((TOP_K * tm * ROW_SUB, LANES), U32),
                        pltpu.SemaphoreType.DMA, pltpu.SemaphoreType.DMA],
        compiler_params=_cparams(("arbitrary",)),
        name="combine",
    )(dest_flat, ys, h_bf, wt, x, gate, sg_bf, su_bf, sd_bf)


def _rope_tables(s):
    half = ROPE_DIM // 2
    inv = ROPE_THETA ** (-jnp.arange(half, dtype=F32) * 2.0 / ROPE_DIM)
    ang = jnp.arange(s, dtype=I32).astype(F32)[:, None] * inv[None, :]
    cos, sin = jnp.cos(ang), jnp.sin(ang)
    pad = HEAD_DIM - ROPE_DIM
    rc = jnp.concatenate([cos, cos, jnp.ones((s, pad), F32)], axis=1)
    ra = jnp.concatenate([jnp.zeros((s, half), F32), sin, jnp.zeros((s, pad), F32)], axis=1)
    rb = jnp.concatenate([-sin, jnp.zeros((s, half + pad), F32)], axis=1)
    return rc, ra, rb


def _mixer(x, mod, rope, mix_norm, w_in, sgu_norm, sgu_w, sgu_b, q_norm, k_norm,
           out_norm_sgu, out_norm_att, w_out):
    d = x.shape[1]
    sh_m, sc_m, g_m = mod[:, 0:d], mod[:, d:2 * d], mod[:, 2 * d:3 * d]
    outs = _proj(x, mix_norm.reshape(1, d), sc_m, sh_m, w_in.astype(BF16),
                 q_norm.reshape(1, HEAD_DIM), k_norm.reshape(1, HEAD_DIM), rope)
    y_sgu = _sgu(outs[0], sgu_norm.reshape(1, -1), sgu_w, sgu_b.T, out_norm_sgu.reshape(1, -1))
    y_att = _attn(outs[1:], out_norm_att.reshape(1, -1))
    return _wout(y_sgu, y_att, w_out.astype(BF16), x, g_m)


def _ffn(x, mod, layer, ffn_norm, router_w, router_bias, exp_gate, exp_up, exp_down,
         shared_gate, shared_up, shared_down):
    t, d = x.shape
    sh_f, sc_f, g_f = mod[:, 3 * d:4 * d], mod[:, 4 * d:5 * d], mod[:, 5 * d:6 * d]
    h_bf, idx_t, wt_t, rank_t, cnt = _router(x, ffn_norm.reshape(1, d), sc_f, sh_f, router_w.T,
                                             router_bias.reshape(N_EXPERTS, 1))
    counts = cnt[:, 0].astype(I32)
    items, seg_start = _work_items(counts, t * TOP_K)
    dest_t = _dest(idx_t, rank_t, seg_start)
    nt = t // TOKEN_TILE
    dest_flat = dest_t.reshape(TOP_K, nt, TOKEN_TILE).transpose(1, 0, 2).reshape(-1)
    xs = _dispatch(dest_flat, h_bf)
    ys = _moe(items, xs, layer, exp_gate, exp_up, exp_down)
    return _combine(dest_flat, ys, h_bf, wt_t.T, x, g_f, shared_gate.astype(BF16),
                    shared_up.astype(BF16), shared_down.astype(BF16))


def kernel(x, c, ada_w, ada_b, mix_norm, w_in, sgu_norm, sgu_w, sgu_b, q_norm, k_norm, out_norm_sgu,
           out_norm_att, w_out, ffn_norm, router_w, router_bias, exp_gate, exp_up, exp_down,
           shared_gate, shared_up, shared_down):
    b, s, d = x.shape
    assert b == 1 and s % ATT_TILE == 0 and d == 2 * ROW_SUB * LANES
    mods = _ada(c, ada_w, ada_b)
    rope = _rope_tables(s)
    xf = x.reshape(s, d)
    for l in range(ada_w.shape[0]):
        xf = _mixer(xf, mods[l], rope, mix_norm[l], w_in[l], sgu_norm[l], sgu_w[l], sgu_b[l], q_norm[l],
                    k_norm[l], out_norm_sgu[l], out_norm_att[l], w_out[l])
        xf = _ffn(xf, mods[l], l, ffn_norm[l], router_w[l], router_bias[l], exp_gate, exp_up,
                  exp_down, shared_gate[l], shared_up[l], shared_down[l])
    return xf.reshape(b, s, d)
```

```python
import functools

import jax
import jax.numpy as jnp
from jax import lax
from jax.experimental import pallas as pl
from jax.experimental.pallas import tpu as pltpu

F32 = jnp.float32
BF16 = jnp.bfloat16
U32 = jnp.uint32
I32 = jnp.int32

HEAD_DIM = 128
N_HEADS = 8
SGU_CHUNK = 128
DILATIONS = (1, 4, 16)
ATT_SPAN = 128
ATT_TILE = 2048
ROPE_THETA = 500000.0
ROPE_DIM = HEAD_DIM // 4
N_EXPERTS = 64
TOP_K = 8
N_GROUPS = 8
TOPK_GROUPS = 4
ROUTED_SCALE = 2.5
NORM_EPS = 1e-6
N_MOD = 6

MOE_BLOCK = 256
TOKEN_TILE = 256
HI_MASK = 0xFFFF0000
LANES = 128
ROW_SUB = 8

VMEM_LIMIT = 56 * 1024 * 1024

NT_DIMS = (((1,), (1,)), ((), ()))


def _cparams(sem):
    return pltpu.CompilerParams(dimension_semantics=sem, vmem_limit_bytes=VMEM_LIMIT)


def _rms(x):
    return x * lax.rsqrt(jnp.mean(x * x, axis=-1, keepdims=True) + NORM_EPS)


def _gelu(x):
    return 0.5 * x * (1.0 + lax.erf(x * 0.7071067811865476))


def _pack_pair(lo_f32, hi_f32):
    lo = pltpu.bitcast(lo_f32.astype(BF16).astype(F32), U32)
    hi = pltpu.bitcast(hi_f32.astype(BF16).astype(F32), U32)
    return (lo >> 16) | (hi & jnp.uint32(HI_MASK))


def _unpack_pair(u):
    lo = pltpu.bitcast(u << 16, F32)
    hi = pltpu.bitcast(u & jnp.uint32(HI_MASK), F32)
    return lo, hi


def _store_rows(ref, base, n, lo_f32, hi_f32):
    for c in range(ROW_SUB):
        cols = slice(c * LANES, (c + 1) * LANES)
        ref[pl.ds(base + c, n, stride=ROW_SUB), :] = _pack_pair(lo_f32[:, cols], hi_f32[:, cols])


def _load_rows(ref, base, n):
    los, his = [], []
    for c in range(ROW_SUB):
        lo, hi = _unpack_pair(ref[pl.ds(base + c, n, stride=ROW_SUB), :])
        los.append(lo)
        his.append(hi)
    return jnp.concatenate(los, axis=1), jnp.concatenate(his, axis=1)


def _ada_kernel(c_ref, w_ref, b_ref, o_ref):
    d, tn = w_ref.shape

    def body(i, acc):
        r = pl.multiple_of(i * 8, 8)
        cc = c_ref[pl.ds(r, 8), :]
        return acc + (cc * jax.nn.sigmoid(cc)) * w_ref[pl.ds(r, 8), :]

    acc = lax.fori_loop(0, d // 8, body, jnp.zeros((8, tn), F32), unroll=8)
    o_ref[...] = jnp.sum(acc, axis=0, keepdims=True) + b_ref[...]


def _ada(c, ada_w, ada_b):
    nl, d, n = ada_w.shape
    tn = 1536
    return pl.pallas_call(
        _ada_kernel,
        grid=(nl, n // tn),
        in_specs=[pl.BlockSpec((d, 1), lambda l, j: (0, 0)),
                  pl.BlockSpec((None, d, tn), lambda l, j: (l, 0, j)),
                  pl.BlockSpec((None, 1, tn), lambda l, j: (l, 0, j))],
        out_specs=pl.BlockSpec((None, 1, tn), lambda l, j: (l, 0, j)),
        out_shape=jax.ShapeDtypeStruct((nl, 1, n), F32),
        compiler_params=_cparams(("arbitrary", "arbitrary")),
        name="ada",
    )(c.reshape(d, 1), ada_w, ada_b.reshape(nl, 1, n))


def _proj_kernel(x_ref, g_ref, sc_ref, sh_ref, w_ref, qg_ref, kg_ref, rc_ref, ra_ref, rb_ref,
                 uv_ref, q1_ref, q4_ref, q16_ref, k1_ref, k4_ref, k16_ref, v1_ref, v4_ref, v16_ref,
                 h_s, st_s):
    tm = x_ref.shape[0]
    pair = 2 * HEAD_DIM
    aw = N_HEADS * HEAD_DIM
    y = _rms(x_ref[...]) * g_ref[...]
    h_s[...] = (y * (1.0 + sc_ref[...]) + sh_ref[...]).astype(BF16)

    for p in range(uv_ref.shape[1] // pair):
        cols = slice(p * pair, (p + 1) * pair)
        uv_ref[:, cols] = jnp.dot(h_s[...], w_ref[:, cols], preferred_element_type=F32).astype(BF16)

    groups = ((q1_ref, q4_ref, q16_ref, qg_ref, HEAD_DIM ** -0.5),
              (k1_ref, k4_ref, k16_ref, kg_ref, 1.0),
              (v1_ref, v4_ref, v16_ref, None, 1.0))
    for gi, (o1, o4, o16, gain_ref, scale) in enumerate(groups):
        for p in range(N_HEADS // 2):
            c0 = uv_ref.shape[1] + gi * aw + p * pair
            acc = jnp.dot(h_s[...], w_ref[:, c0:c0 + pair], preferred_element_type=F32)
            for hh in range(2):
                h = 2 * p + hh
                a = acc[:, hh * HEAD_DIM:(hh + 1) * HEAD_DIM]
                if gain_ref is not None:
                    a = _rms(a) * gain_ref[...]
                    a = (a * rc_ref[...] + pltpu.roll(a, ROPE_DIM // 2, 1) * ra_ref[...]
                         + pltpu.roll(a, HEAD_DIM - ROPE_DIM // 2, 1) * rb_ref[...]) * scale
                o1[:, h * HEAD_DIM:(h + 1) * HEAD_DIM] = a.astype(BF16)
                stage = st_s.at[gi * N_HEADS + h]
                stage[...] = a
                for d, o in ((4, o4), (16, o16)):
                    for r in range(d):
                        c = (h * d + r) * HEAD_DIM
                        o[:, c:c + HEAD_DIM] = stage[pl.ds(r, tm // d, stride=d), :].astype(BF16)


def _proj(x, gain, sc, sh, w_bf, q_gain, k_gain, rope):
    s, d = x.shape
    n = w_bf.shape[1]
    tm = 256
    aw = N_HEADS * HEAD_DIM
    row = lambda i: (i, 0)
    vec = lambda i: (0, 0)
    out_shape = [jax.ShapeDtypeStruct((s, 2 * aw), BF16)]
    out_specs = [pl.BlockSpec((tm, 2 * aw), row)]
    for _ in range(3):
        for dil in DILATIONS:
            out_shape.append(jax.ShapeDtypeStruct((s // dil, dil * aw), BF16))
            out_specs.append(pl.BlockSpec((tm // dil, dil * aw), row))
    return pl.pallas_call(
        _proj_kernel,
        grid=(s // tm,),
        in_specs=[pl.BlockSpec((tm, d), row), pl.BlockSpec((1, d), vec), pl.BlockSpec((1, d), vec),
                  pl.BlockSpec((1, d), vec),
                  pl.BlockSpec((d, n), vec, pipeline_mode=pl.Buffered(1)),
                  pl.BlockSpec((1, HEAD_DIM), vec), pl.BlockSpec((1, HEAD_DIM), vec),
                  pl.BlockSpec((tm, HEAD_DIM), row), pl.BlockSpec((tm, HEAD_DIM), row),
                  pl.BlockSpec((tm, HEAD_DIM), row)],
        out_specs=out_specs,
        out_shape=out_shape,
        scratch_shapes=[pltpu.VMEM((tm, d), BF16), pltpu.VMEM((3 * N_HEADS, tm, HEAD_DIM), F32)],
        compiler_params=_cparams(("arbitrary",)),
        name="proj",
    )(x, gain, sc, sh, w_bf, q_gain, k_gain, *rope)


def _sgu_kernel(uv_ref, gn_ref, w_ref, b_ref, go_ref, o_ref):
    tm = uv_ref.shape[0]
    width = N_HEADS * HEAD_DIM
    ii = lax.broadcasted_iota(I32, (SGU_CHUNK, SGU_CHUNK), 0)
    jj = lax.broadcasted_iota(I32, (SGU_CHUNK, SGU_CHUNK), 1)
    causal = jj <= ii
    for g in range(N_HEADS):
        cols = slice(g * HEAD_DIM, (g + 1) * HEAD_DIM)
        wg = jnp.where(causal, w_ref[g], 0.0).astype(BF16)
        u = _gelu(uv_ref[:, cols].astype(F32))
        v = _gelu(uv_ref[:, width + g * HEAD_DIM:width + (g + 1) * HEAD_DIM].astype(F32))
        vn = (_rms(v) * gn_ref[:, cols]).astype(BF16)
        bias = b_ref[:, g:g + 1]
        for n in range(tm // SGU_CHUNK):
            rows = slice(n * SGU_CHUNK, (n + 1) * SGU_CHUNK)
            mixed = jnp.dot(wg, vn[rows], preferred_element_type=F32) + bias
            y = u[rows] * mixed
            o_ref[rows, cols] = (_rms(y) * go_ref[:, cols]).astype(BF16)


def _sgu(uv, sgu_norm, sgu_w, sgu_b_t, out_norm):
    s = uv.shape[0]
    width = N_HEADS * HEAD_DIM
    tm = 512
    vec = lambda i: (0, 0)
    return pl.pallas_call(
        _sgu_kernel,
        grid=(s // tm,),
        in_specs=[pl.BlockSpec((tm, 2 * width), lambda i: (i, 0)),
                  pl.BlockSpec((1, width), vec),
                  pl.BlockSpec((N_HEADS, SGU_CHUNK, SGU_CHUNK), lambda i: (0, 0, 0)),
                  pl.BlockSpec((SGU_CHUNK, N_HEADS), vec),
                  pl.BlockSpec((1, width), vec)],
        out_specs=pl.BlockSpec((tm, width), lambda i: (i, 0)),
        out_shape=jax.ShapeDtypeStruct((s, width), BF16),
        compiler_params=_cparams(("arbitrary",)),
        name="sgu",
    )(uv, sgu_norm, sgu_w, sgu_b_t, out_norm)


def _attn_kernel(q1, q4, q16, k1, k4, k16, k1p, k4p, k16p, v1, v4, v16, v1p, v4p, v16p, go_ref,
                 o_ref, kb1, kb4, kb16, vb1, vb4, vb16, os1, os4, os16, ls1, ls4, ls16):
    first = pl.program_id(0) == 0
    blk = ATT_SPAN
    for buf, prev, cur in ((kb1, k1p, k1), (kb4, k4p, k4), (kb16, k16p, k16),
                           (vb1, v1p, v1), (vb4, v4p, v4), (vb16, v16p, v16)):
        buf[0:blk, :] = prev[...]
        buf[blk:, :] = cur[...]

    qi = lax.broadcasted_iota(I32, (blk, 2 * blk), 0)
    kj = lax.broadcasted_iota(I32, (blk, 2 * blk), 1)
    dist = qi + blk - kj
    band = (dist >= 0) & (dist <= ATT_SPAN)
    neg = jnp.where(band, 0.0, -jnp.inf)
    neg0 = jnp.where(first, jnp.where(band & (kj >= blk), 0.0, -jnp.inf), neg)

    for d, q_ref, kb, vb, o_s, l_s in ((1, q1, kb1, vb1, os1, ls1), (4, q4, kb4, vb4, os4, ls4),
                                       (16, q16, kb16, vb16, os16, ls16)):
        nb = ATT_TILE // (blk * d)
        for r in range(d):
            cols = slice(r * HEAD_DIM, (r + 1) * HEAD_DIM)
            for b in range(nb):
                q = q_ref[b * blk:(b + 1) * blk, cols]
                kk = kb[b * blk:(b + 2) * blk, cols]
                vv = vb[b * blk:(b + 2) * blk, cols]
                s = lax.dot_general(q, kk, NT_DIMS, preferred_element_type=F32)
                s = s + (neg0 if b == 0 else neg)
                m = jnp.max(s, axis=-1, keepdims=True)
                e = jnp.exp(s - m)
                l = jnp.sum(e, axis=-1, keepdims=True)
                o = jnp.dot(e.astype(BF16), vv, preferred_element_type=F32) / l
                lse = jnp.broadcast_to(m + jnp.log(l), (blk, HEAD_DIM))
                if d == 1:
                    o_s[b * blk:(b + 1) * blk, :] = o
                    l_s[b * blk:(b + 1) * blk, :] = lse
                else:
                    o_s[pl.ds(d * b * blk + r, blk, stride=d), :] = o
                    l_s[pl.ds(d * b * blk + r, blk, stride=d), :] = lse

    step = 256
    for c in range(ATT_TILE // step):
        rows = slice(c * step, (c + 1) * step)
        l1, l4, l16 = ls1[rows, :], ls4[rows, :], ls16[rows, :]
        mx = jnp.maximum(l1, jnp.maximum(l4, l16))
        w1, w4, w16 = jnp.exp(l1 - mx), jnp.exp(l4 - mx), jnp.exp(l16 - mx)
        o = (w1 * os1[rows, :] + w4 * os4[rows, :] + w16 * os16[rows, :]) / (w1 + w4 + w16)
        o_ref[rows, :] = (_rms(o) * go_ref[...]).astype(BF16)


def _attn(qkv, out_norm):
    q1, q4, q16, k1, k4, k16, v1, v4, v16 = qkv
    s = q1.shape[0]
    nt = s // ATT_TILE
    blk = ATT_SPAN

    def cur(d):
        return pl.BlockSpec((ATT_TILE // d, d * HEAD_DIM), lambda i, h: (i, h))

    def prev(d):
        per = ATT_TILE // (d * blk)
        return pl.BlockSpec((blk, d * HEAD_DIM), lambda i, h: (jnp.maximum(i * per - 1, 0), h))

    in_specs = ([cur(d) for d in DILATIONS] + [cur(d) for d in DILATIONS] + [prev(d) for d in DILATIONS]
                + [cur(d) for d in DILATIONS] + [prev(d) for d in DILATIONS]
                + [pl.BlockSpec((1, HEAD_DIM), lambda i, h: (0, h))])
    kv_bufs = [pltpu.VMEM((blk + ATT_TILE // d, d * HEAD_DIM), BF16) for d in DILATIONS]
    acc_bufs = [pltpu.VMEM((ATT_TILE, HEAD_DIM), F32) for _ in DILATIONS]
    return pl.pallas_call(
        _attn_kernel,
        grid=(nt, N_HEADS),
        in_specs=in_specs,
        out_specs=pl.BlockSpec((ATT_TILE, HEAD_DIM), lambda i, h: (i, h)),
        out_shape=jax.ShapeDtypeStruct((s, N_HEADS * HEAD_DIM), BF16),
        scratch_shapes=kv_bufs + kv_bufs + acc_bufs + acc_bufs,
        compiler_params=_cparams(("arbitrary", "arbitrary")),
        name="attn",
    )(q1, q4, q16, k1, k4, k16, k1, k4, k16, v1, v4, v16, v1, v4, v16, out_norm)


def _wout_kernel(a1_ref, a2_ref, w_ref, x_ref, g_ref, o_ref):
    half = a1_ref.shape[1]
    acc = jnp.dot(a1_ref[...], w_ref[0:half, :], preferred_element_type=F32)
    acc = acc + jnp.dot(a2_ref[...], w_ref[half:2 * half, :], preferred_element_type=F32)
    o_ref[...] = x_ref[...] + g_ref[...] * acc


def _wout(y_sgu, y_att, w_bf, x, gate):
    s, d = x.shape
    half = y_sgu.shape[1]
    tm, tn = 1024, 1024
    return pl.pallas_call(
        _wout_kernel,
        grid=(s // tm, d // tn),
        in_specs=[pl.BlockSpec((tm, half), lambda i, j: (i, 0)),
                  pl.BlockSpec((tm, half), lambda i, j: (i, 0)),
                  pl.BlockSpec((2 * half, tn), lambda i, j: (0, j)),
                  pl.BlockSpec((tm, tn), lambda i, j: (i, j)),
                  pl.BlockSpec((1, tn), lambda i, j: (0, j))],
        out_specs=pl.BlockSpec((tm, tn), lambda i, j: (i, j)),
        out_shape=jax.ShapeDtypeStruct((s, d), F32),
        compiler_params=_cparams(("arbitrary", "arbitrary")),
        name="wout",
    )(y_sgu, y_att, w_bf, x, gate)


def _router_kernel(x_ref, g_ref, sc_ref, sh_ref, rwt_ref, rb_ref,
                   h_ref, idx_ref, wt_ref, rank_ref, cnt_ref, carry_s):
    tm = x_ref.shape[0]
    gsz = N_EXPERTS // N_GROUPS

    @pl.when(pl.program_id(0) == 0)
    def _():
        carry_s[...] = jnp.zeros_like(carry_s)

    h = (_rms(x_ref[...]) * g_ref[...]) * (1.0 + sc_ref[...]) + sh_ref[...]
    h_ref[...] = h.astype(BF16)

    hh = h.astype(BF16)
    hl = (h - hh.astype(F32)).astype(BF16)
    rw = rwt_ref[...]
    rh = rw.astype(BF16)
    rl = (rw - rh.astype(F32)).astype(BF16)
    dg = functools.partial(lax.dot_general, dimension_numbers=NT_DIMS, preferred_element_type=F32)
    logits = dg(rh, hh) + dg(rl, hh) + dg(rh, hl)
    scores = jax.nn.sigmoid(logits)
    sel = scores + rb_ref[...]

    io8 = lax.broadcasted_iota(I32, (gsz, tm), 0)
    grp, gscore = [], []
    for g in range(N_GROUPS):
        sg = sel[g * gsz:(g + 1) * gsz, :]
        m1 = jnp.max(sg, axis=0, keepdims=True)
        i1 = jnp.min(jnp.where(sg == m1, io8, gsz), axis=0, keepdims=True)
        m2 = jnp.max(jnp.where(io8 == i1, -jnp.inf, sg), axis=0, keepdims=True)
        grp.append(sg)
        gscore.append(m1 + m2)

    parts = []
    for g in range(N_GROUPS):
        beaten = jnp.zeros((1, tm), I32)
        for g2 in range(N_GROUPS):
            if g2 == g:
                continue
            b = (gscore[g2] >= gscore[g]) if g2 < g else (gscore[g2] > gscore[g])
            beaten = beaten + b.astype(I32)
        parts.append(jnp.where(beaten < TOPK_GROUPS, grp[g], -jnp.inf))
    masked = jnp.concatenate(parts, axis=0)

    io = lax.broadcasted_iota(I32, (N_EXPERTS, tm), 0)
    chosen = jnp.zeros((N_EXPERTS, tm), F32)
    idxs, wts = [], []
    for _ in range(TOP_K):
        m = jnp.max(masked, axis=0, keepdims=True)
        am = jnp.min(jnp.where(masked == m, io, N_EXPERTS), axis=0, keepdims=True)
        hit = io == am
        wts.append(jnp.sum(jnp.where(hit, scores, 0.0), axis=0, keepdims=True))
        idxs.append(am)
        chosen = chosen + hit.astype(F32)
        masked = jnp.where(hit, -jnp.inf, masked)
    wsum = wts[0]
    for k in range(1, TOP_K):
        wsum = wsum + wts[k]
    for k in range(TOP_K):
        idx_ref[k:k + 1, :] = idxs[k]
        wt_ref[k:k + 1, :] = wts[k] / wsum * ROUTED_SCALE

    t0 = lax.broadcasted_iota(I32, (tm, tm), 0)
    t1 = lax.broadcasted_iota(I32, (tm, tm), 1)
    before = (t0 < t1).astype(BF16)
    pre = jnp.dot(chosen.astype(BF16), before, preferred_element_type=F32) + carry_s[...]
    for k in range(TOP_K):
        rank_ref[k:k + 1, :] = jnp.sum(jnp.where(io == idxs[k], pre, 0.0), axis=0,
                                       keepdims=True).astype(I32)
    carry_s[...] = carry_s[...] + jnp.sum(chosen, axis=1, keepdims=True)
    cnt_ref[...] = jnp.broadcast_to(carry_s[...], cnt_ref.shape)


def _router(x, gain, sc, sh, rw_t, rbias):
    t, d = x.shape
    tm = 512
    vec = lambda i: (0, 0)
    tok = lambda i: (0, i)
    return pl.pallas_call(
        _router_kernel,
        grid=(t // tm,),
        in_specs=[pl.BlockSpec((tm, d), lambda i: (i, 0)), pl.BlockSpec((1, d), vec),
                  pl.BlockSpec((1, d), vec), pl.BlockSpec((1, d), vec),
                  pl.BlockSpec((N_EXPERTS, d), vec), pl.BlockSpec((N_EXPERTS, 1), vec)],
        out_specs=[pl.BlockSpec((tm, d), lambda i: (i, 0)), pl.BlockSpec((TOP_K, tm), tok),
                   pl.BlockSpec((TOP_K, tm), tok), pl.BlockSpec((TOP_K, tm), tok),
                   pl.BlockSpec((N_EXPERTS, HEAD_DIM), vec)],
        out_shape=[jax.ShapeDtypeStruct((t, d), BF16), jax.ShapeDtypeStruct((TOP_K, t), I32),
                   jax.ShapeDtypeStruct((TOP_K, t), F32), jax.ShapeDtypeStruct((TOP_K, t), I32),
                   jax.ShapeDtypeStruct((N_EXPERTS, HEAD_DIM), F32)],
        scratch_shapes=[pltpu.VMEM((N_EXPERTS, 1), F32)],
        compiler_params=_cparams(("arbitrary",)),
        name="router",
    )(x, gain, sc, sh, rw_t, rbias)


def _dest_kernel(idx_ref, rank_ref, start_ref, o_ref):
    tm = idx_ref.shape[1]
    io = lax.broadcasted_iota(I32, (N_EXPERTS, tm), 0)
    for k in range(TOP_K):
        seg = jnp.sum(jnp.where(io == idx_ref[k:k + 1, :], start_ref[...], 0), axis=0, keepdims=True)
        o_ref[k:k + 1, :] = (seg + rank_ref[k:k + 1, :]) * ROW_SUB


def _dest(idx_t, rank_t, seg_start):
    t = idx_t.shape[1]
    tm = 2048
    tok = lambda i: (0, i)
    return pl.pallas_call(
        _dest_kernel,
        grid=(t // tm,),
        in_specs=[pl.BlockSpec((TOP_K, tm), tok), pl.BlockSpec((TOP_K, tm), tok),
                  pl.BlockSpec((N_EXPERTS, 1), lambda i: (0, 0))],
        out_specs=pl.BlockSpec((TOP_K, tm), tok),
        out_shape=jax.ShapeDtypeStruct((TOP_K, t), I32),
        compiler_params=_cparams(("arbitrary",)),
        name="dest",
    )(idx_t, rank_t, seg_start.reshape(N_EXPERTS, 1))


def _dispatch_kernel(dest_s, h_ref, sg_ref, su_ref, sd_ref, xs_hbm, sh_ref, pk_s, sem_row):
    tm, d = h_ref.shape
    n = TOP_K * tm
    i = pl.program_id(0)
    last = pl.num_programs(0) - 1
    cur = i % 2
    base = i * n

    def row_copy(src_row, dst_row, buf):
        return pltpu.make_async_copy(pk_s.at[buf, pl.ds(src_row, ROW_SUB)],
                                     xs_hbm.at[pl.ds(dst_row, ROW_SUB)], sem_row.at[buf])

    def wait_all(buf):
        def wait(j, c):
            row_copy(0, 0, buf).wait()
            return c
        lax.fori_loop(0, n, wait, 0, unroll=8)

    @pl.when(i >= 2)
    def _():
        wait_all(cur)

    _store_rows(pk_s.at[cur], 0, tm, h_ref[:, 0:d // 2].astype(F32), h_ref[:, d // 2:d].astype(F32))

    vals = {}

    def p_gate():
        vals["gate"] = jnp.dot(h_ref[...], sg_ref[...], preferred_element_type=F32)

    def p_up():
        up = jnp.dot(h_ref[...], su_ref[...], preferred_element_type=F32)
        g = vals["gate"]
        vals["hid"] = (g * jax.nn.sigmoid(g) * up).astype(BF16)

    def p_shared(m):
        def run():
            cols = slice(m * 2 * LANES, (m + 1) * 2 * LANES)
            sh_ref[:, cols] = jnp.dot(vals["hid"], sd_ref[:, cols], preferred_element_type=F32)
        return run

    pieces = [p_gate, p_up] + [p_shared(m) for m in range(d // (2 * LANES))]
    j = 0
    for idx, piece in enumerate(pieces):
        for _ in range(n // len(pieces) + (1 if idx < n % len(pieces) else 0)):
            row_copy((j % tm) * ROW_SUB, pl.multiple_of(dest_s[base + j], ROW_SUB), cur).start(priority=j % 2)
            j += 1
        piece()

    @pl.when(i == last)
    def _():
        wait_all(cur)

    @pl.when((i == last) & (i >= 1))
    def _():
        wait_all(1 - cur)


def _dispatch(dest_flat, h_bf, sg_bf, su_bf, sd_bf):
    t, d = h_bf.shape
    f = sg_bf.shape[1]
    tm = TOKEN_TILE
    row = lambda i, dest: (i, 0)
    vec = lambda i, dest: (0, 0)
    grid_spec = pltpu.PrefetchScalarGridSpec(
        num_scalar_prefetch=1,
        grid=(t // tm,),
        in_specs=[pl.BlockSpec((tm, d), row),
                  pl.BlockSpec((d, f), vec), pl.BlockSpec((d, f), vec), pl.BlockSpec((f, d), vec)],
        out_specs=[pl.BlockSpec(memory_space=pl.ANY), pl.BlockSpec((tm, d), row)],
        scratch_shapes=[pltpu.VMEM((2, tm * ROW_SUB, LANES), U32), pltpu.SemaphoreType.DMA((2,))])
    return pl.pallas_call(
        _dispatch_kernel,
        grid_spec=grid_spec,
        out_shape=[jax.ShapeDtypeStruct((t * TOP_K * ROW_SUB, LANES), U32),
                   jax.ShapeDtypeStruct((t, d), F32)],
        compiler_params=_cparams(("arbitrary",)),
        name="dispatch",
    )(dest_flat, h_bf, sg_bf, su_bf, sd_bf)


def _moe_kernel(blk_s, wsel_s, lo_s, hi_s, fresh_s, newe_s, x_ref, wg_ref, wu_ref, wd_ref, y_ref,
                acc_s, wg_s, wu_s, wd_s):
    w = pl.program_id(0)
    bm, d = acc_s.shape
    half = d // 2

    @pl.when(w == 0)
    def _():
        acc_s[...] = jnp.zeros_like(acc_s)

    @pl.when(newe_s[w] == 1)
    def _():
        wg_s[...] = wg_ref[...].astype(BF16)
        wu_s[...] = wu_ref[...].astype(BF16)
        wd_s[...] = wd_ref[...].astype(BF16)

    lo, hi = lo_s[w], hi_s[w]

    @pl.when(hi > lo)
    def _():
        xl, xh = _load_rows(x_ref, 0, bm)
        xl, xh = xl.astype(BF16), xh.astype(BF16)
        gate = (jnp.dot(xl, wg_s[0:half, :], preferred_element_type=F32)
                + jnp.dot(xh, wg_s[half:2 * half, :], preferred_element_type=F32))
        up = (jnp.dot(xl, wu_s[0:half, :], preferred_element_type=F32)
              + jnp.dot(xh, wu_s[half:2 * half, :], preferred_element_type=F32))
        hid = (gate * jax.nn.sigmoid(gate) * up).astype(BF16)
        y = jnp.dot(hid, wd_s[...], preferred_element_type=F32)
        rows = lax.broadcasted_iota(I32, (bm, 1), 0)
        keep = (rows >= lo) & (rows < hi)
        acc = jnp.where(fresh_s[w] == 1, 0.0, acc_s[...]) + jnp.where(keep, y, 0.0)
        acc_s[...] = acc
        _store_rows(y_ref, 0, bm, acc[:, 0:half], acc[:, half:2 * half])


def _moe(items, xs, layer, w_gate, w_up, w_down):
    _, _, d, f = w_gate.shape
    n_items = items[0].shape[0]
    bm = MOE_BLOCK
    wmap = lambda w, blk, ex, lo, hi, fi, ne: (layer, ex[w], 0, 0)
    rows = pl.BlockSpec((bm * ROW_SUB, LANES), lambda w, blk, ex, lo, hi, fi, ne: (blk[w], 0))
    grid_spec = pltpu.PrefetchScalarGridSpec(
        num_scalar_prefetch=6,
        grid=(n_items,),
        in_specs=[rows,
                  pl.BlockSpec((None, None, d, f), wmap),
                  pl.BlockSpec((None, None, d, f), wmap),
                  pl.BlockSpec((None, None, f, d), wmap)],
        out_specs=rows,
        scratch_shapes=[pltpu.VMEM((bm, d), F32), pltpu.VMEM((d, f), BF16),
                        pltpu.VMEM((d, f), BF16), pltpu.VMEM((f, d), BF16)])
    return pl.pallas_call(
        _moe_kernel,
        grid_spec=grid_spec,
        out_shape=jax.ShapeDtypeStruct(xs.shape, U32),
        compiler_params=_cparams(("arbitrary",)),
        name="moe",
    )(*items, xs, w_gate, w_up, w_down)


def _work_items(counts, m):
    bm = MOE_BLOCK
    nb = m // bm
    seg_end = jnp.cumsum(counts)
    seg_start = seg_end - counts
    starts = jnp.sort(jnp.concatenate([jnp.arange(nb, dtype=I32) * bm, seg_start]))
    ends = jnp.concatenate([starts[1:], jnp.full((1,), m, I32)])
    length = ends - starts
    blk = jnp.minimum(starts // bm, nb - 1)
    ex = jnp.sum((seg_end[None, :] <= starts[:, None]).astype(I32), axis=1)
    ex = jnp.minimum(ex, N_EXPERTS - 1)
    ex = lax.cummax(jnp.where(length > 0, ex, 0))
    lo = starts - blk * bm
    hi = lo + length
    one = jnp.ones((1,), I32)
    newe = jnp.concatenate([one, (ex[1:] != ex[:-1]).astype(I32)])
    last_blk = lax.cummax(jnp.where(length > 0, blk, -1))
    prev_blk = jnp.concatenate([jnp.full((1,), -1, I32), last_blk[:-1]])
    fresh = ((length > 0) & (blk != prev_blk)).astype(I32)
    cand = jnp.where(newe == 1, ex, N_EXPERTS)
    nxt = jnp.concatenate([lax.cummin(cand[::-1])[::-1][1:], jnp.full((1,), N_EXPERTS, I32)])
    wsel = jnp.where(newe == 1, ex, jnp.where(nxt < N_EXPERTS, nxt, ex))
    return (blk, wsel, lo, hi, fresh, newe), seg_start


def _combine_kernel(dest_s, ys_hbm, sh_ref, wt_ref, x_ref, g_ref, o_ref, y_s, sem_row):
    tm, d = x_ref.shape
    half = d // 2
    n = TOP_K * tm
    i = pl.program_id(0)
    last = pl.num_programs(0) - 1
    cur = i % 2
    nxt = 1 - cur
    base_next = jnp.minimum(i + 1, last) * n

    def row_copy(src_row, dst_row, buf):
        return pltpu.make_async_copy(ys_hbm.at[pl.ds(src_row, ROW_SUB)],
                                     y_s.at[buf, pl.ds(dst_row, ROW_SUB)], sem_row.at[buf])

    def wait_all(buf):
        def wait(j, c):
            row_copy(0, 0, buf).wait()
            return c
        lax.fori_loop(0, n, wait, 0, unroll=8)

    @pl.when(i == 0)
    def _():
        def start(j, c):
            row_copy(pl.multiple_of(dest_s[j], ROW_SUB), pl.multiple_of(j * ROW_SUB, ROW_SUB), 0).start()
            return c
        lax.fori_loop(0, n, start, 0, unroll=8)

    wait_all(cur)

    def p_routed(c):
        def run():
            r_lo = jnp.zeros((tm, LANES), F32)
            r_hi = jnp.zeros((tm, LANES), F32)
            for k in range(TOP_K):
                lo, hi = _unpack_pair(y_s[cur, pl.ds(k * tm * ROW_SUB + c, tm, stride=ROW_SUB), :])
                wk = wt_ref[:, k:k + 1]
                r_lo = r_lo + wk * lo
                r_hi = r_hi + wk * hi
            for r, off in ((r_lo, c * LANES), (r_hi, half + c * LANES)):
                cols = slice(off, off + LANES)
                o_ref[:, cols] = x_ref[:, cols] + g_ref[:, cols] * (r + sh_ref[:, cols])
        return run

    pieces = [p_routed(c) for c in range(ROW_SUB)]
    j = 0
    for idx, piece in enumerate(pieces):
        for _ in range(n // len(pieces) + (1 if idx < n % len(pieces) else 0)):
            row_copy(pl.multiple_of(dest_s[base_next + j], ROW_SUB), j * ROW_SUB, nxt).start(priority=j % 2)
            j += 1
        piece()

    @pl.when(i == last)
    def _():
        wait_all(nxt)


def _combine(dest_flat, ys, shared, wt, x, gate):
    t, d = x.shape
    tm = TOKEN_TILE
    n = TOP_K * tm
    row = lambda i, dest: (i, 0)
    vec = lambda i, dest: (0, 0)
    grid_spec = pltpu.PrefetchScalarGridSpec(
        num_scalar_prefetch=1,
        grid=(t // tm,),
        in_specs=[pl.BlockSpec(memory_space=pl.ANY),
                  pl.BlockSpec((tm, d), row), pl.BlockSpec((tm, TOP_K), row),
                  pl.BlockSpec((tm, d), row), pl.BlockSpec((1, d), vec)],
        out_specs=pl.BlockSpec((tm, d), row),
        scratch_shapes=[pltpu.VMEM((2, n * ROW_SUB, LANES), U32), pltpu.SemaphoreType.DMA((2,))])
    return pl.pallas_call(
        _combine_kernel,
        grid_spec=grid_spec,
        out_shape=jax.ShapeDtypeStruct((t, d), F32),
        compiler_params=_cparams(("arbitrary",)),
        name="combine",
    )(dest_flat, ys, shared, wt, x, gate)


def _rope_tables(s):
    half = ROPE_DIM // 2
    inv = ROPE_THETA ** (-jnp.arange(half, dtype=F32) * 2.0 / ROPE_DIM)
    ang = jnp.arange(s, dtype=I32).astype(F32)[:, None] * inv[None, :]
    cos, sin = jnp.cos(ang), jnp.sin(ang)
    pad = HEAD_DIM - ROPE_DIM
    rc = jnp.concatenate([cos, cos, jnp.ones((s, pad), F32)], axis=1)
    ra = jnp.concatenate([jnp.zeros((s, half), F32), sin, jnp.zeros((s, pad), F32)], axis=1)
    rb = jnp.concatenate([-sin, jnp.zeros((s, half + pad), F32)], axis=1)
    return rc, ra, rb


def _mixer(x, mod, rope, mix_norm, w_in, sgu_norm, sgu_w, sgu_b, q_norm, k_norm,
           out_norm_sgu, out_norm_att, w_out):
    d = x.shape[1]
    sh_m, sc_m, g_m = mod[:, 0:d], mod[:, d:2 * d], mod[:, 2 * d:3 * d]
    outs = _proj(x, mix_norm.reshape(1, d), sc_m, sh_m, w_in.astype(BF16),
                 q_norm.reshape(1, HEAD_DIM), k_norm.reshape(1, HEAD_DIM), rope)
    y_sgu = _sgu(outs[0], sgu_norm.reshape(1, -1), sgu_w, sgu_b.T, out_norm_sgu.reshape(1, -1))
    y_att = _attn(outs[1:], out_norm_att.reshape(1, -1))
    return _wout(y_sgu, y_att, w_out.astype(BF16), x, g_m)


def _ffn(x, mod, layer, ffn_norm, router_w, router_bias, exp_gate, exp_up, exp_down,
         shared_gate, shared_up, shared_down):
    t, d = x.shape
    sh_f, sc_f, g_f = mod[:, 3 * d:4 * d], mod[:, 4 * d:5 * d], mod[:, 5 * d:6 * d]
    h_bf, idx_t, wt_t, rank_t, cnt = _router(x, ffn_norm.reshape(1, d), sc_f, sh_f, router_w.T,
                                             router_bias.reshape(N_EXPERTS, 1))
    counts = cnt[:, 0].astype(I32)
    items, seg_start = _work_items(counts, t * TOP_K)
    dest_t = _dest(idx_t, rank_t, seg_start)
    nt = t // TOKEN_TILE
    dest_flat = dest_t.reshape(TOP_K, nt, TOKEN_TILE).transpose(1, 0, 2).reshape(-1)
    xs, shared = _dispatch(dest_flat, h_bf, shared_gate.astype(BF16), shared_up.astype(BF16),
                           shared_down.astype(BF16))
    ys = _moe(items, xs, layer, exp_gate, exp_up, exp_down)
    return _combine(dest_flat, ys, shared, wt_t.T, x, g_f)


def kernel(x, c, ada_w, ada_b, mix_norm, w_in, sgu_norm, sgu_w, sgu_b, q_norm, k_norm, out_norm_sgu,
           out_norm_att, w_out, ffn_norm, router_w, router_bias, exp_gate, exp_up, exp_down,
           shared_gate, shared_up, shared_down):
    b, s, d = x.shape
    assert b == 1 and s % ATT_TILE == 0 and d == 2 * ROW_SUB * LANES
    mods = _ada(c, ada_w, ada_b)
    rope = _rope_tables(s)
    xf = x.reshape(s, d)
    for l in range(ada_w.shape[0]):
        xf = _mixer(xf, mods[l], rope, mix_norm[l], w_in[l], sgu_norm[l], sgu_w[l], sgu_b[l], q_norm[l],
                    k_norm[l], out_norm_sgu[l], out_norm_att[l], w_out[l])
        xf = _ffn(xf, mods[l], l, ffn_norm[l], router_w[l], router_bias[l], exp_gate, exp_up,
                  exp_down, shared_gate[l], shared_up[l], shared_down[l])
    return xf.reshape(b, s, d)
```

```python
import functools

import jax
import jax.numpy as jnp
from jax import lax
from jax.experimental import pallas as pl
from jax.experimental.pallas import tpu as pltpu

F32 = jnp.float32
BF16 = jnp.bfloat16
U32 = jnp.uint32
I32 = jnp.int32

HEAD_DIM = 128
N_HEADS = 8
SGU_CHUNK = 128
DILATIONS = (1, 4, 16)
ATT_SPAN = 128
ATT_TILE = 2048
ROPE_THETA = 500000.0
ROPE_DIM = HEAD_DIM // 4
N_EXPERTS = 64
TOP_K = 8
N_GROUPS = 8
TOPK_GROUPS = 4
ROUTED_SCALE = 2.5
NORM_EPS = 1e-6
N_MOD = 6

MOE_BLOCK = 512
TOKEN_TILE = 256
HI_MASK = 0xFFFF0000
LANES = 128
ROW_SUB = 8

VMEM_LIMIT = 56 * 1024 * 1024

NT_DIMS = (((1,), (1,)), ((), ()))


def _cparams(sem):
    return pltpu.CompilerParams(dimension_semantics=sem, vmem_limit_bytes=VMEM_LIMIT)


def _rms(x):
    return x * lax.rsqrt(jnp.mean(x * x, axis=-1, keepdims=True) + NORM_EPS)


def _gelu(x):
    return 0.5 * x * (1.0 + lax.erf(x * 0.7071067811865476))


def _pack_pair(lo_f32, hi_f32):
    lo = pltpu.bitcast(lo_f32.astype(BF16).astype(F32), U32)
    hi = pltpu.bitcast(hi_f32.astype(BF16).astype(F32), U32)
    return (lo >> 16) | (hi & jnp.uint32(HI_MASK))


def _unpack_pair(u):
    lo = pltpu.bitcast(u << 16, F32)
    hi = pltpu.bitcast(u & jnp.uint32(HI_MASK), F32)
    return lo, hi


def _store_rows(ref, base, n, lo_f32, hi_f32):
    for c in range(ROW_SUB):
        cols = slice(c * LANES, (c + 1) * LANES)
        ref[pl.ds(base + c, n, stride=ROW_SUB), :] = _pack_pair(lo_f32[:, cols], hi_f32[:, cols])


def _load_rows(ref, base, n):
    los, his = [], []
    for c in range(ROW_SUB):
        lo, hi = _unpack_pair(ref[pl.ds(base + c, n, stride=ROW_SUB), :])
        los.append(lo)
        his.append(hi)
    return jnp.concatenate(los, axis=1), jnp.concatenate(his, axis=1)


def _ada_kernel(c_ref, w_ref, b_ref, o_ref):
    d, tn = w_ref.shape

    def body(i, acc):
        r = pl.multiple_of(i * 8, 8)
        cc = c_ref[pl.ds(r, 8), :]
        return acc + (cc * jax.nn.sigmoid(cc)) * w_ref[pl.ds(r, 8), :]

    acc = lax.fori_loop(0, d // 8, body, jnp.zeros((8, tn), F32), unroll=8)
    o_ref[...] = jnp.sum(acc, axis=0, keepdims=True) + b_ref[...]


def _ada(c, ada_w, ada_b):
    nl, d, n = ada_w.shape
    tn = 1536
    return pl.pallas_call(
        _ada_kernel,
        grid=(nl, n // tn),
        in_specs=[pl.BlockSpec((d, 1), lambda l, j: (0, 0)),
                  pl.BlockSpec((None, d, tn), lambda l, j: (l, 0, j)),
                  pl.BlockSpec((None, 1, tn), lambda l, j: (l, 0, j))],
        out_specs=pl.BlockSpec((None, 1, tn), lambda l, j: (l, 0, j)),
        out_shape=jax.ShapeDtypeStruct((nl, 1, n), F32),
        compiler_params=_cparams(("arbitrary", "arbitrary")),
        name="ada",
    )(c.reshape(d, 1), ada_w, ada_b.reshape(nl, 1, n))


def _proj_kernel(x_ref, g_ref, sc_ref, sh_ref, w_ref, qg_ref, kg_ref, rc_ref, ra_ref, rb_ref,
                 uv_ref, q1_ref, q4_ref, q16_ref, k1_ref, k4_ref, k16_ref, v1_ref, v4_ref, v16_ref,
                 h_s, st_s):
    tm = x_ref.shape[0]
    pair = 2 * HEAD_DIM
    aw = N_HEADS * HEAD_DIM
    y = _rms(x_ref[...]) * g_ref[...]
    h_s[...] = (y * (1.0 + sc_ref[...]) + sh_ref[...]).astype(BF16)

    for p in range(uv_ref.shape[1] // pair):
        cols = slice(p * pair, (p + 1) * pair)
        uv_ref[:, cols] = jnp.dot(h_s[...], w_ref[:, cols], preferred_element_type=F32).astype(BF16)

    groups = ((q1_ref, q4_ref, q16_ref, qg_ref, HEAD_DIM ** -0.5),
              (k1_ref, k4_ref, k16_ref, kg_ref, 1.0),
              (v1_ref, v4_ref, v16_ref, None, 1.0))
    for gi, (o1, o4, o16, gain_ref, scale) in enumerate(groups):
        for p in range(N_HEADS // 2):
            c0 = uv_ref.shape[1] + gi * aw + p * pair
            acc = jnp.dot(h_s[...], w_ref[:, c0:c0 + pair], preferred_element_type=F32)
            for hh in range(2):
                h = 2 * p + hh
                a = acc[:, hh * HEAD_DIM:(hh + 1) * HEAD_DIM]
                if gain_ref is not None:
                    a = _rms(a) * gain_ref[...]
                    a = (a * rc_ref[...] + pltpu.roll(a, ROPE_DIM // 2, 1) * ra_ref[...]
                         + pltpu.roll(a, HEAD_DIM - ROPE_DIM // 2, 1) * rb_ref[...]) * scale
                o1[:, h * HEAD_DIM:(h + 1) * HEAD_DIM] = a.astype(BF16)
                stage = st_s.at[gi * N_HEADS + h]
                stage[...] = a
                for d, o in ((4, o4), (16, o16)):
                    for r in range(d):
                        c = (h * d + r) * HEAD_DIM
                        o[:, c:c + HEAD_DIM] = stage[pl.ds(r, tm // d, stride=d), :].astype(BF16)


def _proj(x, gain, sc, sh, w_bf, q_gain, k_gain, rope):
    s, d = x.shape
    n = w_bf.shape[1]
    tm = 256
    aw = N_HEADS * HEAD_DIM
    row = lambda i: (i, 0)
    vec = lambda i: (0, 0)
    out_shape = [jax.ShapeDtypeStruct((s, 2 * aw), BF16)]
    out_specs = [pl.BlockSpec((tm, 2 * aw), row)]
    for _ in range(3):
        for dil in DILATIONS:
            out_shape.append(jax.ShapeDtypeStruct((s // dil, dil * aw), BF16))
            out_specs.append(pl.BlockSpec((tm // dil, dil * aw), row))
    return pl.pallas_call(
        _proj_kernel,
        grid=(s // tm,),
        in_specs=[pl.BlockSpec((tm, d), row), pl.BlockSpec((1, d), vec), pl.BlockSpec((1, d), vec),
                  pl.BlockSpec((1, d), vec),
                  pl.BlockSpec((d, n), vec, pipeline_mode=pl.Buffered(1)),
                  pl.BlockSpec((1, HEAD_DIM), vec), pl.BlockSpec((1, HEAD_DIM), vec),
                  pl.BlockSpec((tm, HEAD_DIM), row), pl.BlockSpec((tm, HEAD_DIM), row),
                  pl.BlockSpec((tm, HEAD_DIM), row)],
        out_specs=out_specs,
        out_shape=out_shape,
        scratch_shapes=[pltpu.VMEM((tm, d), BF16), pltpu.VMEM((3 * N_HEADS, tm, HEAD_DIM), F32)],
        compiler_params=_cparams(("arbitrary",)),
        name="proj",
    )(x, gain, sc, sh, w_bf, q_gain, k_gain, *rope)


def _sgu_kernel(uv_ref, gn_ref, w_ref, b_ref, go_ref, o_ref):
    tm = uv_ref.shape[0]
    width = N_HEADS * HEAD_DIM
    ii = lax.broadcasted_iota(I32, (SGU_CHUNK, SGU_CHUNK), 0)
    jj = lax.broadcasted_iota(I32, (SGU_CHUNK, SGU_CHUNK), 1)
    causal = jj <= ii
    for g in range(N_HEADS):
        cols = slice(g * HEAD_DIM, (g + 1) * HEAD_DIM)
        wg = jnp.where(causal, w_ref[g], 0.0).astype(BF16)
        u = _gelu(uv_ref[:, cols].astype(F32))
        v = _gelu(uv_ref[:, width + g * HEAD_DIM:width + (g + 1) * HEAD_DIM].astype(F32))
        vn = (_rms(v) * gn_ref[:, cols]).astype(BF16)
        bias = b_ref[:, g:g + 1]
        for n in range(tm // SGU_CHUNK):
            rows = slice(n * SGU_CHUNK, (n + 1) * SGU_CHUNK)
            mixed = jnp.dot(wg, vn[rows], preferred_element_type=F32) + bias
            y = u[rows] * mixed
            o_ref[rows, cols] = (_rms(y) * go_ref[:, cols]).astype(BF16)


def _sgu(uv, sgu_norm, sgu_w, sgu_b_t, out_norm):
    s = uv.shape[0]
    width = N_HEADS * HEAD_DIM
    tm = 512
    vec = lambda i: (0, 0)
    return pl.pallas_call(
        _sgu_kernel,
        grid=(s // tm,),
        in_specs=[pl.BlockSpec((tm, 2 * width), lambda i: (i, 0)),
                  pl.BlockSpec((1, width), vec),
                  pl.BlockSpec((N_HEADS, SGU_CHUNK, SGU_CHUNK), lambda i: (0, 0, 0)),
                  pl.BlockSpec((SGU_CHUNK, N_HEADS), vec),
                  pl.BlockSpec((1, width), vec)],
        out_specs=pl.BlockSpec((tm, width), lambda i: (i, 0)),
        out_shape=jax.ShapeDtypeStruct((s, width), BF16),
        compiler_params=_cparams(("arbitrary",)),
        name="sgu",
    )(uv, sgu_norm, sgu_w, sgu_b_t, out_norm)


def _attn_kernel(q1, q4, q16, k1, k4, k16, k1p, k4p, k16p, v1, v4, v16, v1p, v4p, v16p, go_ref,
                 o_ref, kb1, kb4, kb16, vb1, vb4, vb16, os1, os4, os16, ls1, ls4, ls16):
    first = pl.program_id(0) == 0
    blk = ATT_SPAN
    for buf, prev, cur in ((kb1, k1p, k1), (kb4, k4p, k4), (kb16, k16p, k16),
                           (vb1, v1p, v1), (vb4, v4p, v4), (vb16, v16p, v16)):
        buf[0:blk, :] = prev[...]
        buf[blk:, :] = cur[...]

    qi = lax.broadcasted_iota(I32, (blk, 2 * blk), 0)
    kj = lax.broadcasted_iota(I32, (blk, 2 * blk), 1)
    dist = qi + blk - kj
    band = (dist >= 0) & (dist <= ATT_SPAN)
    neg = jnp.where(band, 0.0, -jnp.inf)
    neg0 = jnp.where(first, jnp.where(band & (kj >= blk), 0.0, -jnp.inf), neg)

    for d, q_ref, kb, vb, o_s, l_s in ((1, q1, kb1, vb1, os1, ls1), (4, q4, kb4, vb4, os4, ls4),
                                       (16, q16, kb16, vb16, os16, ls16)):
        nb = ATT_TILE // (blk * d)
        for r in range(d):
            cols = slice(r * HEAD_DIM, (r + 1) * HEAD_DIM)
            for b in range(nb):
                q = q_ref[b * blk:(b + 1) * blk, cols]
                kk = kb[b * blk:(b + 2) * blk, cols]
                vv = vb[b * blk:(b + 2) * blk, cols]
                s = lax.dot_general(q, kk, NT_DIMS, preferred_element_type=F32)
                s = s + (neg0 if b == 0 else neg)
                m = jnp.max(s, axis=-1, keepdims=True)
                e = jnp.exp(s - m)
                l = jnp.sum(e, axis=-1, keepdims=True)
                o = jnp.dot(e.astype(BF16), vv, preferred_element_type=F32) / l
                lse = jnp.broadcast_to(m + jnp.log(l), (blk, HEAD_DIM))
                if d == 1:
                    o_s[b * blk:(b + 1) * blk, :] = o
                    l_s[b * blk:(b + 1) * blk, :] = lse
                else:
                    o_s[pl.ds(d * b * blk + r, blk, stride=d), :] = o
                    l_s[pl.ds(d * b * blk + r, blk, stride=d), :] = lse

    step = 256
    for c in range(ATT_TILE // step):
        rows = slice(c * step, (c + 1) * step)
        l1, l4, l16 = ls1[rows, :], ls4[rows, :], ls16[rows, :]
        mx = jnp.maximum(l1, jnp.maximum(l4, l16))
        w1, w4, w16 = jnp.exp(l1 - mx), jnp.exp(l4 - mx), jnp.exp(l16 - mx)
        o = (w1 * os1[rows, :] + w4 * os4[rows, :] + w16 * os16[rows, :]) / (w1 + w4 + w16)
        o_ref[rows, :] = (_rms(o) * go_ref[...]).astype(BF16)


def _attn(qkv, out_norm):
    q1, q4, q16, k1, k4, k16, v1, v4, v16 = qkv
    s = q1.shape[0]
    nt = s // ATT_TILE
    blk = ATT_SPAN

    def cur(d):
        return pl.BlockSpec((ATT_TILE // d, d * HEAD_DIM), lambda i, h: (i, h))

    def prev(d):
        per = ATT_TILE // (d * blk)
        return pl.BlockSpec((blk, d * HEAD_DIM), lambda i, h: (jnp.maximum(i * per - 1, 0), h))

    in_specs = ([cur(d) for d in DILATIONS] + [cur(d) for d in DILATIONS] + [prev(d) for d in DILATIONS]
                + [cur(d) for d in DILATIONS] + [prev(d) for d in DILATIONS]
                + [pl.BlockSpec((1, HEAD_DIM), lambda i, h: (0, h))])
    kv_bufs = [pltpu.VMEM((blk + ATT_TILE // d, d * HEAD_DIM), BF16) for d in DILATIONS]
    acc_bufs = [pltpu.VMEM((ATT_TILE, HEAD_DIM), F32) for _ in DILATIONS]
    return pl.pallas_call(
        _attn_kernel,
        grid=(nt, N_HEADS),
        in_specs=in_specs,
        out_specs=pl.BlockSpec((ATT_TILE, HEAD_DIM), lambda i, h: (i, h)),
        out_shape=jax.ShapeDtypeStruct((s, N_HEADS * HEAD_DIM), BF16),
        scratch_shapes=kv_bufs + kv_bufs + acc_bufs + acc_bufs,
        compiler_params=_cparams(("arbitrary", "arbitrary")),
        name="attn",
    )(q1, q4, q16, k1, k4, k16, k1, k4, k16, v1, v4, v16, v1, v4, v16, out_norm)


def _wout_kernel(a1_ref, a2_ref, w_ref, x_ref, g_ref, o_ref):
    half = a1_ref.shape[1]
    acc = jnp.dot(a1_ref[...], w_ref[0:half, :], preferred_element_type=F32)
    acc = acc + jnp.dot(a2_ref[...], w_ref[half:2 * half, :], preferred_element_type=F32)
    o_ref[...] = x_ref[...] + g_ref[...] * acc


def _wout(y_sgu, y_att, w_bf, x, gate):
    s, d = x.shape
    half = y_sgu.shape[1]
    tm, tn = 1024, 1024
    return pl.pallas_call(
        _wout_kernel,
        grid=(s // tm, d // tn),
        in_specs=[pl.BlockSpec((tm, half), lambda i, j: (i, 0)),
                  pl.BlockSpec((tm, half), lambda i, j: (i, 0)),
                  pl.BlockSpec((2 * half, tn), lambda i, j: (0, j)),
                  pl.BlockSpec((tm, tn), lambda i, j: (i, j)),
                  pl.BlockSpec((1, tn), lambda i, j: (0, j))],
        out_specs=pl.BlockSpec((tm, tn), lambda i, j: (i, j)),
        out_shape=jax.ShapeDtypeStruct((s, d), F32),
        compiler_params=_cparams(("arbitrary", "arbitrary")),
        name="wout",
    )(y_sgu, y_att, w_bf, x, gate)


def _router_kernel(x_ref, g_ref, sc_ref, sh_ref, rwt_ref, rb_ref,
                   h_ref, hp_ref, idx_ref, wt_ref, rank_ref, cnt_ref, carry_s):
    tm, d = x_ref.shape
    gsz = N_EXPERTS // N_GROUPS

    @pl.when(pl.program_id(0) == 0)
    def _():
        carry_s[...] = jnp.zeros_like(carry_s)

    h = (_rms(x_ref[...]) * g_ref[...]) * (1.0 + sc_ref[...]) + sh_ref[...]
    h_ref[...] = h.astype(BF16)
    _store_rows(hp_ref, 0, tm, h[:, 0:d // 2], h[:, d // 2:d])

    hh = h.astype(BF16)
    hl = (h - hh.astype(F32)).astype(BF16)
    rw = rwt_ref[...]
    rh = rw.astype(BF16)
    rl = (rw - rh.astype(F32)).astype(BF16)
    dg = functools.partial(lax.dot_general, dimension_numbers=NT_DIMS, preferred_element_type=F32)
    logits = dg(rh, hh) + dg(rl, hh) + dg(rh, hl)
    scores = jax.nn.sigmoid(logits)
    sel = scores + rb_ref[...]

    io8 = lax.broadcasted_iota(I32, (gsz, tm), 0)
    grp, gscore = [], []
    for g in range(N_GROUPS):
        sg = sel[g * gsz:(g + 1) * gsz, :]
        m1 = jnp.max(sg, axis=0, keepdims=True)
        i1 = jnp.min(jnp.where(sg == m1, io8, gsz), axis=0, keepdims=True)
        m2 = jnp.max(jnp.where(io8 == i1, -jnp.inf, sg), axis=0, keepdims=True)
        grp.append(sg)
        gscore.append(m1 + m2)

    parts = []
    for g in range(N_GROUPS):
        beaten = jnp.zeros((1, tm), I32)
        for g2 in range(N_GROUPS):
            if g2 == g:
                continue
            b = (gscore[g2] >= gscore[g]) if g2 < g else (gscore[g2] > gscore[g])
            beaten = beaten + b.astype(I32)
        parts.append(jnp.where(beaten < TOPK_GROUPS, grp[g], -jnp.inf))
    masked = jnp.concatenate(parts, axis=0)

    io = lax.broadcasted_iota(I32, (N_EXPERTS, tm), 0)
    chosen = jnp.zeros((N_EXPERTS, tm), F32)
    idxs, wts = [], []
    for _ in range(TOP_K):
        m = jnp.max(masked, axis=0, keepdims=True)
        am = jnp.min(jnp.where(masked == m, io, N_EXPERTS), axis=0, keepdims=True)
        hit = io == am
        wts.append(jnp.sum(jnp.where(hit, scores, 0.0), axis=0, keepdims=True))
        idxs.append(am)
        chosen = chosen + hit.astype(F32)
        masked = jnp.where(hit, -jnp.inf, masked)
    wsum = wts[0]
    for k in range(1, TOP_K):
        wsum = wsum + wts[k]
    for k in range(TOP_K):
        idx_ref[k:k + 1, :] = idxs[k]
        wt_ref[k:k + 1, :] = wts[k] / wsum * ROUTED_SCALE

    t0 = lax.broadcasted_iota(I32, (tm, tm), 0)
    t1 = lax.broadcasted_iota(I32, (tm, tm), 1)
    before = (t0 < t1).astype(BF16)
    pre = jnp.dot(chosen.astype(BF16), before, preferred_element_type=F32) + carry_s[...]
    for k in range(TOP_K):
        rank_ref[k:k + 1, :] = jnp.sum(jnp.where(io == idxs[k], pre, 0.0), axis=0,
                                       keepdims=True).astype(I32)
    carry_s[...] = carry_s[...] + jnp.sum(chosen, axis=1, keepdims=True)
    cnt_ref[...] = jnp.broadcast_to(carry_s[...], cnt_ref.shape)


def _router(x, gain, sc, sh, rw_t, rbias):
    t, d = x.shape
    tm = 512
    vec = lambda i: (0, 0)
    tok = lambda i: (0, i)
    return pl.pallas_call(
        _router_kernel,
        grid=(t // tm,),
        in_specs=[pl.BlockSpec((tm, d), lambda i: (i, 0)), pl.BlockSpec((1, d), vec),
                  pl.BlockSpec((1, d), vec), pl.BlockSpec((1, d), vec),
                  pl.BlockSpec((N_EXPERTS, d), vec), pl.BlockSpec((N_EXPERTS, 1), vec)],
        out_specs=[pl.BlockSpec((tm, d), lambda i: (i, 0)),
                   pl.BlockSpec((tm * ROW_SUB, LANES), lambda i: (i, 0)), pl.BlockSpec((TOP_K, tm), tok),
                   pl.BlockSpec((TOP_K, tm), tok), pl.BlockSpec((TOP_K, tm), tok),
                   pl.BlockSpec((N_EXPERTS, HEAD_DIM), vec)],
        out_shape=[jax.ShapeDtypeStruct((t, d), BF16),
                   jax.ShapeDtypeStruct((t * ROW_SUB, LANES), U32), jax.ShapeDtypeStruct((TOP_K, t), I32),
                   jax.ShapeDtypeStruct((TOP_K, t), F32), jax.ShapeDtypeStruct((TOP_K, t), I32),
                   jax.ShapeDtypeStruct((N_EXPERTS, HEAD_DIM), F32)],
        scratch_shapes=[pltpu.VMEM((N_EXPERTS, 1), F32)],
        compiler_params=_cparams(("arbitrary",)),
        name="router",
    )(x, gain, sc, sh, rw_t, rbias)


def _dest_kernel(idx_ref, rank_ref, start_ref, o_ref):
    tm = idx_ref.shape[1]
    io = lax.broadcasted_iota(I32, (N_EXPERTS, tm), 0)
    for k in range(TOP_K):
        seg = jnp.sum(jnp.where(io == idx_ref[k:k + 1, :], start_ref[...], 0), axis=0, keepdims=True)
        o_ref[k:k + 1, :] = (seg + rank_ref[k:k + 1, :]) * ROW_SUB


def _dest(idx_t, rank_t, seg_start):
    t = idx_t.shape[1]
    tm = 2048
    tok = lambda i: (0, i)
    return pl.pallas_call(
        _dest_kernel,
        grid=(t // tm,),
        in_specs=[pl.BlockSpec((TOP_K, tm), tok), pl.BlockSpec((TOP_K, tm), tok),
                  pl.BlockSpec((N_EXPERTS, 1), lambda i: (0, 0))],
        out_specs=pl.BlockSpec((TOP_K, tm), tok),
        out_shape=jax.ShapeDtypeStruct((TOP_K, t), I32),
        compiler_params=_cparams(("arbitrary",)),
        name="dest",
    )(idx_t, rank_t, seg_start.reshape(N_EXPERTS, 1))


def _dispatch_kernel(dest_s, h_ref, sg_ref, su_ref, sd_ref, xs_hbm, sh_ref, pk_s, sem_row):
    tm, d = h_ref.shape
    n = TOP_K * tm
    i = pl.program_id(0)
    last = pl.num_programs(0) - 1
    cur = i % 2
    base = i * n

    def row_copy(src_row, dst_row, buf):
        return pltpu.make_async_copy(pk_s.at[buf, pl.ds(src_row, ROW_SUB)],
                                     xs_hbm.at[pl.ds(dst_row, ROW_SUB)], sem_row.at[buf])

    def wait_all(buf):
        def wait(j, c):
            row_copy(0, 0, buf).wait()
            return c
        lax.fori_loop(0, n, wait, 0, unroll=8)

    @pl.when(i >= 2)
    def _():
        wait_all(cur)

    _store_rows(pk_s.at[cur], 0, tm, h_ref[:, 0:d // 2].astype(F32), h_ref[:, d // 2:d].astype(F32))

    vals = {}

    def p_gate():
        vals["gate"] = jnp.dot(h_ref[...], sg_ref[...], preferred_element_type=F32)

    def p_up():
        up = jnp.dot(h_ref[...], su_ref[...], preferred_element_type=F32)
        g = vals["gate"]
        vals["hid"] = (g * jax.nn.sigmoid(g) * up).astype(BF16)

    def p_shared(m):
        def run():
            cols = slice(m * 2 * LANES, (m + 1) * 2 * LANES)
            sh_ref[:, cols] = jnp.dot(vals["hid"], sd_ref[:, cols], preferred_element_type=F32)
        return run

    pieces = [p_gate, p_up] + [p_shared(m) for m in range(d // (2 * LANES))]
    j = 0
    for idx, piece in enumerate(pieces):
        for _ in range(n // len(pieces) + (1 if idx < n % len(pieces) else 0)):
            row_copy((j % tm) * ROW_SUB, pl.multiple_of(dest_s[base + j], ROW_SUB), cur).start(priority=j % 2)
            j += 1
        piece()

    @pl.when(i == last)
    def _():
        wait_all(cur)

    @pl.when((i == last) & (i >= 1))
    def _():
        wait_all(1 - cur)


def _dispatch(dest_flat, h_bf, sg_bf, su_bf, sd_bf):
    t, d = h_bf.shape
    f = sg_bf.shape[1]
    tm = TOKEN_TILE
    row = lambda i, dest: (i, 0)
    vec = lambda i, dest: (0, 0)
    grid_spec = pltpu.PrefetchScalarGridSpec(
        num_scalar_prefetch=1,
        grid=(t // tm,),
        in_specs=[pl.BlockSpec((tm, d), row),
                  pl.BlockSpec((d, f), vec), pl.BlockSpec((d, f), vec), pl.BlockSpec((f, d), vec)],
        out_specs=[pl.BlockSpec(memory_space=pl.ANY), pl.BlockSpec((tm, d), row)],
        scratch_shapes=[pltpu.VMEM((2, tm * ROW_SUB, LANES), U32), pltpu.SemaphoreType.DMA((2,))])
    return pl.pallas_call(
        _dispatch_kernel,
        grid_spec=grid_spec,
        out_shape=[jax.ShapeDtypeStruct((t * TOP_K * ROW_SUB, LANES), U32),
                   jax.ShapeDtypeStruct((t, d), F32)],
        compiler_params=_cparams(("arbitrary",)),
        name="dispatch",
    )(dest_flat, h_bf, sg_bf, su_bf, sd_bf)


def _moe_kernel(blk_s, wsel_s, lo_s, hi_s, fresh_s, newe_s, x_ref, wg_ref, wu_ref, wd_ref, y_ref,
                acc_s, wg_s, wu_s, wd_s):
    w = pl.program_id(0)
    bm, d = acc_s.shape
    half = d // 2

    @pl.when(w == 0)
    def _():
        acc_s[...] = jnp.zeros_like(acc_s)

    @pl.when(newe_s[w] == 1)
    def _():
        wg_s[...] = wg_ref[...].astype(BF16)
        wu_s[...] = wu_ref[...].astype(BF16)
        wd_s[...] = wd_ref[...].astype(BF16)

    lo, hi = lo_s[w], hi_s[w]

    @pl.when(hi > lo)
    def _():
        xl, xh = _load_rows(x_ref, 0, bm)
        xl, xh = xl.astype(BF16), xh.astype(BF16)
        gate = (jnp.dot(xl, wg_s[0:half, :], preferred_element_type=F32)
                + jnp.dot(xh, wg_s[half:2 * half, :], preferred_element_type=F32))
        up = (jnp.dot(xl, wu_s[0:half, :], preferred_element_type=F32)
              + jnp.dot(xh, wu_s[half:2 * half, :], preferred_element_type=F32))
        hid = (gate * jax.nn.sigmoid(gate) * up).astype(BF16)
        y = jnp.dot(hid, wd_s[...], preferred_element_type=F32)
        rows = lax.broadcasted_iota(I32, (bm, 1), 0)
        keep = (rows >= lo) & (rows < hi)
        acc = jnp.where(fresh_s[w] == 1, 0.0, acc_s[...]) + jnp.where(keep, y, 0.0)
        acc_s[...] = acc
        _store_rows(y_ref, 0, bm, acc[:, 0:half], acc[:, half:2 * half])


def _moe(items, xs, layer, w_gate, w_up, w_down):
    _, _, d, f = w_gate.shape
    n_items = items[0].shape[0]
    bm = MOE_BLOCK
    wmap = lambda w, blk, ex, lo, hi, fi, ne: (layer, ex[w], 0, 0)
    rows = pl.BlockSpec((bm * ROW_SUB, LANES), lambda w, blk, ex, lo, hi, fi, ne: (blk[w], 0))
    grid_spec = pltpu.PrefetchScalarGridSpec(
        num_scalar_prefetch=6,
        grid=(n_items,),
        in_specs=[rows,
                  pl.BlockSpec((None, None, d, f), wmap),
                  pl.BlockSpec((None, None, d, f), wmap),
                  pl.BlockSpec((None, None, f, d), wmap)],
        out_specs=rows,
        scratch_shapes=[pltpu.VMEM((bm, d), F32), pltpu.VMEM((d, f), BF16),
                        pltpu.VMEM((d, f), BF16), pltpu.VMEM((f, d), BF16)])
    return pl.pallas_call(
        _moe_kernel,
        grid_spec=grid_spec,
        out_shape=jax.ShapeDtypeStruct(xs.shape, U32),
        compiler_params=_cparams(("arbitrary",)),
        name="moe",
    )(*items, xs, w_gate, w_up, w_down)


def _work_items(counts, m):
    bm = MOE_BLOCK
    nb = m // bm
    seg_end = jnp.cumsum(counts)
    seg_start = seg_end - counts
    starts = jnp.sort(jnp.concatenate([jnp.arange(nb, dtype=I32) * bm, seg_start]))
    ends = jnp.concatenate([starts[1:], jnp.full((1,), m, I32)])
    length = ends - starts
    blk = jnp.minimum(starts // bm, nb - 1)
    ex = jnp.sum((seg_end[None, :] <= starts[:, None]).astype(I32), axis=1)
    ex = jnp.minimum(ex, N_EXPERTS - 1)
    ex = lax.cummax(jnp.where(length > 0, ex, 0))
    lo = starts - blk * bm
    hi = lo + length
    one = jnp.ones((1,), I32)
    newe = jnp.concatenate([one, (ex[1:] != ex[:-1]).astype(I32)])
    last_blk = lax.cummax(jnp.where(length > 0, blk, -1))
    prev_blk = jnp.concatenate([jnp.full((1,), -1, I32), last_blk[:-1]])
    fresh = (length > 0) & (blk != prev_blk)
    kind = jnp.where(length == 0, KIND_EMPTY,
                     jnp.where(fresh, jnp.where(blk == 0, KIND_FIRST, KIND_FRESH), KIND_PLAIN)).astype(I32)
    cand = jnp.where(newe == 1, ex, N_EXPERTS)
    nxt = jnp.concatenate([lax.cummin(cand[::-1])[::-1][1:], jnp.full((1,), N_EXPERTS, I32)])
    wsel = jnp.where(newe == 1, ex, jnp.where(nxt < N_EXPERTS, nxt, ex))
    return (blk, wsel, lo, hi, kind, newe), seg_start


def _combine_kernel(dest_s, ys_hbm, sh_ref, wt_ref, x_ref, g_ref, o_ref, y_s, sem_row):
    tm, d = x_ref.shape
    half = d // 2
    n = TOP_K * tm
    i = pl.program_id(0)
    last = pl.num_programs(0) - 1
    cur = i % 2
    nxt = 1 - cur
    base_next = jnp.minimum(i + 1, last) * n

    def row_copy(src_row, dst_row, buf):
        return pltpu.make_async_copy(ys_hbm.at[pl.ds(src_row, ROW_SUB)],
                                     y_s.at[buf, pl.ds(dst_row, ROW_SUB)], sem_row.at[buf])

    def wait_all(buf):
        def wait(j, c):
            row_copy(0, 0, buf).wait()
            return c
        lax.fori_loop(0, n, wait, 0, unroll=8)

    @pl.when(i == 0)
    def _():
        def start(j, c):
            row_copy(pl.multiple_of(dest_s[j], ROW_SUB), pl.multiple_of(j * ROW_SUB, ROW_SUB), 0).start()
            return c
        lax.fori_loop(0, n, start, 0, unroll=8)

    wait_all(cur)

    def p_routed(c):
        def run():
            r_lo = jnp.zeros((tm, LANES), F32)
            r_hi = jnp.zeros((tm, LANES), F32)
            for k in range(TOP_K):
                lo, hi = _unpack_pair(y_s[cur, pl.ds(k * tm * ROW_SUB + c, tm, stride=ROW_SUB), :])
                wk = wt_ref[:, k:k + 1]
                r_lo = r_lo + wk * lo
                r_hi = r_hi + wk * hi
            for r, off in ((r_lo, c * LANES), (r_hi, half + c * LANES)):
                cols = slice(off, off + LANES)
                o_ref[:, cols] = x_ref[:, cols] + g_ref[:, cols] * (r + sh_ref[:, cols])
        return run

    pieces = [p_routed(c) for c in range(ROW_SUB)]
    j = 0
    for idx, piece in enumerate(pieces):
        for _ in range(n // len(pieces) + (1 if idx < n % len(pieces) else 0)):
            row_copy(pl.multiple_of(dest_s[base_next + j], ROW_SUB), j * ROW_SUB, nxt).start(priority=j % 2)
            j += 1
        piece()

    @pl.when(i == last)
    def _():
        wait_all(nxt)


def _combine(dest_flat, ys, shared, wt, x, gate):
    t, d = x.shape
    tm = TOKEN_TILE
    n = TOP_K * tm
    row = lambda i, dest: (i, 0)
    vec = lambda i, dest: (0, 0)
    grid_spec = pltpu.PrefetchScalarGridSpec(
        num_scalar_prefetch=1,
        grid=(t // tm,),
        in_specs=[pl.BlockSpec(memory_space=pl.ANY),
                  pl.BlockSpec((tm, d), row), pl.BlockSpec((tm, TOP_K), row),
                  pl.BlockSpec((tm, d), row), pl.BlockSpec((1, d), vec)],
        out_specs=pl.BlockSpec((tm, d), row),
        scratch_shapes=[pltpu.VMEM((2, n * ROW_SUB, LANES), U32), pltpu.SemaphoreType.DMA((2,))])
    return pl.pallas_call(
        _combine_kernel,
        grid_spec=grid_spec,
        out_shape=jax.ShapeDtypeStruct((t, d), F32),
        compiler_params=_cparams(("arbitrary",)),
        name="combine",
    )(dest_flat, ys, shared, wt, x, gate)


KIND_EMPTY, KIND_PLAIN, KIND_FRESH, KIND_FIRST = 0, 1, 2, 3


def _fused_moe_kernel(blk_s, wsel_s, lo_s, hi_s, kind_s, newe_s, inv_s,
                      hp_hbm, wg_ref, wu_ref, wd_ref, yt_hbm,
                      x_s, y_s, acc_s, wg_s, wu_s, wd_s, sem_g, sem_s):
    w = pl.program_id(0)
    bm, d = acc_s.shape
    half = d // 2
    nb = inv_s.shape[0] // bm
    n_tokens = inv_s.shape[0] // TOP_K
    assert n_tokens & (n_tokens - 1) == 0
    last_buf = (nb - 1) % 2
    b = blk_s[w]
    cur = b % 2
    kind = kind_s[w]
    lo, hi = lo_s[w], hi_s[w]

    def gather_copy(src_row, dst_row, buf):
        return pltpu.make_async_copy(hp_hbm.at[pl.ds(src_row, ROW_SUB)],
                                     x_s.at[buf, pl.ds(dst_row, ROW_SUB)], sem_g.at[buf])

    def scatter_copy(src_row, dst_row, buf):
        return pltpu.make_async_copy(y_s.at[buf, pl.ds(src_row, ROW_SUB)],
                                     yt_hbm.at[pl.ds(dst_row, ROW_SUB)], sem_s.at[buf])

    def start_gather(block, r, buf):
        j = inv_s[block * bm + r]
        token = j & (n_tokens - 1)
        gather_copy(pl.multiple_of(token * ROW_SUB, ROW_SUB), pl.multiple_of(r * ROW_SUB, ROW_SUB),
                    buf).start(priority=0)

    def start_scatter(block, r, buf):
        j = inv_s[block * bm + r]
        scatter_copy(pl.multiple_of(r * ROW_SUB, ROW_SUB), pl.multiple_of(j * ROW_SUB, ROW_SUB),
                     buf).start(priority=1)

    def wait_rows(copy, buf):
        def body(r, c):
            copy(0, 0, buf).wait()
            return c
        lax.fori_loop(0, bm, body, 0, unroll=8)

    def loop_rows(start, block, buf):
        def body(r, c):
            start(block, r, buf)
            return c
        lax.fori_loop(0, bm, body, 0, unroll=8)

    @pl.when(newe_s[w] == 1)
    def _():
        wg_s[...] = wg_ref[...].astype(BF16)
        wu_s[...] = wu_ref[...].astype(BF16)
        wd_s[...] = wd_ref[...].astype(BF16)

    def compute(fresh, dmas):
        chunk = 2 * LANES
        pieces = []
        vals = {}

        def p_load():
            xl, xh = _load_rows(x_s.at[cur], 0, bm)
            vals["xl"], vals["xh"] = xl.astype(BF16), xh.astype(BF16)

        def p_in(name, w_s, c):
            def run():
                cols = slice(c * chunk, (c + 1) * chunk)
                vals[name, c] = (jnp.dot(vals["xl"], w_s[0:half, cols], preferred_element_type=F32)
                                 + jnp.dot(vals["xh"], w_s[half:d, cols], preferred_element_type=F32))
            return run

        def p_hid():
            n_c = wg_s.shape[1] // chunk
            gate = jnp.concatenate([vals["g", c] for c in range(n_c)], axis=1)
            up = jnp.concatenate([vals["u", c] for c in range(n_c)], axis=1)
            vals["hid"] = (gate * jax.nn.sigmoid(gate) * up).astype(BF16)

        def p_down(m):
            def run():
                cols = slice(m * chunk, (m + 1) * chunk)
                y = jnp.dot(vals["hid"], wd_s[:, cols], preferred_element_type=F32)
                rows = lax.broadcasted_iota(I32, (bm, 1), 0)
                y = jnp.where((rows >= lo) & (rows < hi), y, 0.0)
                acc_s[:, cols] = y if fresh else acc_s[:, cols] + y
            return run

        def p_pack(c):
            def run():
                lo_cols = slice(c * LANES, (c + 1) * LANES)
                hi_cols = slice(half + c * LANES, half + (c + 1) * LANES)
                y_s[cur, pl.ds(c, bm, stride=ROW_SUB), :] = _pack_pair(acc_s[:, lo_cols], acc_s[:, hi_cols])
            return run

        pieces.append(p_load)
        for c in range(wg_s.shape[1] // chunk):
            pieces.append(p_in("g", wg_s, c))
            pieces.append(p_in("u", wu_s, c))
        pieces.append(p_hid)
        pieces += [p_down(m) for m in range(d // chunk)]
        pieces += [p_pack(c) for c in range(ROW_SUB)]
        n_p = len(pieces)
        k = 0
        for idx, piece in enumerate(pieces):
            for _ in range(len(dmas) // n_p + (1 if idx < len(dmas) % n_p else 0)):
                dmas[k]()
                k += 1
            piece()

    @pl.when(kind == KIND_FIRST)
    def _():
        loop_rows(start_gather, 0, 0)
        wait_rows(gather_copy, 0)
        loop_rows(start_gather, 1, 1)
        compute(True, [])

    @pl.when(kind == KIND_FRESH)
    def _():
        wait_rows(gather_copy, cur)

        @pl.when(b >= 2)
        def _():
            wait_rows(scatter_copy, cur)

        nxt_block = jnp.minimum(b + 1, nb - 1)
        dmas = []
        for r in range(bm):
            dmas.append(functools.partial(start_gather, nxt_block, r, 1 - cur))
            dmas.append(functools.partial(start_scatter, b - 1, r, 1 - cur))
        compute(True, dmas)

    @pl.when(kind == KIND_PLAIN)
    def _():
        compute(False, [])

    @pl.when(w == pl.num_programs(0) - 1)
    def _():
        wait_rows(gather_copy, 1 - last_buf)
        wait_rows(scatter_copy, 1 - last_buf)
        loop_rows(start_scatter, nb - 1, last_buf)
        wait_rows(scatter_copy, last_buf)


def _fused_moe(items, inv, hp, layer, w_gate, w_up, w_down):
    _, _, d, f = w_gate.shape
    m = inv.shape[0]
    bm = MOE_BLOCK
    assert m // bm >= 2
    wmap = lambda w, blk, ws, lo, hi, kd, ne, iv: (layer, ws[w], 0, 0)
    grid_spec = pltpu.PrefetchScalarGridSpec(
        num_scalar_prefetch=7,
        grid=(items[0].shape[0],),
        in_specs=[pl.BlockSpec(memory_space=pl.ANY),
                  pl.BlockSpec((None, None, d, f), wmap),
                  pl.BlockSpec((None, None, d, f), wmap),
                  pl.BlockSpec((None, None, f, d), wmap)],
        out_specs=pl.BlockSpec(memory_space=pl.ANY),
        scratch_shapes=[pltpu.VMEM((2, bm * ROW_SUB, LANES), U32), pltpu.VMEM((2, bm * ROW_SUB, LANES), U32),
                        pltpu.VMEM((bm, d), F32), pltpu.VMEM((d, f), BF16), pltpu.VMEM((d, f), BF16),
                        pltpu.VMEM((f, d), BF16), pltpu.SemaphoreType.DMA((2,)),
                        pltpu.SemaphoreType.DMA((2,))])
    return pl.pallas_call(
        _fused_moe_kernel,
        grid_spec=grid_spec,
        out_shape=jax.ShapeDtypeStruct((m * ROW_SUB, LANES), U32),
        compiler_params=_cparams(("arbitrary",)),
        name="moe",
    )(*items, inv, hp, w_gate, w_up, w_down)


def _finish_kernel(*refs):
    yt_refs = refs[:TOP_K]
    h_ref, wt_ref, x_ref, g_ref, sg_ref, su_ref, sd_ref, o_ref = refs[TOP_K:]
    tm, d = x_ref.shape
    half = d // 2
    gate = jnp.dot(h_ref[...], sg_ref[...], preferred_element_type=F32)
    up = jnp.dot(h_ref[...], su_ref[...], preferred_element_type=F32)
    hid = (gate * jax.nn.sigmoid(gate) * up).astype(BF16)
    shared = jnp.dot(hid, sd_ref[...], preferred_element_type=F32)
    for c in range(ROW_SUB):
        r_lo = jnp.zeros((tm, LANES), F32)
        r_hi = jnp.zeros((tm, LANES), F32)
        for k in range(TOP_K):
            lo, hi = _unpack_pair(yt_refs[k][pl.ds(c, tm, stride=ROW_SUB), :])
            wk = wt_ref[:, k:k + 1]
            r_lo = r_lo + wk * lo
            r_hi = r_hi + wk * hi
        for r, off in ((r_lo, c * LANES), (r_hi, half + c * LANES)):
            cols = slice(off, off + LANES)
            o_ref[:, cols] = x_ref[:, cols] + g_ref[:, cols] * (r + shared[:, cols])


def _finish(yt, h_bf, wt, x, gate, sg_bf, su_bf, sd_bf):
    t, d = x.shape
    f = sg_bf.shape[1]
    tm = TOKEN_TILE
    nt = t // tm
    row = lambda i: (i, 0)
    vec = lambda i: (0, 0)
    yt_specs = [pl.BlockSpec((tm * ROW_SUB, LANES), functools.partial(lambda k, i: (k * nt + i, 0), k))
                for k in range(TOP_K)]
    return pl.pallas_call(
        _finish_kernel,
        grid=(nt,),
        in_specs=yt_specs + [
            pl.BlockSpec((tm, d), row), pl.BlockSpec((tm, TOP_K), row),
            pl.BlockSpec((tm, d), row), pl.BlockSpec((1, d), vec),
            pl.BlockSpec((d, f), vec), pl.BlockSpec((d, f), vec), pl.BlockSpec((f, d), vec)],
        out_specs=pl.BlockSpec((tm, d), row),
        out_shape=jax.ShapeDtypeStruct((t, d), F32),
        compiler_params=_cparams(("arbitrary",)),
        name="finish",
    )(*([yt] * TOP_K), h_bf, wt, x, gate, sg_bf, su_bf, sd_bf)


def _rope_tables(s):
    half = ROPE_DIM // 2
    inv = ROPE_THETA ** (-jnp.arange(half, dtype=F32) * 2.0 / ROPE_DIM)
    ang = jnp.arange(s, dtype=I32).astype(F32)[:, None] * inv[None, :]
    cos, sin = jnp.cos(ang), jnp.sin(ang)
    pad = HEAD_DIM - ROPE_DIM
    rc = jnp.concatenate([cos, cos, jnp.ones((s, pad), F32)], axis=1)
    ra = jnp.concatenate([jnp.zeros((s, half), F32), sin, jnp.zeros((s, pad), F32)], axis=1)
    rb = jnp.concatenate([-sin, jnp.zeros((s, half + pad), F32)], axis=1)
    return rc, ra, rb


def _mixer(x, mod, rope, mix_norm, w_in, sgu_norm, sgu_w, sgu_b, q_norm, k_norm,
           out_norm_sgu, out_norm_att, w_out):
    d = x.shape[1]
    sh_m, sc_m, g_m = mod[:, 0:d], mod[:, d:2 * d], mod[:, 2 * d:3 * d]
    outs = _proj(x, mix_norm.reshape(1, d), sc_m, sh_m, w_in.astype(BF16),
                 q_norm.reshape(1, HEAD_DIM), k_norm.reshape(1, HEAD_DIM), rope)
    y_sgu = _sgu(outs[0], sgu_norm.reshape(1, -1), sgu_w, sgu_b.T, out_norm_sgu.reshape(1, -1))
    y_att = _attn(outs[1:], out_norm_att.reshape(1, -1))
    return _wout(y_sgu, y_att, w_out.astype(BF16), x, g_m)


def _ffn(x, mod, layer, ffn_norm, router_w, router_bias, exp_gate, exp_up, exp_down,
         shared_gate, shared_up, shared_down):
    t, d = x.shape
    sh_f, sc_f, g_f = mod[:, 3 * d:4 * d], mod[:, 4 * d:5 * d], mod[:, 5 * d:6 * d]
    h_bf, hp, idx_t, wt_t, rank_t, cnt = _router(x, ffn_norm.reshape(1, d), sc_f, sh_f, router_w.T,
                                                 router_bias.reshape(N_EXPERTS, 1))
    counts = cnt[:, 0].astype(I32)
    items, seg_start = _work_items(counts, t * TOP_K)
    dest_t = _dest(idx_t, rank_t, seg_start)
    inv = jnp.argsort(dest_t.reshape(-1)).astype(I32)
    yt = _fused_moe(items, inv, hp, layer, exp_gate, exp_up, exp_down)
    return _finish(yt, h_bf, wt_t.T, x, g_f, shared_gate.astype(BF16), shared_up.astype(BF16),
                   shared_down.astype(BF16))


def kernel(x, c, ada_w, ada_b, mix_norm, w_in, sgu_norm, sgu_w, sgu_b, q_norm, k_norm, out_norm_sgu,
           out_norm_att, w_out, ffn_norm, router_w, router_bias, exp_gate, exp_up, exp_down,
           shared_gate, shared_up, shared_down):
    b, s, d = x.shape
    assert b == 1 and s % ATT_TILE == 0 and d == 2 * ROW_SUB * LANES
    mods = _ada(c, ada_w, ada_b)
    rope = _rope_tables(s)
    xf = x.reshape(s, d)
    for l in range(ada_w.shape[0]):
        xf = _mixer(xf, mods[l], rope, mix_norm[l], w_in[l], sgu_norm[l], sgu_w[l], sgu_b[l], q_norm[l],
                    k_norm[l], out_norm_sgu[l], out_norm_att[l], w_out[l])
        xf = _ffn(xf, mods[l], l, ffn_norm[l], router_w[l], router_bias[l], exp_gate, exp_up,
                  exp_down, shared_gate[l], shared_up[l], shared_down[l])
    return xf.reshape(b, s, d)
```

```python
import functools

import jax
import jax.numpy as jnp
from jax import lax
from jax.experimental import pallas as pl
from jax.experimental.pallas import tpu as pltpu

F32 = jnp.float32
BF16 = jnp.bfloat16
U32 = jnp.uint32
I32 = jnp.int32

HEAD_DIM = 128
N_HEADS = 8
SGU_CHUNK = 128
DILATIONS = (1, 4, 16)
ATT_SPAN = 128
ATT_TILE = 2048
LOG2_E = 1.4426950408889634
ROPE_THETA = 500000.0
ROPE_DIM = HEAD_DIM // 4
N_EXPERTS = 64
TOP_K = 8
N_GROUPS = 8
TOPK_GROUPS = 4
ROUTED_SCALE = 2.5
NORM_EPS = 1e-6
N_MOD = 6

MOE_BLOCK = 512
TOKEN_TILE = 256
HI_MASK = 0xFFFF0000
LANES = 128
ROW_SUB = 8

VMEM_LIMIT = 56 * 1024 * 1024

NT_DIMS = (((1,), (1,)), ((), ()))


def _cparams(sem):
    return pltpu.CompilerParams(dimension_semantics=sem, vmem_limit_bytes=VMEM_LIMIT)


def _rms(x):
    return x * lax.rsqrt(jnp.mean(x * x, axis=-1, keepdims=True) + NORM_EPS)


def _gelu(x):
    return 0.5 * x * (1.0 + lax.erf(x * 0.7071067811865476))


def _pack_pair(lo_f32, hi_f32):
    lo = pltpu.bitcast(lo_f32.astype(BF16).astype(F32), U32)
    hi = pltpu.bitcast(hi_f32.astype(BF16).astype(F32), U32)
    return (lo >> 16) | (hi & jnp.uint32(HI_MASK))


def _unpack_pair(u):
    lo = pltpu.bitcast(u << 16, F32)
    hi = pltpu.bitcast(u & jnp.uint32(HI_MASK), F32)
    return lo, hi


def _store_rows(ref, base, n, lo_f32, hi_f32):
    for c in range(ROW_SUB):
        cols = slice(c * LANES, (c + 1) * LANES)
        ref[pl.ds(base + c, n, stride=ROW_SUB), :] = _pack_pair(lo_f32[:, cols], hi_f32[:, cols])


def _load_rows(ref, base, n):
    los, his = [], []
    for c in range(ROW_SUB):
        lo, hi = _unpack_pair(ref[pl.ds(base + c, n, stride=ROW_SUB), :])
        los.append(lo)
        his.append(hi)
    return jnp.concatenate(los, axis=1), jnp.concatenate(his, axis=1)


def _ada_kernel(c_ref, w_ref, b_ref, o_ref):
    d, tn = w_ref.shape

    def body(i, acc):
        r = pl.multiple_of(i * 8, 8)
        cc = c_ref[pl.ds(r, 8), :]
        return acc + (cc * jax.nn.sigmoid(cc)) * w_ref[pl.ds(r, 8), :]

    acc = lax.fori_loop(0, d // 8, body, jnp.zeros((8, tn), F32), unroll=8)
    o_ref[...] = jnp.sum(acc, axis=0, keepdims=True) + b_ref[...]


def _ada(c, ada_w, ada_b):
    nl, d, n = ada_w.shape
    tn = 1536
    return pl.pallas_call(
        _ada_kernel,
        grid=(nl, n // tn),
        in_specs=[pl.BlockSpec((d, 1), lambda l, j: (0, 0)),
                  pl.BlockSpec((None, d, tn), lambda l, j: (l, 0, j)),
                  pl.BlockSpec((None, 1, tn), lambda l, j: (l, 0, j))],
        out_specs=pl.BlockSpec((None, 1, tn), lambda l, j: (l, 0, j)),
        out_shape=jax.ShapeDtypeStruct((nl, 1, n), F32),
        compiler_params=_cparams(("arbitrary", "arbitrary")),
        name="ada",
    )(c.reshape(d, 1), ada_w, ada_b.reshape(nl, 1, n))


def _proj_kernel(x_ref, g_ref, sc_ref, sh_ref, w_ref, qg_ref, kg_ref, rc_ref, ra_ref, rb_ref,
                 uv_ref, q1_ref, q4_ref, q16_ref, k1_ref, k4_ref, k16_ref, v1_ref, v4_ref, v16_ref,
                 h_s, st_s):
    tm = x_ref.shape[0]
    pair = 2 * HEAD_DIM
    aw = N_HEADS * HEAD_DIM
    y = _rms(x_ref[...]) * g_ref[...]
    h_s[...] = (y * (1.0 + sc_ref[...]) + sh_ref[...]).astype(BF16)

    for p in range(uv_ref.shape[1] // pair):
        cols = slice(p * pair, (p + 1) * pair)
        uv_ref[:, cols] = jnp.dot(h_s[...], w_ref[:, cols], preferred_element_type=F32).astype(BF16)

    groups = ((q1_ref, q4_ref, q16_ref, qg_ref, LOG2_E * HEAD_DIM ** -0.5),
              (k1_ref, k4_ref, k16_ref, kg_ref, 1.0),
              (v1_ref, v4_ref, v16_ref, None, 1.0))
    for gi, (o1, o4, o16, gain_ref, scale) in enumerate(groups):
        for p in range(N_HEADS // 2):
            c0 = uv_ref.shape[1] + gi * aw + p * pair
            acc = jnp.dot(h_s[...], w_ref[:, c0:c0 + pair], preferred_element_type=F32)
            for hh in range(2):
                h = 2 * p + hh
                a = acc[:, hh * HEAD_DIM:(hh + 1) * HEAD_DIM]
                if gain_ref is not None:
                    a = _rms(a) * gain_ref[...]
                    a = (a * rc_ref[...] + pltpu.roll(a, ROPE_DIM // 2, 1) * ra_ref[...]
                         + pltpu.roll(a, HEAD_DIM - ROPE_DIM // 2, 1) * rb_ref[...]) * scale
                o1[:, h * HEAD_DIM:(h + 1) * HEAD_DIM] = a.astype(BF16)
                stage = st_s.at[gi * N_HEADS + h]
                stage[...] = a
                for d, o in ((4, o4), (16, o16)):
                    for r in range(d):
                        c = (h * d + r) * HEAD_DIM
                        o[:, c:c + HEAD_DIM] = stage[pl.ds(r, tm // d, stride=d), :].astype(BF16)


def _proj(x, gain, sc, sh, w_bf, q_gain, k_gain, rope):
    s, d = x.shape
    n = w_bf.shape[1]
    tm = 256
    aw = N_HEADS * HEAD_DIM
    row = lambda i: (i, 0)
    vec = lambda i: (0, 0)
    out_shape = [jax.ShapeDtypeStruct((s, 2 * aw), BF16)]
    out_specs = [pl.BlockSpec((tm, 2 * aw), row)]
    for _ in range(3):
        for dil in DILATIONS:
            out_shape.append(jax.ShapeDtypeStruct((s // dil, dil * aw), BF16))
            out_specs.append(pl.BlockSpec((tm // dil, dil * aw), row))
    return pl.pallas_call(
        _proj_kernel,
        grid=(s // tm,),
        in_specs=[pl.BlockSpec((tm, d), row), pl.BlockSpec((1, d), vec), pl.BlockSpec((1, d), vec),
                  pl.BlockSpec((1, d), vec),
                  pl.BlockSpec((d, n), vec, pipeline_mode=pl.Buffered(1)),
                  pl.BlockSpec((1, HEAD_DIM), vec), pl.BlockSpec((1, HEAD_DIM), vec),
                  pl.BlockSpec((tm, HEAD_DIM), row), pl.BlockSpec((tm, HEAD_DIM), row),
                  pl.BlockSpec((tm, HEAD_DIM), row)],
        out_specs=out_specs,
        out_shape=out_shape,
        scratch_shapes=[pltpu.VMEM((tm, d), BF16), pltpu.VMEM((3 * N_HEADS, tm, HEAD_DIM), F32)],
        compiler_params=_cparams(("arbitrary",)),
        name="proj",
    )(x, gain, sc, sh, w_bf, q_gain, k_gain, *rope)


def _sgu_kernel(uv_ref, gn_ref, w_ref, b_ref, go_ref, o_ref):
    tm = uv_ref.shape[0]
    width = N_HEADS * HEAD_DIM
    ii = lax.broadcasted_iota(I32, (SGU_CHUNK, SGU_CHUNK), 0)
    jj = lax.broadcasted_iota(I32, (SGU_CHUNK, SGU_CHUNK), 1)
    causal = jj <= ii
    for g in range(N_HEADS):
        cols = slice(g * HEAD_DIM, (g + 1) * HEAD_DIM)
        wg = jnp.where(causal, w_ref[g], 0.0).astype(BF16)
        u = _gelu(uv_ref[:, cols].astype(F32))
        v = _gelu(uv_ref[:, width + g * HEAD_DIM:width + (g + 1) * HEAD_DIM].astype(F32))
        vn = (_rms(v) * gn_ref[:, cols]).astype(BF16)
        bias = b_ref[:, g:g + 1]
        for n in range(tm // SGU_CHUNK):
            rows = slice(n * SGU_CHUNK, (n + 1) * SGU_CHUNK)
            mixed = jnp.dot(wg, vn[rows], preferred_element_type=F32) + bias
            y = u[rows] * mixed
            o_ref[rows, cols] = (_rms(y) * go_ref[:, cols]).astype(BF16)


def _sgu(uv, sgu_norm, sgu_w, sgu_b_t, out_norm):
    s = uv.shape[0]
    width = N_HEADS * HEAD_DIM
    tm = 512
    vec = lambda i: (0, 0)
    return pl.pallas_call(
        _sgu_kernel,
        grid=(s // tm,),
        in_specs=[pl.BlockSpec((tm, 2 * width), lambda i: (i, 0)),
                  pl.BlockSpec((1, width), vec),
                  pl.BlockSpec((N_HEADS, SGU_CHUNK, SGU_CHUNK), lambda i: (0, 0, 0)),
                  pl.BlockSpec((SGU_CHUNK, N_HEADS), vec),
                  pl.BlockSpec((1, width), vec)],
        out_specs=pl.BlockSpec((tm, width), lambda i: (i, 0)),
        out_shape=jax.ShapeDtypeStruct((s, width), BF16),
        compiler_params=_cparams(("arbitrary",)),
        name="sgu",
    )(uv, sgu_norm, sgu_w, sgu_b_t, out_norm)


def _attn_kernel(q1, q4, q16, k1, k4, k16, k1p, k4p, k16p, v1, v4, v16, v1p, v4p, v16p, go_ref,
                 o_ref, kb1, kb4, kb16, vb1, vb4, vb16, os1, os4, os16, ls1, ls4, ls16):
    first = pl.program_id(0) == 0
    blk = ATT_SPAN
    for buf, prev, cur in ((kb1, k1p, k1), (kb4, k4p, k4), (kb16, k16p, k16),
                           (vb1, v1p, v1), (vb4, v4p, v4), (vb16, v16p, v16)):
        buf[0:blk, :] = prev[...]
        buf[blk:, :] = cur[...]

    qi = lax.broadcasted_iota(I32, (blk, 2 * blk), 0)
    kj = lax.broadcasted_iota(I32, (blk, 2 * blk), 1)
    dist = qi + blk - kj
    band = (dist >= 0) & (dist <= ATT_SPAN)
    neg = jnp.where(band, 0.0, -jnp.inf)
    neg0 = jnp.where(first, jnp.where(band & (kj >= blk), 0.0, -jnp.inf), neg)

    for d, q_ref, kb, vb, o_s, l_s in ((1, q1, kb1, vb1, os1, ls1), (4, q4, kb4, vb4, os4, ls4),
                                       (16, q16, kb16, vb16, os16, ls16)):
        nb = ATT_TILE // (blk * d)
        for r in range(d):
            cols = slice(r * HEAD_DIM, (r + 1) * HEAD_DIM)
            for b in range(nb):
                q = q_ref[b * blk:(b + 1) * blk, cols]
                kk = kb[b * blk:(b + 2) * blk, cols]
                vv = vb[b * blk:(b + 2) * blk, cols]
                s = lax.dot_general(q, kk, NT_DIMS, preferred_element_type=F32)
                s = s + (neg0 if b == 0 else neg)
                m = jnp.max(s, axis=-1, keepdims=True)
                e = jnp.exp2(s - m)
                l = jnp.sum(e, axis=-1, keepdims=True)
                o = jnp.dot(e.astype(BF16), vv, preferred_element_type=F32) / l
                lse = jnp.broadcast_to(m + jnp.log2(l), (blk, HEAD_DIM))
                if d == 1:
                    o_s[b * blk:(b + 1) * blk, :] = o
                    l_s[b * blk:(b + 1) * blk, :] = lse
                else:
                    o_s[pl.ds(d * b * blk + r, blk, stride=d), :] = o
                    l_s[pl.ds(d * b * blk + r, blk, stride=d), :] = lse

    step = 256
    for c in range(ATT_TILE // step):
        rows = slice(c * step, (c + 1) * step)
        l1, l4, l16 = ls1[rows, :], ls4[rows, :], ls16[rows, :]
        mx = jnp.maximum(l1, jnp.maximum(l4, l16))
        w1, w4, w16 = jnp.exp2(l1 - mx), jnp.exp2(l4 - mx), jnp.exp2(l16 - mx)
        o = (w1 * os1[rows, :] + w4 * os4[rows, :] + w16 * os16[rows, :]) / (w1 + w4 + w16)
        o_ref[rows, :] = (_rms(o) * go_ref[...]).astype(BF16)


def _attn(qkv, out_norm):
    q1, q4, q16, k1, k4, k16, v1, v4, v16 = qkv
    s = q1.shape[0]
    nt = s // ATT_TILE
    blk = ATT_SPAN

    def cur(d):
        return pl.BlockSpec((ATT_TILE // d, d * HEAD_DIM), lambda i, h: (i, h))

    def prev(d):
        per = ATT_TILE // (d * blk)
        return pl.BlockSpec((blk, d * HEAD_DIM), lambda i, h: (jnp.maximum(i * per - 1, 0), h))

    in_specs = ([cur(d) for d in DILATIONS] + [cur(d) for d in DILATIONS] + [prev(d) for d in DILATIONS]
                + [cur(d) for d in DILATIONS] + [prev(d) for d in DILATIONS]
                + [pl.BlockSpec((1, HEAD_DIM), lambda i, h: (0, h))])
    kv_bufs = [pltpu.VMEM((blk + ATT_TILE // d, d * HEAD_DIM), BF16) for d in DILATIONS]
    acc_bufs = [pltpu.VMEM((ATT_TILE, HEAD_DIM), F32) for _ in DILATIONS]
    return pl.pallas_call(
        _attn_kernel,
        grid=(nt, N_HEADS),
        in_specs=in_specs,
        out_specs=pl.BlockSpec((ATT_TILE, HEAD_DIM), lambda i, h: (i, h)),
        out_shape=jax.ShapeDtypeStruct((s, N_HEADS * HEAD_DIM), BF16),
        scratch_shapes=kv_bufs + kv_bufs + acc_bufs + acc_bufs,
        compiler_params=_cparams(("arbitrary", "arbitrary")),
        name="attn",
    )(q1, q4, q16, k1, k4, k16, k1, k4, k16, v1, v4, v16, v1, v4, v16, out_norm)


def _wout_kernel(a1_ref, a2_ref, w_ref, x_ref, g_ref, o_ref):
    half = a1_ref.shape[1]
    acc = jnp.dot(a1_ref[...], w_ref[0:half, :], preferred_element_type=F32)
    acc = acc + jnp.dot(a2_ref[...], w_ref[half:2 * half, :], preferred_element_type=F32)
    o_ref[...] = x_ref[...] + g_ref[...] * acc


def _wout(y_sgu, y_att, w_bf, x, gate):
    s, d = x.shape
    half = y_sgu.shape[1]
    tm, tn = 1024, 1024
    return pl.pallas_call(
        _wout_kernel,
        grid=(s // tm, d // tn),
        in_specs=[pl.BlockSpec((tm, half), lambda i, j: (i, 0)),
                  pl.BlockSpec((tm, half), lambda i, j: (i, 0)),
                  pl.BlockSpec((2 * half, tn), lambda i, j: (0, j)),
                  pl.BlockSpec((tm, tn), lambda i, j: (i, j)),
                  pl.BlockSpec((1, tn), lambda i, j: (0, j))],
        out_specs=pl.BlockSpec((tm, tn), lambda i, j: (i, j)),
        out_shape=jax.ShapeDtypeStruct((s, d), F32),
        compiler_params=_cparams(("arbitrary", "arbitrary")),
        name="wout",
    )(y_sgu, y_att, w_bf, x, gate)


def _router_kernel(x_ref, g_ref, sc_ref, sh_ref, rwt_ref, rb_ref,
                   h_ref, idx_ref, wt_ref, rank_ref, cnt_ref, carry_s):
    tm = x_ref.shape[0]
    gsz = N_EXPERTS // N_GROUPS

    @pl.when(pl.program_id(0) == 0)
    def _():
        carry_s[...] = jnp.zeros_like(carry_s)

    h = (_rms(x_ref[...]) * g_ref[...]) * (1.0 + sc_ref[...]) + sh_ref[...]
    h_ref[...] = h.astype(BF16)

    hh = h.astype(BF16)
    hl = (h - hh.astype(F32)).astype(BF16)
    rw = rwt_ref[...]
    rh = rw.astype(BF16)
    rl = (rw - rh.astype(F32)).astype(BF16)
    dg = functools.partial(lax.dot_general, dimension_numbers=NT_DIMS, preferred_element_type=F32)
    logits = dg(rh, hh) + dg(rl, hh) + dg(rh, hl)
    scores = jax.nn.sigmoid(logits)
    sel = scores + rb_ref[...]

    io8 = lax.broadcasted_iota(I32, (gsz, tm), 0)
    grp, gscore = [], []
    for g in range(N_GROUPS):
        sg = sel[g * gsz:(g + 1) * gsz, :]
        m1 = jnp.max(sg, axis=0, keepdims=True)
        i1 = jnp.min(jnp.where(sg == m1, io8, gsz), axis=0, keepdims=True)
        m2 = jnp.max(jnp.where(io8 == i1, -jnp.inf, sg), axis=0, keepdims=True)
        grp.append(sg)
        gscore.append(m1 + m2)

    parts = []
    for g in range(N_GROUPS):
        beaten = jnp.zeros((1, tm), I32)
        for g2 in range(N_GROUPS):
            if g2 == g:
                continue
            b = (gscore[g2] >= gscore[g]) if g2 < g else (gscore[g2] > gscore[g])
            beaten = beaten + b.astype(I32)
        parts.append(jnp.where(beaten < TOPK_GROUPS, grp[g], -jnp.inf))
    masked = jnp.concatenate(parts, axis=0)

    io = lax.broadcasted_iota(I32, (N_EXPERTS, tm), 0)
    chosen = jnp.zeros((N_EXPERTS, tm), F32)
    idxs, wts = [], []
    for _ in range(TOP_K):
        m = jnp.max(masked, axis=0, keepdims=True)
        am = jnp.min(jnp.where(masked == m, io, N_EXPERTS), axis=0, keepdims=True)
        hit = io == am
        wts.append(jnp.sum(jnp.where(hit, scores, 0.0), axis=0, keepdims=True))
        idxs.append(am)
        chosen = chosen + hit.astype(F32)
        masked = jnp.where(hit, -jnp.inf, masked)
    wsum = wts[0]
    for k in range(1, TOP_K):
        wsum = wsum + wts[k]
    for k in range(TOP_K):
        idx_ref[k:k + 1, :] = idxs[k]
        wt_ref[k:k + 1, :] = wts[k] / wsum * ROUTED_SCALE

    t0 = lax.broadcasted_iota(I32, (tm, tm), 0)
    t1 = lax.broadcasted_iota(I32, (tm, tm), 1)
    before = (t0 < t1).astype(BF16)
    pre = jnp.dot(chosen.astype(BF16), before, preferred_element_type=F32) + carry_s[...]
    for k in range(TOP_K):
        rank_ref[k:k + 1, :] = jnp.sum(jnp.where(io == idxs[k], pre, 0.0), axis=0,
                                       keepdims=True).astype(I32)
    carry_s[...] = carry_s[...] + jnp.sum(chosen, axis=1, keepdims=True)
    cnt_ref[...] = jnp.broadcast_to(carry_s[...], cnt_ref.shape)


def _router(x, gain, sc, sh, rw_t, rbias):
    t, d = x.shape
    tm = 512
    vec = lambda i: (0, 0)
    tok = lambda i: (0, i)
    return pl.pallas_call(
        _router_kernel,
        grid=(t // tm,),
        in_specs=[pl.BlockSpec((tm, d), lambda i: (i, 0)), pl.BlockSpec((1, d), vec),
                  pl.BlockSpec((1, d), vec), pl.BlockSpec((1, d), vec),
                  pl.BlockSpec((N_EXPERTS, d), vec), pl.BlockSpec((N_EXPERTS, 1), vec)],
        out_specs=[pl.BlockSpec((tm, d), lambda i: (i, 0)), pl.BlockSpec((TOP_K, tm), tok),
                   pl.BlockSpec((TOP_K, tm), tok), pl.BlockSpec((TOP_K, tm), tok),
                   pl.BlockSpec((N_EXPERTS, HEAD_DIM), vec)],
        out_shape=[jax.ShapeDtypeStruct((t, d), BF16), jax.ShapeDtypeStruct((TOP_K, t), I32),
                   jax.ShapeDtypeStruct((TOP_K, t), F32), jax.ShapeDtypeStruct((TOP_K, t), I32),
                   jax.ShapeDtypeStruct((N_EXPERTS, HEAD_DIM), F32)],
        scratch_shapes=[pltpu.VMEM((N_EXPERTS, 1), F32)],
        compiler_params=_cparams(("arbitrary",)),
        name="router",
    )(x, gain, sc, sh, rw_t, rbias)


def _dest_kernel(idx_ref, rank_ref, start_ref, o_ref):
    tm = idx_ref.shape[1]
    io = lax.broadcasted_iota(I32, (N_EXPERTS, tm), 0)
    for k in range(TOP_K):
        seg = jnp.sum(jnp.where(io == idx_ref[k:k + 1, :], start_ref[...], 0), axis=0, keepdims=True)
        o_ref[k:k + 1, :] = (seg + rank_ref[k:k + 1, :]) * ROW_SUB


def _dest(idx_t, rank_t, seg_start):
    t = idx_t.shape[1]
    tm = 2048
    tok = lambda i: (0, i)
    return pl.pallas_call(
        _dest_kernel,
        grid=(t // tm,),
        in_specs=[pl.BlockSpec((TOP_K, tm), tok), pl.BlockSpec((TOP_K, tm), tok),
                  pl.BlockSpec((N_EXPERTS, 1), lambda i: (0, 0))],
        out_specs=pl.BlockSpec((TOP_K, tm), tok),
        out_shape=jax.ShapeDtypeStruct((TOP_K, t), I32),
        compiler_params=_cparams(("arbitrary",)),
        name="dest",
    )(idx_t, rank_t, seg_start.reshape(N_EXPERTS, 1))


def _dispatch_kernel(dest_s, h_ref, sg_ref, su_ref, sd_ref, xs_hbm, sh_ref, pk_s, sem_row):
    tm, d = h_ref.shape
    n = TOP_K * tm
    i = pl.program_id(0)
    last = pl.num_programs(0) - 1
    cur = i % 2
    base = i * n

    def row_copy(src_row, dst_row, buf):
        return pltpu.make_async_copy(pk_s.at[buf, pl.ds(src_row, ROW_SUB)],
                                     xs_hbm.at[pl.ds(dst_row, ROW_SUB)], sem_row.at[buf])

    def wait_all(buf):
        def wait(j, c):
            row_copy(0, 0, buf).wait()
            return c
        lax.fori_loop(0, n, wait, 0, unroll=8)

    @pl.when(i >= 2)
    def _():
        wait_all(cur)

    _store_rows(pk_s.at[cur], 0, tm, h_ref[:, 0:d // 2].astype(F32), h_ref[:, d // 2:d].astype(F32))

    vals = {}

    def p_gate():
        vals["gate"] = jnp.dot(h_ref[...], sg_ref[...], preferred_element_type=F32)

    def p_up():
        up = jnp.dot(h_ref[...], su_ref[...], preferred_element_type=F32)
        g = vals["gate"]
        vals["hid"] = (g * jax.nn.sigmoid(g) * up).astype(BF16)

    def p_shared(m):
        def run():
            cols = slice(m * 2 * LANES, (m + 1) * 2 * LANES)
            sh_ref[:, cols] = jnp.dot(vals["hid"], sd_ref[:, cols],
                                      preferred_element_type=F32).astype(sh_ref.dtype)
        return run

    pieces = [p_gate, p_up] + [p_shared(m) for m in range(d // (2 * LANES))]
    j = 0
    for idx, piece in enumerate(pieces):
        for _ in range(n // len(pieces) + (1 if idx < n % len(pieces) else 0)):
            row_copy((j % tm) * ROW_SUB, pl.multiple_of(dest_s[base + j], ROW_SUB), cur).start(priority=j % 2)
            j += 1
        piece()

    @pl.when(i == last)
    def _():
        wait_all(cur)

    @pl.when((i == last) & (i >= 1))
    def _():
        wait_all(1 - cur)


def _dispatch(dest_flat, h_bf, sg_bf, su_bf, sd_bf):
    t, d = h_bf.shape
    f = sg_bf.shape[1]
    tm = TOKEN_TILE
    row = lambda i, dest: (i, 0)
    vec = lambda i, dest: (0, 0)
    grid_spec = pltpu.PrefetchScalarGridSpec(
        num_scalar_prefetch=1,
        grid=(t // tm,),
        in_specs=[pl.BlockSpec((tm, d), row),
                  pl.BlockSpec((d, f), vec), pl.BlockSpec((d, f), vec), pl.BlockSpec((f, d), vec)],
        out_specs=[pl.BlockSpec(memory_space=pl.ANY), pl.BlockSpec((tm, d), row)],
        scratch_shapes=[pltpu.VMEM((2, tm * ROW_SUB, LANES), U32), pltpu.SemaphoreType.DMA((2,))])
    return pl.pallas_call(
        _dispatch_kernel,
        grid_spec=grid_spec,
        out_shape=[jax.ShapeDtypeStruct((t * TOP_K * ROW_SUB, LANES), U32),
                   jax.ShapeDtypeStruct((t, d), BF16)],
        compiler_params=_cparams(("arbitrary",)),
        name="dispatch",
    )(dest_flat, h_bf, sg_bf, su_bf, sd_bf)


KIND_EMPTY, KIND_PLAIN, KIND_FRESH, KIND_FIRST = 0, 1, 2, 3


def _work_items(counts, m):
    bm = MOE_BLOCK
    nb = m // bm
    seg_end = jnp.cumsum(counts)
    seg_start = seg_end - counts
    starts = jnp.sort(jnp.concatenate([jnp.arange(nb, dtype=I32) * bm, seg_start]))
    ends = jnp.concatenate([starts[1:], jnp.full((1,), m, I32)])
    length = ends - starts
    blk = jnp.minimum(starts // bm, nb - 1)
    ex = jnp.sum((seg_end[None, :] <= starts[:, None]).astype(I32), axis=1)
    ex = jnp.minimum(ex, N_EXPERTS - 1)
    ex = lax.cummax(jnp.where(length > 0, ex, 0))
    lo = starts - blk * bm
    hi = lo + length
    one = jnp.ones((1,), I32)
    newe = jnp.concatenate([one, (ex[1:] != ex[:-1]).astype(I32)])
    last_blk = lax.cummax(jnp.where(length > 0, blk, -1))
    prev_blk = jnp.concatenate([jnp.full((1,), -1, I32), last_blk[:-1]])
    fresh = (length > 0) & (blk != prev_blk)
    kind = jnp.where(length == 0, KIND_EMPTY,
                     jnp.where(fresh, jnp.where(blk == 0, KIND_FIRST, KIND_FRESH), KIND_PLAIN)).astype(I32)
    cand = jnp.where(newe == 1, ex, N_EXPERTS)
    nxt = jnp.concatenate([lax.cummin(cand[::-1])[::-1][1:], jnp.full((1,), N_EXPERTS, I32)])
    wsel = jnp.where(newe == 1, ex, jnp.where(nxt < N_EXPERTS, nxt, ex))
    return (blk, wsel, lo, hi, kind, newe), seg_start


def _fused_moe_kernel(blk_s, wsel_s, lo_s, hi_s, kind_s, newe_s, inv_s,
                      x_ref, wg_ref, wu_ref, wd_ref, yt_hbm,
                      y_s, acc_s, wg_s, wu_s, wd_s, sem_s):
    w = pl.program_id(0)
    bm, d = acc_s.shape
    half = d // 2
    nb = inv_s.shape[0] // bm
    last_buf = (nb - 1) % 2
    b = blk_s[w]
    cur = b % 2
    kind = kind_s[w]
    lo, hi = lo_s[w], hi_s[w]

    def scatter_copy(src_row, dst_row, buf):
        return pltpu.make_async_copy(y_s.at[buf, pl.ds(src_row, ROW_SUB)],
                                     yt_hbm.at[pl.ds(dst_row, ROW_SUB)], sem_s.at[buf])

    def start_scatter(block, r, buf):
        j = inv_s[block * bm + r]
        scatter_copy(pl.multiple_of(r * ROW_SUB, ROW_SUB), pl.multiple_of(j * ROW_SUB, ROW_SUB),
                     buf).start(priority=r % 2 if isinstance(r, int) else 0)

    def wait_rows(copy, buf):
        def body(r, c):
            copy(0, 0, buf).wait()
            return c
        lax.fori_loop(0, bm, body, 0, unroll=8)

    def loop_rows(start, block, buf):
        def body(r, c):
            start(block, r, buf)
            return c
        lax.fori_loop(0, bm, body, 0, unroll=8)

    @pl.when(newe_s[w] == 1)
    def _():
        wg_s[...] = wg_ref[...].astype(BF16)
        wu_s[...] = wu_ref[...].astype(BF16)
        wd_s[...] = wd_ref[...].astype(BF16)

    def compute(fresh, dmas):
        chunk = 2 * LANES
        pieces = []
        vals = {}

        def p_load():
            xl, xh = _load_rows(x_ref, 0, bm)
            vals["xl"], vals["xh"] = xl.astype(BF16), xh.astype(BF16)

        def p_in(name, w_s, c):
            def run():
                cols = slice(c * chunk, (c + 1) * chunk)
                vals[name, c] = (jnp.dot(vals["xl"], w_s[0:half, cols], preferred_element_type=F32)
                                 + jnp.dot(vals["xh"], w_s[half:d, cols], preferred_element_type=F32))
            return run

        def p_hid():
            n_c = wg_s.shape[1] // chunk
            gate = jnp.concatenate([vals["g", c] for c in range(n_c)], axis=1)
            up = jnp.concatenate([vals["u", c] for c in range(n_c)], axis=1)
            vals["hid"] = (gate * jax.nn.sigmoid(gate) * up).astype(BF16)

        def p_down(m):
            def run():
                cols = slice(m * chunk, (m + 1) * chunk)
                y = jnp.dot(vals["hid"], wd_s[:, cols], preferred_element_type=F32)
                rows = lax.broadcasted_iota(I32, (bm, 1), 0)
                y = jnp.where((rows >= lo) & (rows < hi), y, 0.0)
                acc_s[:, cols] = y if fresh else acc_s[:, cols] + y
            return run

        def p_pack(c):
            def run():
                lo_cols = slice(c * LANES, (c + 1) * LANES)
                hi_cols = slice(half + c * LANES, half + (c + 1) * LANES)
                y_s[cur, pl.ds(c, bm, stride=ROW_SUB), :] = _pack_pair(acc_s[:, lo_cols], acc_s[:, hi_cols])
            return run

        pieces.append(p_load)
        for c in range(wg_s.shape[1] // chunk):
            pieces.append(p_in("g", wg_s, c))
            pieces.append(p_in("u", wu_s, c))
        pieces.append(p_hid)
        pieces += [p_down(m) for m in range(d // chunk)]
        pieces += [p_pack(c) for c in range(ROW_SUB)]
        n_p = len(pieces)
        k = 0
        for idx, piece in enumerate(pieces):
            for _ in range(len(dmas) // n_p + (1 if idx < len(dmas) % n_p else 0)):
                dmas[k]()
                k += 1
            piece()

    @pl.when(kind == KIND_FIRST)
    def _():
        compute(True, [])

    @pl.when(kind == KIND_FRESH)
    def _():
        @pl.when(b >= 2)
        def _():
            wait_rows(scatter_copy, cur)

        compute(True, [functools.partial(start_scatter, b - 1, r, 1 - cur) for r in range(bm)])

    @pl.when(kind == KIND_PLAIN)
    def _():
        compute(False, [])

    @pl.when(w == pl.num_programs(0) - 1)
    def _():
        wait_rows(scatter_copy, 1 - last_buf)
        loop_rows(start_scatter, nb - 1, last_buf)
        wait_rows(scatter_copy, last_buf)


def _fused_moe(items, inv, xs, layer, w_gate, w_up, w_down):
    _, _, d, f = w_gate.shape
    m = inv.shape[0]
    bm = MOE_BLOCK
    assert m // bm >= 2
    wmap = lambda w, blk, ws, lo, hi, kd, ne, iv: (layer, ws[w], 0, 0)
    grid_spec = pltpu.PrefetchScalarGridSpec(
        num_scalar_prefetch=7,
        grid=(items[0].shape[0],),
        in_specs=[pl.BlockSpec((bm * ROW_SUB, LANES), lambda w, blk, ws, lo, hi, kd, ne, iv: (blk[w], 0)),
                  pl.BlockSpec((None, None, d, f), wmap),
                  pl.BlockSpec((None, None, d, f), wmap),
                  pl.BlockSpec((None, None, f, d), wmap)],
        out_specs=pl.BlockSpec(memory_space=pl.ANY),
        scratch_shapes=[pltpu.VMEM((2, bm * ROW_SUB, LANES), U32),
                        pltpu.VMEM((bm, d), F32), pltpu.VMEM((d, f), BF16), pltpu.VMEM((d, f), BF16),
                        pltpu.VMEM((f, d), BF16), pltpu.SemaphoreType.DMA((2,))])
    return pl.pallas_call(
        _fused_moe_kernel,
        grid_spec=grid_spec,
        out_shape=jax.ShapeDtypeStruct((m * ROW_SUB, LANES), U32),
        compiler_params=_cparams(("arbitrary",)),
        name="moe",
    )(*items, inv, xs, w_gate, w_up, w_down)


def _finish_kernel(*refs):
    yt_refs = refs[:TOP_K]
    shared, wt_ref, x_ref, g_ref, o_ref = refs[TOP_K:]
    tm, d = x_ref.shape
    half = d // 2
    for c in range(ROW_SUB):
        r_lo = jnp.zeros((tm, LANES), F32)
        r_hi = jnp.zeros((tm, LANES), F32)
        for k in range(TOP_K):
            lo, hi = _unpack_pair(yt_refs[k][pl.ds(c, tm, stride=ROW_SUB), :])
            wk = wt_ref[:, k:k + 1]
            r_lo = r_lo + wk * lo
            r_hi = r_hi + wk * hi
        for r, off in ((r_lo, c * LANES), (r_hi, half + c * LANES)):
            cols = slice(off, off + LANES)
            o_ref[:, cols] = x_ref[:, cols] + g_ref[:, cols] * (r + shared[:, cols].astype(F32))


def _finish(yt, shared, wt, x, gate):
    t, d = x.shape
    tm = TOKEN_TILE
    nt = t // tm
    row = lambda i: (i, 0)
    vec = lambda i: (0, 0)
    yt_specs = [pl.BlockSpec((tm * ROW_SUB, LANES), functools.partial(lambda k, i: (k * nt + i, 0), k))
                for k in range(TOP_K)]
    return pl.pallas_call(
        _finish_kernel,
        grid=(nt,),
        in_specs=yt_specs + [
            pl.BlockSpec((tm, d), row), pl.BlockSpec((tm, TOP_K), row),
            pl.BlockSpec((tm, d), row), pl.BlockSpec((1, d), vec)],
        out_specs=pl.BlockSpec((tm, d), row),
        out_shape=jax.ShapeDtypeStruct((t, d), F32),
        compiler_params=_cparams(("arbitrary",)),
        name="finish",
    )(*([yt] * TOP_K), shared, wt, x, gate)


def _rope_tables(s):
    half = ROPE_DIM // 2
    inv = ROPE_THETA ** (-jnp.arange(half, dtype=F32) * 2.0 / ROPE_DIM)
    ang = jnp.arange(s, dtype=I32).astype(F32)[:, None] * inv[None, :]
    cos, sin = jnp.cos(ang), jnp.sin(ang)
    pad = HEAD_DIM - ROPE_DIM
    rc = jnp.concatenate([cos, cos, jnp.ones((s, pad), F32)], axis=1)
    ra = jnp.concatenate([jnp.zeros((s, half), F32), sin, jnp.zeros((s, pad), F32)], axis=1)
    rb = jnp.concatenate([-sin, jnp.zeros((s, half + pad), F32)], axis=1)
    return rc, ra, rb


def _mixer(x, mod, rope, mix_norm, w_in, sgu_norm, sgu_w, sgu_b, q_norm, k_norm,
           out_norm_sgu, out_norm_att, w_out):
    d = x.shape[1]
    sh_m, sc_m, g_m = mod[:, 0:d], mod[:, d:2 * d], mod[:, 2 * d:3 * d]
    outs = _proj(x, mix_norm.reshape(1, d), sc_m, sh_m, w_in.astype(BF16),
                 q_norm.reshape(1, HEAD_DIM), k_norm.reshape(1, HEAD_DIM), rope)
    y_sgu = _sgu(outs[0], sgu_norm.reshape(1, -1), sgu_w, sgu_b.T, out_norm_sgu.reshape(1, -1))
    y_att = _attn(outs[1:], out_norm_att.reshape(1, -1))
    return _wout(y_sgu, y_att, w_out.astype(BF16), x, g_m)


def _ffn(x, mod, layer, ffn_norm, router_w, router_bias, exp_gate, exp_up, exp_down,
         shared_gate, shared_up, shared_down):
    t, d = x.shape
    sh_f, sc_f, g_f = mod[:, 3 * d:4 * d], mod[:, 4 * d:5 * d], mod[:, 5 * d:6 * d]
    h_bf, idx_t, wt_t, rank_t, cnt = _router(x, ffn_norm.reshape(1, d), sc_f, sh_f, router_w.T,
                                             router_bias.reshape(N_EXPERTS, 1))
    counts = cnt[:, 0].astype(I32)
    items, seg_start = _work_items(counts, t * TOP_K)
    dest_t = _dest(idx_t, rank_t, seg_start)
    nt = t // TOKEN_TILE
    dest_flat = dest_t.reshape(TOP_K, nt, TOKEN_TILE).transpose(1, 0, 2).reshape(-1)
    xs, shared = _dispatch(dest_flat, h_bf, shared_gate.astype(BF16), shared_up.astype(BF16),
                           shared_down.astype(BF16))
    inv = jnp.argsort(dest_t.reshape(-1)).astype(I32)
    yt = _fused_moe(items, inv, xs, layer, exp_gate, exp_up, exp_down)
    return _finish(yt, shared, wt_t.T, x, g_f)


def kernel(x, c, ada_w, ada_b, mix_norm, w_in, sgu_norm, sgu_w, sgu_b, q_norm, k_norm, out_norm_sgu,
           out_norm_att, w_out, ffn_norm, router_w, router_bias, exp_gate, exp_up, exp_down,
           shared_gate, shared_up, shared_down):
    b, s, d = x.shape
    assert b == 1 and s % ATT_TILE == 0 and d == 2 * ROW_SUB * LANES
    mods = _ada(c, ada_w, ada_b)
    rope = _rope_tables(s)
    xf = x.reshape(s, d)
    for l in range(ada_w.shape[0]):
        xf = _mixer(xf, mods[l], rope, mix_norm[l], w_in[l], sgu_norm[l], sgu_w[l], sgu_b[l], q_norm[l],
                    k_norm[l], out_norm_sgu[l], out_norm_att[l], w_out[l])
        xf = _ffn(xf, mods[l], l, ffn_norm[l], router_w[l], router_bias[l], exp_gate, exp_up,
                  exp_down, shared_gate[l], shared_up[l], shared_down[l])
    return xf.reshape(b, s, d)
```

```python
import functools

import jax
import jax.numpy as jnp
from jax import lax
from jax.experimental import pallas as pl
from jax.experimental.pallas import tpu as pltpu

F32 = jnp.float32
BF16 = jnp.bfloat16
U32 = jnp.uint32
I32 = jnp.int32

HEAD_DIM = 128
N_HEADS = 8
SGU_CHUNK = 128
DILATIONS = (1, 4, 16)
ATT_SPAN = 128
ATT_TILE = 2048
LOG2_E = 1.4426950408889634
ROPE_THETA = 500000.0
ROPE_DIM = HEAD_DIM // 4
N_EXPERTS = 64
TOP_K = 8
N_GROUPS = 8
TOPK_GROUPS = 4
ROUTED_SCALE = 2.5
NORM_EPS = 1e-6
N_MOD = 6

MOE_BLOCK = 512
TOKEN_TILE = 256
HI_MASK = 0xFFFF0000
LANES = 128
ROW_SUB = 8

VMEM_LIMIT = 56 * 1024 * 1024

NT_DIMS = (((1,), (1,)), ((), ()))


def _cparams(sem):
    return pltpu.CompilerParams(dimension_semantics=sem, vmem_limit_bytes=VMEM_LIMIT)


def _rms(x):
    return x * lax.rsqrt(jnp.mean(x * x, axis=-1, keepdims=True) + NORM_EPS)


def _gelu(x):
    return 0.5 * x * (1.0 + lax.erf(x * 0.7071067811865476))


def _pack_pair(lo_f32, hi_f32):
    lo = pltpu.bitcast(lo_f32.astype(BF16).astype(F32), U32)
    hi = pltpu.bitcast(hi_f32.astype(BF16).astype(F32), U32)
    return (lo >> 16) | (hi & jnp.uint32(HI_MASK))


def _unpack_pair(u):
    lo = pltpu.bitcast(u << 16, F32)
    hi = pltpu.bitcast(u & jnp.uint32(HI_MASK), F32)
    return lo, hi


def _store_rows(ref, base, n, lo_f32, hi_f32):
    for c in range(ROW_SUB):
        cols = slice(c * LANES, (c + 1) * LANES)
        ref[pl.ds(base + c, n, stride=ROW_SUB), :] = _pack_pair(lo_f32[:, cols], hi_f32[:, cols])


def _load_rows(ref, base, n):
    los, his = [], []
    for c in range(ROW_SUB):
        lo, hi = _unpack_pair(ref[pl.ds(base + c, n, stride=ROW_SUB), :])
        los.append(lo)
        his.append(hi)
    return jnp.concatenate(los, axis=1), jnp.concatenate(his, axis=1)


def _ada_kernel(c_ref, w_ref, b_ref, o_ref):
    d, tn = w_ref.shape

    def body(i, acc):
        r = pl.multiple_of(i * 8, 8)
        cc = c_ref[pl.ds(r, 8), :]
        return acc + (cc * jax.nn.sigmoid(cc)) * w_ref[pl.ds(r, 8), :]

    acc = lax.fori_loop(0, d // 8, body, jnp.zeros((8, tn), F32), unroll=8)
    o_ref[...] = jnp.sum(acc, axis=0, keepdims=True) + b_ref[...]


def _ada(c, ada_w, ada_b):
    nl, d, n = ada_w.shape
    tn = 1536
    return pl.pallas_call(
        _ada_kernel,
        grid=(nl, n // tn),
        in_specs=[pl.BlockSpec((d, 1), lambda l, j: (0, 0)),
                  pl.BlockSpec((None, d, tn), lambda l, j: (l, 0, j)),
                  pl.BlockSpec((None, 1, tn), lambda l, j: (l, 0, j))],
        out_specs=pl.BlockSpec((None, 1, tn), lambda l, j: (l, 0, j)),
        out_shape=jax.ShapeDtypeStruct((nl, 1, n), F32),
        compiler_params=_cparams(("arbitrary", "arbitrary")),
        name="ada",
    )(c.reshape(d, 1), ada_w, ada_b.reshape(nl, 1, n))


def _proj_kernel(x_ref, g_ref, sc_ref, sh_ref, w_ref, qg_ref, kg_ref, rc_ref, ra_ref, rb_ref,
                 uv_ref, q1_ref, q4_ref, q16_ref, k1_ref, k4_ref, k16_ref, v1_ref, v4_ref, v16_ref,
                 h_s, st_s):
    tm = x_ref.shape[0]
    pair = 2 * HEAD_DIM
    aw = N_HEADS * HEAD_DIM
    y = _rms(x_ref[...]) * g_ref[...]
    h_s[...] = (y * (1.0 + sc_ref[...]) + sh_ref[...]).astype(BF16)

    for p in range(uv_ref.shape[1] // pair):
        cols = slice(p * pair, (p + 1) * pair)
        uv_ref[:, cols] = jnp.dot(h_s[...], w_ref[:, cols], preferred_element_type=F32).astype(BF16)

    groups = ((q1_ref, q4_ref, q16_ref, qg_ref, LOG2_E * HEAD_DIM ** -0.5),
              (k1_ref, k4_ref, k16_ref, kg_ref, 1.0),
              (v1_ref, v4_ref, v16_ref, None, 1.0))
    for gi, (o1, o4, o16, gain_ref, scale) in enumerate(groups):
        for p in range(N_HEADS // 2):
            c0 = uv_ref.shape[1] + gi * aw + p * pair
            acc = jnp.dot(h_s[...], w_ref[:, c0:c0 + pair], preferred_element_type=F32)
            for hh in range(2):
                h = 2 * p + hh
                a = acc[:, hh * HEAD_DIM:(hh + 1) * HEAD_DIM]
                if gain_ref is not None:
                    a = _rms(a) * gain_ref[...]
                    a = (a * rc_ref[...] + pltpu.roll(a, ROPE_DIM // 2, 1) * ra_ref[...]
                         + pltpu.roll(a, HEAD_DIM - ROPE_DIM // 2, 1) * rb_ref[...]) * scale
                o1[:, h * HEAD_DIM:(h + 1) * HEAD_DIM] = a.astype(BF16)
                stage = st_s.at[gi * N_HEADS + h]
                stage[...] = a
                for d, o in ((4, o4), (16, o16)):
                    for r in range(d):
                        c = (h * d + r) * HEAD_DIM
                        o[:, c:c + HEAD_DIM] = stage[pl.ds(r, tm // d, stride=d), :].astype(BF16)


def _proj(x, gain, sc, sh, w_bf, q_gain, k_gain, rope):
    s, d = x.shape
    n = w_bf.shape[1]
    tm = 256
    aw = N_HEADS * HEAD_DIM
    row = lambda i: (i, 0)
    vec = lambda i: (0, 0)
    out_shape = [jax.ShapeDtypeStruct((s, 2 * aw), BF16)]
    out_specs = [pl.BlockSpec((tm, 2 * aw), row)]
    for _ in range(3):
        for dil in DILATIONS:
            out_shape.append(jax.ShapeDtypeStruct((s // dil, dil * aw), BF16))
            out_specs.append(pl.BlockSpec((tm // dil, dil * aw), row))
    return pl.pallas_call(
        _proj_kernel,
        grid=(s // tm,),
        in_specs=[pl.BlockSpec((tm, d), row), pl.BlockSpec((1, d), vec), pl.BlockSpec((1, d), vec),
                  pl.BlockSpec((1, d), vec),
                  pl.BlockSpec((d, n), vec, pipeline_mode=pl.Buffered(1)),
                  pl.BlockSpec((1, HEAD_DIM), vec), pl.BlockSpec((1, HEAD_DIM), vec),
                  pl.BlockSpec((tm, HEAD_DIM), row), pl.BlockSpec((tm, HEAD_DIM), row),
                  pl.BlockSpec((tm, HEAD_DIM), row)],
        out_specs=out_specs,
        out_shape=out_shape,
        scratch_shapes=[pltpu.VMEM((tm, d), BF16), pltpu.VMEM((3 * N_HEADS, tm, HEAD_DIM), F32)],
        compiler_params=_cparams(("arbitrary",)),
        name="proj",
    )(x, gain, sc, sh, w_bf, q_gain, k_gain, *rope)


def _sgu_kernel(uv_ref, gn_ref, w_ref, b_ref, go_ref, o_ref):
    tm = uv_ref.shape[0]
    width = N_HEADS * HEAD_DIM
    ii = lax.broadcasted_iota(I32, (SGU_CHUNK, SGU_CHUNK), 0)
    jj = lax.broadcasted_iota(I32, (SGU_CHUNK, SGU_CHUNK), 1)
    causal = jj <= ii
    for g in range(N_HEADS):
        cols = slice(g * HEAD_DIM, (g + 1) * HEAD_DIM)
        wg = jnp.where(causal, w_ref[g], 0.0).astype(BF16)
        u = _gelu(uv_ref[:, cols].astype(F32))
        v = _gelu(uv_ref[:, width + g * HEAD_DIM:width + (g + 1) * HEAD_DIM].astype(F32))
        vn = (_rms(v) * gn_ref[:, cols]).astype(BF16)
        bias = b_ref[:, g:g + 1]
        for n in range(tm // SGU_CHUNK):
            rows = slice(n * SGU_CHUNK, (n + 1) * SGU_CHUNK)
            mixed = jnp.dot(wg, vn[rows], preferred_element_type=F32) + bias
            y = u[rows] * mixed
            o_ref[rows, cols] = (_rms(y) * go_ref[:, cols]).astype(BF16)


def _sgu(uv, sgu_norm, sgu_w, sgu_b_t, out_norm):
    s = uv.shape[0]
    width = N_HEADS * HEAD_DIM
    tm = 512
    vec = lambda i: (0, 0)
    return pl.pallas_call(
        _sgu_kernel,
        grid=(s // tm,),
        in_specs=[pl.BlockSpec((tm, 2 * width), lambda i: (i, 0)),
                  pl.BlockSpec((1, width), vec),
                  pl.BlockSpec((N_HEADS, SGU_CHUNK, SGU_CHUNK), lambda i: (0, 0, 0)),
                  pl.BlockSpec((SGU_CHUNK, N_HEADS), vec),
                  pl.BlockSpec((1, width), vec)],
        out_specs=pl.BlockSpec((tm, width), lambda i: (i, 0)),
        out_shape=jax.ShapeDtypeStruct((s, width), BF16),
        compiler_params=_cparams(("arbitrary",)),
        name="sgu",
    )(uv, sgu_norm, sgu_w, sgu_b_t, out_norm)


def _attn_kernel(q1, q4, q16, k1, k4, k16, k1p, k4p, k16p, v1, v4, v16, v1p, v4p, v16p, go_ref,
                 o_ref, kb1, kb4, kb16, vb1, vb4, vb16, os1, os4, os16, ls1, ls4, ls16):
    first = pl.program_id(0) == 0
    blk = ATT_SPAN
    for buf, prev, cur in ((kb1, k1p, k1), (kb4, k4p, k4), (kb16, k16p, k16),
                           (vb1, v1p, v1), (vb4, v4p, v4), (vb16, v16p, v16)):
        buf[0:blk, :] = prev[...]
        buf[blk:, :] = cur[...]

    qi = lax.broadcasted_iota(I32, (blk, 2 * blk), 0)
    kj = lax.broadcasted_iota(I32, (blk, 2 * blk), 1)
    dist = qi + blk - kj
    band = (dist >= 0) & (dist <= ATT_SPAN)
    neg = jnp.where(band, 0.0, -jnp.inf)
    neg0 = jnp.where(first, jnp.where(band & (kj >= blk), 0.0, -jnp.inf), neg)

    for d, q_ref, kb, vb, o_s, l_s in ((1, q1, kb1, vb1, os1, ls1), (4, q4, kb4, vb4, os4, ls4),
                                       (16, q16, kb16, vb16, os16, ls16)):
        nb = ATT_TILE // (blk * d)
        for r in range(d):
            cols = slice(r * HEAD_DIM, (r + 1) * HEAD_DIM)
            for b in range(nb):
                q = q_ref[b * blk:(b + 1) * blk, cols]
                kk = kb[b * blk:(b + 2) * blk, cols]
                vv = vb[b * blk:(b + 2) * blk, cols]
                s = lax.dot_general(q, kk, NT_DIMS, preferred_element_type=F32)
                s = s + (neg0 if b == 0 else neg)
                m = jnp.max(s, axis=-1, keepdims=True)
                e = jnp.exp2(s - m)
                l = jnp.sum(e, axis=-1, keepdims=True)
                o = jnp.dot(e.astype(BF16), vv, preferred_element_type=F32) / l
                lse = jnp.broadcast_to(m + jnp.log2(l), (blk, HEAD_DIM))
                if d == 1:
                    o_s[b * blk:(b + 1) * blk, :] = o
                    l_s[b * blk:(b + 1) * blk, :] = lse
                else:
                    o_s[pl.ds(d * b * blk + r, blk, stride=d), :] = o
                    l_s[pl.ds(d * b * blk + r, blk, stride=d), :] = lse

    step = 256
    for c in range(ATT_TILE // step):
        rows = slice(c * step, (c + 1) * step)
        l1, l4, l16 = ls1[rows, :], ls4[rows, :], ls16[rows, :]
        mx = jnp.maximum(l1, jnp.maximum(l4, l16))
        w1, w4, w16 = jnp.exp2(l1 - mx), jnp.exp2(l4 - mx), jnp.exp2(l16 - mx)
        o = (w1 * os1[rows, :] + w4 * os4[rows, :] + w16 * os16[rows, :]) / (w1 + w4 + w16)
        o_ref[rows, :] = (_rms(o) * go_ref[...]).astype(BF16)


def _attn(qkv, out_norm):
    q1, q4, q16, k1, k4, k16, v1, v4, v16 = qkv
    s = q1.shape[0]
    nt = s // ATT_TILE
    blk = ATT_SPAN

    def cur(d):
        return pl.BlockSpec((ATT_TILE // d, d * HEAD_DIM), lambda i, h: (i, h))

    def prev(d):
        per = ATT_TILE // (d * blk)
        return pl.BlockSpec((blk, d * HEAD_DIM), lambda i, h: (jnp.maximum(i * per - 1, 0), h))

    in_specs = ([cur(d) for d in DILATIONS] + [cur(d) for d in DILATIONS] + [prev(d) for d in DILATIONS]
                + [cur(d) for d in DILATIONS] + [prev(d) for d in DILATIONS]
                + [pl.BlockSpec((1, HEAD_DIM), lambda i, h: (0, h))])
    kv_bufs = [pltpu.VMEM((blk + ATT_TILE // d, d * HEAD_DIM), BF16) for d in DILATIONS]
    acc_bufs = [pltpu.VMEM((ATT_TILE, HEAD_DIM), F32) for _ in DILATIONS]
    return pl.pallas_call(
        _attn_kernel,
        grid=(nt, N_HEADS),
        in_specs=in_specs,
        out_specs=pl.BlockSpec((ATT_TILE, HEAD_DIM), lambda i, h: (i, h)),
        out_shape=jax.ShapeDtypeStruct((s, N_HEADS * HEAD_DIM), BF16),
        scratch_shapes=kv_bufs + kv_bufs + acc_bufs + acc_bufs,
        compiler_params=_cparams(("arbitrary", "arbitrary")),
        name="attn",
    )(q1, q4, q16, k1, k4, k16, k1, k4, k16, v1, v4, v16, v1, v4, v16, out_norm)


def _wout_kernel(a1_ref, a2_ref, w_ref, x_ref, g_ref, o_ref):
    half = a1_ref.shape[1]
    acc = jnp.dot(a1_ref[...], w_ref[0:half, :], preferred_element_type=F32)
    acc = acc + jnp.dot(a2_ref[...], w_ref[half:2 * half, :], preferred_element_type=F32)
    o_ref[...] = x_ref[...] + g_ref[...] * acc


def _wout(y_sgu, y_att, w_bf, x, gate):
    s, d = x.shape
    half = y_sgu.shape[1]
    tm, tn = 1024, 1024
    return pl.pallas_call(
        _wout_kernel,
        grid=(s // tm, d // tn),
        in_specs=[pl.BlockSpec((tm, half), lambda i, j: (i, 0)),
                  pl.BlockSpec((tm, half), lambda i, j: (i, 0)),
                  pl.BlockSpec((2 * half, tn), lambda i, j: (0, j)),
                  pl.BlockSpec((tm, tn), lambda i, j: (i, j)),
                  pl.BlockSpec((1, tn), lambda i, j: (0, j))],
        out_specs=pl.BlockSpec((tm, tn), lambda i, j: (i, j)),
        out_shape=jax.ShapeDtypeStruct((s, d), F32),
        compiler_params=_cparams(("arbitrary", "arbitrary")),
        name="wout",
    )(y_sgu, y_att, w_bf, x, gate)


def _router_kernel(x_ref, g_ref, sc_ref, sh_ref, rwt_ref, rb_ref,
                   h_ref, idx_ref, wt_ref, rank_ref, cnt_ref, carry_s):
    tm = x_ref.shape[0]
    gsz = N_EXPERTS // N_GROUPS

    @pl.when(pl.program_id(0) == 0)
    def _():
        carry_s[...] = jnp.zeros_like(carry_s)

    h = (_rms(x_ref[...]) * g_ref[...]) * (1.0 + sc_ref[...]) + sh_ref[...]
    h_ref[...] = h.astype(BF16)

    hh = h.astype(BF16)
    hl = (h - hh.astype(F32)).astype(BF16)
    rw = rwt_ref[...]
    rh = rw.astype(BF16)
    rl = (rw - rh.astype(F32)).astype(BF16)
    dg = functools.partial(lax.dot_general, dimension_numbers=NT_DIMS, preferred_element_type=F32)
    logits = dg(rh, hh) + dg(rl, hh) + dg(rh, hl)
    scores = jax.nn.sigmoid(logits)
    sel = scores + rb_ref[...]

    io8 = lax.broadcasted_iota(I32, (gsz, tm), 0)
    grp, gscore = [], []
    for g in range(N_GROUPS):
        sg = sel[g * gsz:(g + 1) * gsz, :]
        m1 = jnp.max(sg, axis=0, keepdims=True)
        i1 = jnp.min(jnp.where(sg == m1, io8, gsz), axis=0, keepdims=True)
        m2 = jnp.max(jnp.where(io8 == i1, -jnp.inf, sg), axis=0, keepdims=True)
        grp.append(sg)
        gscore.append(m1 + m2)

    parts = []
    for g in range(N_GROUPS):
        beaten = jnp.zeros((1, tm), I32)
        for g2 in range(N_GROUPS):
            if g2 == g:
                continue
            b = (gscore[g2] >= gscore[g]) if g2 < g else (gscore[g2] > gscore[g])
            beaten = beaten + b.astype(I32)
        parts.append(jnp.where(beaten < TOPK_GROUPS, grp[g], -jnp.inf))
    masked = jnp.concatenate(parts, axis=0)

    io = lax.broadcasted_iota(I32, (N_EXPERTS, tm), 0)
    chosen = jnp.zeros((N_EXPERTS, tm), F32)
    idxs, wts = [], []
    for _ in range(TOP_K):
        m = jnp.max(masked, axis=0, keepdims=True)
        am = jnp.min(jnp.where(masked == m, io, N_EXPERTS), axis=0, keepdims=True)
        hit = io == am
        wts.append(jnp.sum(jnp.where(hit, scores, 0.0), axis=0, keepdims=True))
        idxs.append(am)
        chosen = chosen + hit.astype(F32)
        masked = jnp.where(hit, -jnp.inf, masked)
    wsum = wts[0]
    for k in range(1, TOP_K):
        wsum = wsum + wts[k]
    for k in range(TOP_K):
        idx_ref[k:k + 1, :] = idxs[k]
        wt_ref[k:k + 1, :] = wts[k] / wsum * ROUTED_SCALE

    t0 = lax.broadcasted_iota(I32, (tm, tm), 0)
    t1 = lax.broadcasted_iota(I32, (tm, tm), 1)
    before = (t0 < t1).astype(BF16)
    pre = jnp.dot(chosen.astype(BF16), before, preferred_element_type=F32) + carry_s[...]
    for k in range(TOP_K):
        rank_ref[k:k + 1, :] = jnp.sum(jnp.where(io == idxs[k], pre, 0.0), axis=0,
                                       keepdims=True).astype(I32)
    carry_s[...] = carry_s[...] + jnp.sum(chosen, axis=1, keepdims=True)
    cnt_ref[...] = jnp.broadcast_to(carry_s[...], cnt_ref.shape)


def _router(x, gain, sc, sh, rw_t, rbias):
    t, d = x.shape
    tm = 512
    vec = lambda i: (0, 0)
    tok = lambda i: (0, i)
    return pl.pallas_call(
        _router_kernel,
        grid=(t // tm,),
        in_specs=[pl.BlockSpec((tm, d), lambda i: (i, 0)), pl.BlockSpec((1, d), vec),
                  pl.BlockSpec((1, d), vec), pl.BlockSpec((1, d), vec),
                  pl.BlockSpec((N_EXPERTS, d), vec), pl.BlockSpec((N_EXPERTS, 1), vec)],
        out_specs=[pl.BlockSpec((tm, d), lambda i: (i, 0)), pl.BlockSpec((TOP_K, tm), tok),
                   pl.BlockSpec((TOP_K, tm), tok), pl.BlockSpec((TOP_K, tm), tok),
                   pl.BlockSpec((N_EXPERTS, HEAD_DIM), vec)],
        out_shape=[jax.ShapeDtypeStruct((t, d), BF16), jax.ShapeDtypeStruct((TOP_K, t), I32),
                   jax.ShapeDtypeStruct((TOP_K, t), F32), jax.ShapeDtypeStruct((TOP_K, t), I32),
                   jax.ShapeDtypeStruct((N_EXPERTS, HEAD_DIM), F32)],
        scratch_shapes=[pltpu.VMEM((N_EXPERTS, 1), F32)],
        compiler_params=_cparams(("arbitrary",)),
        name="router",
    )(x, gain, sc, sh, rw_t, rbias)


def _dest_kernel(idx_ref, rank_ref, start_ref, o_ref):
    tm = idx_ref.shape[1]
    io = lax.broadcasted_iota(I32, (N_EXPERTS, tm), 0)
    for k in range(TOP_K):
        seg = jnp.sum(jnp.where(io == idx_ref[k:k + 1, :], start_ref[...], 0), axis=0, keepdims=True)
        o_ref[k:k + 1, :] = (seg + rank_ref[k:k + 1, :]) * ROW_SUB


def _dest(idx_t, rank_t, seg_start):
    t = idx_t.shape[1]
    tm = 2048
    tok = lambda i: (0, i)
    return pl.pallas_call(
        _dest_kernel,
        grid=(t // tm,),
        in_specs=[pl.BlockSpec((TOP_K, tm), tok), pl.BlockSpec((TOP_K, tm), tok),
                  pl.BlockSpec((N_EXPERTS, 1), lambda i: (0, 0))],
        out_specs=pl.BlockSpec((TOP_K, tm), tok),
        out_shape=jax.ShapeDtypeStruct((TOP_K, t), I32),
        compiler_params=_cparams(("arbitrary",)),
        name="dest",
    )(idx_t, rank_t, seg_start.reshape(N_EXPERTS, 1))


def _dispatch_kernel(dest_s, h_ref, sg_ref, su_ref, sd_ref, xs_hbm, sh_ref, pk_s, sem_row):
    tm, d = h_ref.shape
    n = TOP_K * tm
    i = pl.program_id(0)
    last = pl.num_programs(0) - 1
    cur = i % 2
    base = i * n

    def row_copy(src_row, dst_row, buf):
        return pltpu.make_async_copy(pk_s.at[buf, pl.ds(src_row, ROW_SUB)],
                                     xs_hbm.at[pl.ds(dst_row, ROW_SUB)], sem_row.at[buf])

    def wait_all(buf):
        def wait(j, c):
            row_copy(0, 0, buf).wait()
            return c
        lax.fori_loop(0, n, wait, 0, unroll=8)

    @pl.when(i >= 2)
    def _():
        wait_all(cur)

    _store_rows(pk_s.at[cur], 0, tm, h_ref[:, 0:d // 2].astype(F32), h_ref[:, d // 2:d].astype(F32))

    vals = {}

    def p_gate():
        vals["gate"] = jnp.dot(h_ref[...], sg_ref[...], preferred_element_type=F32)

    def p_up():
        up = jnp.dot(h_ref[...], su_ref[...], preferred_element_type=F32)
        g = vals["gate"]
        vals["hid"] = (g * jax.nn.sigmoid(g) * up).astype(BF16)

    def p_shared(m):
        def run():
            cols = slice(m * 2 * LANES, (m + 1) * 2 * LANES)
            sh_ref[:, cols] = jnp.dot(vals["hid"], sd_ref[:, cols],
                                      preferred_element_type=F32).astype(sh_ref.dtype)
        return run

    pieces = [p_gate, p_up] + [p_shared(m) for m in range(d // (2 * LANES))]
    j = 0
    for idx, piece in enumerate(pieces):
        for _ in range(n // len(pieces) + (1 if idx < n % len(pieces) else 0)):
            row_copy((j % tm) * ROW_SUB, pl.multiple_of(dest_s[base + j], ROW_SUB), cur).start(priority=j % 2)
            j += 1
        piece()

    @pl.when(i == last)
    def _():
        wait_all(cur)

    @pl.when((i == last) & (i >= 1))
    def _():
        wait_all(1 - cur)


def _dispatch(dest_flat, h_bf, sg_bf, su_bf, sd_bf):
    t, d = h_bf.shape
    f = sg_bf.shape[1]
    tm = TOKEN_TILE
    row = lambda i, dest: (i, 0)
    vec = lambda i, dest: (0, 0)
    grid_spec = pltpu.PrefetchScalarGridSpec(
        num_scalar_prefetch=1,
        grid=(t // tm,),
        in_specs=[pl.BlockSpec((tm, d), row),
                  pl.BlockSpec((d, f), vec), pl.BlockSpec((d, f), vec), pl.BlockSpec((f, d), vec)],
        out_specs=[pl.BlockSpec(memory_space=pl.ANY), pl.BlockSpec((tm, d), row)],
        scratch_shapes=[pltpu.VMEM((2, tm * ROW_SUB, LANES), U32), pltpu.SemaphoreType.DMA((2,))])
    return pl.pallas_call(
        _dispatch_kernel,
        grid_spec=grid_spec,
        out_shape=[jax.ShapeDtypeStruct((t * TOP_K * ROW_SUB, LANES), U32),
                   jax.ShapeDtypeStruct((t, d), BF16)],
        compiler_params=_cparams(("arbitrary",)),
        name="dispatch",
    )(dest_flat, h_bf, sg_bf, su_bf, sd_bf)


KIND_EMPTY, KIND_PLAIN, KIND_FRESH, KIND_FIRST = 0, 1, 2, 3


def _work_items(counts, m):
    bm = MOE_BLOCK
    nb = m // bm
    seg_end = jnp.cumsum(counts)
    seg_start = seg_end - counts
    starts = jnp.sort(jnp.concatenate([jnp.arange(nb, dtype=I32) * bm, seg_start]))
    ends = jnp.concatenate([starts[1:], jnp.full((1,), m, I32)])
    length = ends - starts
    blk = jnp.minimum(starts // bm, nb - 1)
    ex = jnp.sum((seg_end[None, :] <= starts[:, None]).astype(I32), axis=1)
    ex = jnp.minimum(ex, N_EXPERTS - 1)
    ex = lax.cummax(jnp.where(length > 0, ex, 0))
    lo = starts - blk * bm
    hi = lo + length
    one = jnp.ones((1,), I32)
    newe = jnp.concatenate([one, (ex[1:] != ex[:-1]).astype(I32)])
    last_blk = lax.cummax(jnp.where(length > 0, blk, -1))
    prev_blk = jnp.concatenate([jnp.full((1,), -1, I32), last_blk[:-1]])
    fresh = (length > 0) & (blk != prev_blk)
    kind = jnp.where(length == 0, KIND_EMPTY,
                     jnp.where(fresh, jnp.where(blk == 0, KIND_FIRST, KIND_FRESH), KIND_PLAIN)).astype(I32)
    cand = jnp.where(newe == 1, ex, N_EXPERTS)
    nxt = jnp.concatenate([lax.cummin(cand[::-1])[::-1][1:], jnp.full((1,), N_EXPERTS, I32)])
    wsel = jnp.where(newe == 1, ex, jnp.where(nxt < N_EXPERTS, nxt, ex))
    return (blk, wsel, lo, hi, kind, newe), seg_start


def _fused_moe_kernel(blk_s, wsel_s, lo_s, hi_s, kind_s, newe_s, inv_s,
                      x_ref, wg_ref, wu_ref, wd_ref, yt_hbm,
                      y_s, acc_s, wg_s, wu_s, wd_s, sem_s):
    w = pl.program_id(0)
    bm, d = acc_s.shape
    half = d // 2
    nb = inv_s.shape[0] // bm
    last_buf = (nb - 1) % 2
    b = blk_s[w]
    cur = b % 2
    kind = kind_s[w]
    lo, hi = lo_s[w], hi_s[w]

    def scatter_copy(src_row, dst_row, buf):
        return pltpu.make_async_copy(y_s.at[buf, pl.ds(src_row, ROW_SUB)],
                                     yt_hbm.at[pl.ds(dst_row, ROW_SUB)], sem_s.at[buf])

    def start_scatter(block, r, buf):
        j = inv_s[block * bm + r]
        scatter_copy(pl.multiple_of(r * ROW_SUB, ROW_SUB), pl.multiple_of(j * ROW_SUB, ROW_SUB),
                     buf).start(priority=r % 2 if isinstance(r, int) else 0)

    def wait_rows(copy, buf):
        def body(r, c):
            copy(0, 0, buf).wait()
            return c
        lax.fori_loop(0, bm, body, 0, unroll=8)

    def loop_rows(start, block, buf):
        def body(r, c):
            start(block, r, buf)
            return c
        lax.fori_loop(0, bm, body, 0, unroll=8)

    @pl.when(newe_s[w] == 1)
    def _():
        wg_s[...] = wg_ref[...].astype(BF16)
        wu_s[...] = wu_ref[...].astype(BF16)
        wd_s[...] = wd_ref[...].astype(BF16)

    def compute(fresh, dmas):
        chunk = 2 * LANES
        pieces = []
        vals = {}

        def p_load(c):
            def run():
                xl, xh = _unpack_pair(x_ref[pl.ds(c, bm, stride=ROW_SUB), :])
                vals["xl", c], vals["xh", c] = xl.astype(BF16), xh.astype(BF16)
            return run

        def p_cat():
            vals["xl"] = jnp.concatenate([vals["xl", c] for c in range(ROW_SUB)], axis=1)
            vals["xh"] = jnp.concatenate([vals["xh", c] for c in range(ROW_SUB)], axis=1)

        def p_in(name, w_s, c, part):
            def run():
                cols = slice(c * chunk, (c + 1) * chunk)
                if part == 0:
                    vals[name, c] = jnp.dot(vals["xl"], w_s[0:half, cols], preferred_element_type=F32)
                else:
                    vals[name, c] = vals[name, c] + jnp.dot(vals["xh"], w_s[half:d, cols],
                                                            preferred_element_type=F32)
            return run

        def p_hid():
            n_c = wg_s.shape[1] // chunk
            gate = jnp.concatenate([vals["g", c] for c in range(n_c)], axis=1)
            up = jnp.concatenate([vals["u", c] for c in range(n_c)], axis=1)
            vals["hid"] = (gate * jax.nn.sigmoid(gate) * up).astype(BF16)

        def p_down(m):
            def run():
                cols = slice(m * chunk, (m + 1) * chunk)
                y = jnp.dot(vals["hid"], wd_s[:, cols], preferred_element_type=F32)
                rows = lax.broadcasted_iota(I32, (bm, 1), 0)
                y = jnp.where((rows >= lo) & (rows < hi), y, 0.0)
                acc_s[:, cols] = y if fresh else acc_s[:, cols] + y
            return run

        def p_pack(c):
            def run():
                lo_cols = slice(c * LANES, (c + 1) * LANES)
                hi_cols = slice(half + c * LANES, half + (c + 1) * LANES)
                y_s[cur, pl.ds(c, bm, stride=ROW_SUB), :] = _pack_pair(acc_s[:, lo_cols], acc_s[:, hi_cols])
            return run

        pieces += [p_load(c) for c in range(ROW_SUB)]
        pieces.append(p_cat)
        for c in range(wg_s.shape[1] // chunk):
            for part in range(2):
                pieces.append(p_in("g", wg_s, c, part))
                pieces.append(p_in("u", wu_s, c, part))
        pieces.append(p_hid)
        pieces += [p_down(m) for m in range(d // chunk)]
        pieces += [p_pack(c) for c in range(ROW_SUB)]
        n_p = len(pieces)
        k = 0
        for idx, piece in enumerate(pieces):
            for _ in range(len(dmas) // n_p + (1 if idx < len(dmas) % n_p else 0)):
                dmas[k]()
                k += 1
            piece()

    @pl.when(kind == KIND_FIRST)
    def _():
        compute(True, [])

    @pl.when(kind == KIND_FRESH)
    def _():
        @pl.when(b >= 2)
        def _():
            wait_rows(scatter_copy, cur)

        compute(True, [functools.partial(start_scatter, b - 1, r, 1 - cur) for r in range(bm)])

    @pl.when(kind == KIND_PLAIN)
    def _():
        compute(False, [])

    @pl.when(w == pl.num_programs(0) - 1)
    def _():
        wait_rows(scatter_copy, 1 - last_buf)
        loop_rows(start_scatter, nb - 1, last_buf)
        wait_rows(scatter_copy, last_buf)


def _fused_moe(items, inv, xs, layer, w_gate, w_up, w_down):
    _, _, d, f = w_gate.shape
    m = inv.shape[0]
    bm = MOE_BLOCK
    assert m // bm >= 2
    wmap = lambda w, blk, ws, lo, hi, kd, ne, iv: (layer, ws[w], 0, 0)
    grid_spec = pltpu.PrefetchScalarGridSpec(
        num_scalar_prefetch=7,
        grid=(items[0].shape[0],),
        in_specs=[pl.BlockSpec((bm * ROW_SUB, LANES), lambda w, blk, ws, lo, hi, kd, ne, iv: (blk[w], 0)),
                  pl.BlockSpec((None, None, d, f), wmap),
                  pl.BlockSpec((None, None, d, f), wmap),
                  pl.BlockSpec((None, None, f, d), wmap)],
        out_specs=pl.BlockSpec(memory_space=pl.ANY),
        scratch_shapes=[pltpu.VMEM((2, bm * ROW_SUB, LANES), U32),
                        pltpu.VMEM((bm, d), F32), pltpu.VMEM((d, f), BF16), pltpu.VMEM((d, f), BF16),
                        pltpu.VMEM((f, d), BF16), pltpu.SemaphoreType.DMA((2,))])
    return pl.pallas_call(
        _fused_moe_kernel,
        grid_spec=grid_spec,
        out_shape=jax.ShapeDtypeStruct((m * ROW_SUB, LANES), U32),
        compiler_params=_cparams(("arbitrary",)),
        name="moe",
    )(*items, inv, xs, w_gate, w_up, w_down)


def _finish_kernel(*refs):
    yt_refs = refs[:TOP_K]
    shared, wt_ref, x_ref, g_ref, o_ref = refs[TOP_K:]
    tm, d = x_ref.shape
    half = d // 2
    for c in range(ROW_SUB):
        r_lo = jnp.zeros((tm, LANES), F32)
        r_hi = jnp.zeros((tm, LANES), F32)
        for k in range(TOP_K):
            lo, hi = _unpack_pair(yt_refs[k][pl.ds(c, tm, stride=ROW_SUB), :])
            wk = wt_ref[:, k:k + 1]
            r_lo = r_lo + wk * lo
            r_hi = r_hi + wk * hi
        for r, off in ((r_lo, c * LANES), (r_hi, half + c * LANES)):
            cols = slice(off, off + LANES)
            o_ref[:, cols] = x_ref[:, cols] + g_ref[:, cols] * (r + shared[:, cols].astype(F32))


def _finish(yt, shared, wt, x, gate):
    t, d = x.shape
    tm = TOKEN_TILE
    nt = t // tm
    row = lambda i: (i, 0)
    vec = lambda i: (0, 0)
    yt_specs = [pl.BlockSpec((tm * ROW_SUB, LANES), functools.partial(lambda k, i: (k * nt + i, 0), k))
                for k in range(TOP_K)]
    return pl.pallas_call(
        _finish_kernel,
        grid=(nt,),
        in_specs=yt_specs + [
            pl.BlockSpec((tm, d), row), pl.BlockSpec((tm, TOP_K), row),
            pl.BlockSpec((tm, d), row), pl.BlockSpec((1, d), vec)],
        out_specs=pl.BlockSpec((tm, d), row),
        out_shape=jax.ShapeDtypeStruct((t, d), F32),
        compiler_params=_cparams(("arbitrary",)),
        name="finish",
    )(*([yt] * TOP_K), shared, wt, x, gate)


def _rope_tables(s):
    half = ROPE_DIM // 2
    inv = ROPE_THETA ** (-jnp.arange(half, dtype=F32) * 2.0 / ROPE_DIM)
    ang = jnp.arange(s, dtype=I32).astype(F32)[:, None] * inv[None, :]
    cos, sin = jnp.cos(ang), jnp.sin(ang)
    pad = HEAD_DIM - ROPE_DIM
    rc = jnp.concatenate([cos, cos, jnp.ones((s, pad), F32)], axis=1)
    ra = jnp.concatenate([jnp.zeros((s, half), F32), sin, jnp.zeros((s, pad), F32)], axis=1)
    rb = jnp.concatenate([-sin, jnp.zeros((s, half + pad), F32)], axis=1)
    return rc, ra, rb


def _mixer(x, mod, rope, mix_norm, w_in, sgu_norm, sgu_w, sgu_b, q_norm, k_norm,
           out_norm_sgu, out_norm_att, w_out):
    d = x.shape[1]
    sh_m, sc_m, g_m = mod[:, 0:d], mod[:, d:2 * d], mod[:, 2 * d:3 * d]
    outs = _proj(x, mix_norm.reshape(1, d), sc_m, sh_m, w_in.astype(BF16),
                 q_norm.reshape(1, HEAD_DIM), k_norm.reshape(1, HEAD_DIM), rope)
    y_sgu = _sgu(outs[0], sgu_norm.reshape(1, -1), sgu_w, sgu_b.T, out_norm_sgu.reshape(1, -1))
    y_att = _attn(outs[1:], out_norm_att.reshape(1, -1))
    return _wout(y_sgu, y_att, w_out.astype(BF16), x, g_m)


def _ffn(x, mod, layer, ffn_norm, router_w, router_bias, exp_gate, exp_up, exp_down,
         shared_gate, shared_up, shared_down):
    t, d = x.shape
    sh_f, sc_f, g_f = mod[:, 3 * d:4 * d], mod[:, 4 * d:5 * d], mod[:, 5 * d:6 * d]
    h_bf, idx_t, wt_t, rank_t, cnt = _router(x, ffn_norm.reshape(1, d), sc_f, sh_f, router_w.T,
                                             router_bias.reshape(N_EXPERTS, 1))
    counts = cnt[:, 0].astype(I32)
    items, seg_start = _work_items(counts, t * TOP_K)
    dest_t = _dest(idx_t, rank_t, seg_start)
    nt = t // TOKEN_TILE
    dest_flat = dest_t.reshape(TOP_K, nt, TOKEN_TILE).transpose(1, 0, 2).reshape(-1)
    xs, shared = _dispatch(dest_flat, h_bf, shared_gate.astype(BF16), shared_up.astype(BF16),
                           shared_down.astype(BF16))
    inv = jnp.argsort(dest_t.reshape(-1)).astype(I32)
    yt = _fused_moe(items, inv, xs, layer, exp_gate, exp_up, exp_down)
    return _finish(yt, shared, wt_t.T, x, g_f)


def kernel(x, c, ada_w, ada_b, mix_norm, w_in, sgu_norm, sgu_w, sgu_b, q_norm, k_norm, out_norm_sgu,
           out_norm_att, w_out, ffn_norm, router_w, router_bias, exp_gate, exp_up, exp_down,
           shared_gate, shared_up, shared_down):
    b, s, d = x.shape
    assert b == 1 and s % ATT_TILE == 0 and d == 2 * ROW_SUB * LANES
    mods = _ada(c, ada_w, ada_b)
    rope = _rope_tables(s)
    xf = x.reshape(s, d)
    for l in range(ada_w.shape[0]):
        xf = _mixer(xf, mods[l], rope, mix_norm[l], w_in[l], sgu_norm[l], sgu_w[l], sgu_b[l], q_norm[l],
                    k_norm[l], out_norm_sgu[l], out_norm_att[l], w_out[l])
        xf = _ffn(xf, mods[l], l, ffn_norm[l], router_w[l], router_bias[l], exp_gate, exp_up,
                  exp_down, shared_gate[l], shared_up[l], shared_down[l])
    return xf.reshape(b, s, d)
```

```python
import functools

import jax
import jax.numpy as jnp
from jax import lax
from jax.experimental import pallas as pl
from jax.experimental.pallas import tpu as pltpu

F32 = jnp.float32
BF16 = jnp.bfloat16
U32 = jnp.uint32
I32 = jnp.int32

HEAD_DIM = 128
N_HEADS = 8
SGU_CHUNK = 128
DILATIONS = (1, 4, 16)
ATT_SPAN = 128
ATT_TILE = 2048
LOG2_E = 1.4426950408889634
ROPE_THETA = 500000.0
ROPE_DIM = HEAD_DIM // 4
N_EXPERTS = 64
TOP_K = 8
N_GROUPS = 8
TOPK_GROUPS = 4
ROUTED_SCALE = 2.5
NORM_EPS = 1e-6
N_MOD = 6

MOE_BLOCK = 512
TOKEN_TILE = 256
HI_MASK = 0xFFFF0000
LANES = 128
ROW_SUB = 8

VMEM_LIMIT = 56 * 1024 * 1024

NT_DIMS = (((1,), (1,)), ((), ()))


def _cparams(sem):
    return pltpu.CompilerParams(dimension_semantics=sem, vmem_limit_bytes=VMEM_LIMIT)


def _rms(x):
    return x * lax.rsqrt(jnp.mean(x * x, axis=-1, keepdims=True) + NORM_EPS)


def _gelu(x):
    return 0.5 * x * (1.0 + lax.erf(x * 0.7071067811865476))


def _pack_pair(lo_f32, hi_f32):
    lo = pltpu.bitcast(lo_f32.astype(BF16).astype(F32), U32)
    hi = pltpu.bitcast(hi_f32.astype(BF16).astype(F32), U32)
    return (lo >> 16) | (hi & jnp.uint32(HI_MASK))


def _unpack_pair(u):
    lo = pltpu.bitcast(u << 16, F32)
    hi = pltpu.bitcast(u & jnp.uint32(HI_MASK), F32)
    return lo, hi


def _store_rows(ref, base, n, lo_f32, hi_f32):
    for c in range(ROW_SUB):
        cols = slice(c * LANES, (c + 1) * LANES)
        ref[pl.ds(base + c, n, stride=ROW_SUB), :] = _pack_pair(lo_f32[:, cols], hi_f32[:, cols])


def _load_rows(ref, base, n):
    los, his = [], []
    for c in range(ROW_SUB):
        lo, hi = _unpack_pair(ref[pl.ds(base + c, n, stride=ROW_SUB), :])
        los.append(lo)
        his.append(hi)
    return jnp.concatenate(los, axis=1), jnp.concatenate(his, axis=1)


def _ada_kernel(c_ref, w_ref, b_ref, o_ref):
    d, tn = w_ref.shape

    def body(i, acc):
        r = pl.multiple_of(i * 8, 8)
        cc = c_ref[pl.ds(r, 8), :]
        return acc + (cc * jax.nn.sigmoid(cc)) * w_ref[pl.ds(r, 8), :]

    acc = lax.fori_loop(0, d // 8, body, jnp.zeros((8, tn), F32), unroll=8)
    o_ref[...] = jnp.sum(acc, axis=0, keepdims=True) + b_ref[...]


def _ada(c, ada_w, ada_b):
    nl, d, n = ada_w.shape
    tn = 1536
    return pl.pallas_call(
        _ada_kernel,
        grid=(nl, n // tn),
        in_specs=[pl.BlockSpec((d, 1), lambda l, j: (0, 0)),
                  pl.BlockSpec((None, d, tn), lambda l, j: (l, 0, j)),
                  pl.BlockSpec((None, 1, tn), lambda l, j: (l, 0, j))],
        out_specs=pl.BlockSpec((None, 1, tn), lambda l, j: (l, 0, j)),
        out_shape=jax.ShapeDtypeStruct((nl, 1, n), F32),
        compiler_params=_cparams(("arbitrary", "arbitrary")),
        name="ada",
    )(c.reshape(d, 1), ada_w, ada_b.reshape(nl, 1, n))


def _proj_kernel(x_ref, g_ref, sc_ref, sh_ref, w_ref, qg_ref, kg_ref, rc_ref, ra_ref, rb_ref,
                 uv_ref, q1_ref, q4_ref, q16_ref, k1_ref, k4_ref, k16_ref, v1_ref, v4_ref, v16_ref,
                 h_s, st_s):
    tm = x_ref.shape[0]
    pair = 2 * HEAD_DIM
    aw = N_HEADS * HEAD_DIM
    y = _rms(x_ref[...]) * g_ref[...]
    h_s[...] = (y * (1.0 + sc_ref[...]) + sh_ref[...]).astype(BF16)

    for p in range(uv_ref.shape[1] // pair):
        cols = slice(p * pair, (p + 1) * pair)
        uv_ref[:, cols] = jnp.dot(h_s[...], w_ref[:, cols], preferred_element_type=F32).astype(BF16)

    groups = ((q1_ref, q4_ref, q16_ref, qg_ref, LOG2_E * HEAD_DIM ** -0.5),
              (k1_ref, k4_ref, k16_ref, kg_ref, 1.0),
              (v1_ref, v4_ref, v16_ref, None, 1.0))
    for gi, (o1, o4, o16, gain_ref, scale) in enumerate(groups):
        for p in range(N_HEADS // 2):
            c0 = uv_ref.shape[1] + gi * aw + p * pair
            acc = jnp.dot(h_s[...], w_ref[:, c0:c0 + pair], preferred_element_type=F32)
            for hh in range(2):
                h = 2 * p + hh
                a = acc[:, hh * HEAD_DIM:(hh + 1) * HEAD_DIM]
                if gain_ref is not None:
                    a = _rms(a) * gain_ref[...]
                    a = (a * rc_ref[...] + pltpu.roll(a, ROPE_DIM // 2, 1) * ra_ref[...]
                         + pltpu.roll(a, HEAD_DIM - ROPE_DIM // 2, 1) * rb_ref[...]) * scale
                o1[:, h * HEAD_DIM:(h + 1) * HEAD_DIM] = a.astype(BF16)
                stage = st_s.at[gi * N_HEADS + h]
                stage[...] = a
                for d, o in ((4, o4), (16, o16)):
                    for r in range(d):
                        c = (h * d + r) * HEAD_DIM
                        o[:, c:c + HEAD_DIM] = stage[pl.ds(r, tm // d, stride=d), :].astype(BF16)


def _proj(x, gain, sc, sh, w_bf, q_gain, k_gain, rope):
    s, d = x.shape
    n = w_bf.shape[1]
    tm = 256
    aw = N_HEADS * HEAD_DIM
    row = lambda i: (i, 0)
    vec = lambda i: (0, 0)
    out_shape = [jax.ShapeDtypeStruct((s, 2 * aw), BF16)]
    out_specs = [pl.BlockSpec((tm, 2 * aw), row)]
    for _ in range(3):
        for dil in DILATIONS:
            out_shape.append(jax.ShapeDtypeStruct((s // dil, dil * aw), BF16))
            out_specs.append(pl.BlockSpec((tm // dil, dil * aw), row))
    return pl.pallas_call(
        _proj_kernel,
        grid=(s // tm,),
        in_specs=[pl.BlockSpec((tm, d), row), pl.BlockSpec((1, d), vec), pl.BlockSpec((1, d), vec),
                  pl.BlockSpec((1, d), vec),
                  pl.BlockSpec((d, n), vec, pipeline_mode=pl.Buffered(1)),
                  pl.BlockSpec((1, HEAD_DIM), vec), pl.BlockSpec((1, HEAD_DIM), vec),
                  pl.BlockSpec((tm, HEAD_DIM), row), pl.BlockSpec((tm, HEAD_DIM), row),
                  pl.BlockSpec((tm, HEAD_DIM), row)],
        out_specs=out_specs,
        out_shape=out_shape,
        scratch_shapes=[pltpu.VMEM((tm, d), BF16), pltpu.VMEM((3 * N_HEADS, tm, HEAD_DIM), F32)],
        compiler_params=_cparams(("arbitrary",)),
        name="proj",
    )(x, gain, sc, sh, w_bf, q_gain, k_gain, *rope)


def _sgu_kernel(uv_ref, gn_ref, w_ref, b_ref, go_ref, o_ref):
    tm = uv_ref.shape[0]
    width = N_HEADS * HEAD_DIM
    ii = lax.broadcasted_iota(I32, (SGU_CHUNK, SGU_CHUNK), 0)
    jj = lax.broadcasted_iota(I32, (SGU_CHUNK, SGU_CHUNK), 1)
    causal = jj <= ii
    for g in range(N_HEADS):
        cols = slice(g * HEAD_DIM, (g + 1) * HEAD_DIM)
        wg = jnp.where(causal, w_ref[g], 0.0).astype(BF16)
        u = _gelu(uv_ref[:, cols].astype(F32))
        v = _gelu(uv_ref[:, width + g * HEAD_DIM:width + (g + 1) * HEAD_DIM].astype(F32))
        vn = (_rms(v) * gn_ref[:, cols]).astype(BF16)
        bias = b_ref[:, g:g + 1]
        for n in range(tm // SGU_CHUNK):
            rows = slice(n * SGU_CHUNK, (n + 1) * SGU_CHUNK)
            mixed = jnp.dot(wg, vn[rows], preferred_element_type=F32) + bias
            y = u[rows] * mixed
            o_ref[rows, cols] = (_rms(y) * go_ref[:, cols]).astype(BF16)


def _sgu(uv, sgu_norm, sgu_w, sgu_b_t, out_norm):
    s = uv.shape[0]
    width = N_HEADS * HEAD_DIM
    tm = 512
    vec = lambda i: (0, 0)
    return pl.pallas_call(
        _sgu_kernel,
        grid=(s // tm,),
        in_specs=[pl.BlockSpec((tm, 2 * width), lambda i: (i, 0)),
                  pl.BlockSpec((1, width), vec),
                  pl.BlockSpec((N_HEADS, SGU_CHUNK, SGU_CHUNK), lambda i: (0, 0, 0)),
                  pl.BlockSpec((SGU_CHUNK, N_HEADS), vec),
                  pl.BlockSpec((1, width), vec)],
        out_specs=pl.BlockSpec((tm, width), lambda i: (i, 0)),
        out_shape=jax.ShapeDtypeStruct((s, width), BF16),
        compiler_params=_cparams(("arbitrary",)),
        name="sgu",
    )(uv, sgu_norm, sgu_w, sgu_b_t, out_norm)


def _attn_kernel(q1, q4, q16, k1, k4, k16, k1p, k4p, k16p, v1, v4, v16, v1p, v4p, v16p, go_ref,
                 o_ref, kb1, kb4, kb16, vb1, vb4, vb16, os1, os4, os16, ls1, ls4, ls16):
    first = pl.program_id(0) == 0
    blk = ATT_SPAN
    for buf, prev, cur in ((kb1, k1p, k1), (kb4, k4p, k4), (kb16, k16p, k16),
                           (vb1, v1p, v1), (vb4, v4p, v4), (vb16, v16p, v16)):
        buf[0:blk, :] = prev[...]
        buf[blk:, :] = cur[...]

    qi = lax.broadcasted_iota(I32, (blk, 2 * blk), 0)
    kj = lax.broadcasted_iota(I32, (blk, 2 * blk), 1)
    dist = qi + blk - kj
    band = (dist >= 0) & (dist <= ATT_SPAN)
    neg = jnp.where(band, 0.0, -jnp.inf)
    neg0 = jnp.where(first, jnp.where(band & (kj >= blk), 0.0, -jnp.inf), neg)

    for d, q_ref, kb, vb, o_s, l_s in ((1, q1, kb1, vb1, os1, ls1), (4, q4, kb4, vb4, os4, ls4),
                                       (16, q16, kb16, vb16, os16, ls16)):
        nb = ATT_TILE // (blk * d)
        for r in range(d):
            cols = slice(r * HEAD_DIM, (r + 1) * HEAD_DIM)
            for b in range(nb):
                q = q_ref[b * blk:(b + 1) * blk, cols]
                kk = kb[b * blk:(b + 2) * blk, cols]
                vv = vb[b * blk:(b + 2) * blk, cols]
                s = lax.dot_general(q, kk, NT_DIMS, preferred_element_type=F32)
                s = s + (neg0 if b == 0 else neg)
                m = jnp.max(s, axis=-1, keepdims=True)
                e = jnp.exp2(s - m)
                l = jnp.sum(e, axis=-1, keepdims=True)
                o = jnp.dot(e.astype(BF16), vv, preferred_element_type=F32) / l
                lse = jnp.broadcast_to(m + jnp.log2(l), (blk, HEAD_DIM))
                if d == 1:
                    o_s[b * blk:(b + 1) * blk, :] = o
                    l_s[b * blk:(b + 1) * blk, :] = lse
                else:
                    o_s[pl.ds(d * b * blk + r, blk, stride=d), :] = o
                    l_s[pl.ds(d * b * blk + r, blk, stride=d), :] = lse

    step = 256
    for c in range(ATT_TILE // step):
        rows = slice(c * step, (c + 1) * step)
        l1, l4, l16 = ls1[rows, :], ls4[rows, :], ls16[rows, :]
        mx = jnp.maximum(l1, jnp.maximum(l4, l16))
        w1, w4, w16 = jnp.exp2(l1 - mx), jnp.exp2(l4 - mx), jnp.exp2(l16 - mx)
        o = (w1 * os1[rows, :] + w4 * os4[rows, :] + w16 * os16[rows, :]) / (w1 + w4 + w16)
        o_ref[rows, :] = (_rms(o) * go_ref[...]).astype(BF16)


def _attn(qkv, out_norm):
    q1, q4, q16, k1, k4, k16, v1, v4, v16 = qkv
    s = q1.shape[0]
    nt = s // ATT_TILE
    blk = ATT_SPAN

    def cur(d):
        return pl.BlockSpec((ATT_TILE // d, d * HEAD_DIM), lambda i, h: (i, h))

    def prev(d):
        per = ATT_TILE // (d * blk)
        return pl.BlockSpec((blk, d * HEAD_DIM), lambda i, h: (jnp.maximum(i * per - 1, 0), h))

    in_specs = ([cur(d) for d in DILATIONS] + [cur(d) for d in DILATIONS] + [prev(d) for d in DILATIONS]
                + [cur(d) for d in DILATIONS] + [prev(d) for d in DILATIONS]
                + [pl.BlockSpec((1, HEAD_DIM), lambda i, h: (0, h))])
    kv_bufs = [pltpu.VMEM((blk + ATT_TILE // d, d * HEAD_DIM), BF16) for d in DILATIONS]
    acc_bufs = [pltpu.VMEM((ATT_TILE, HEAD_DIM), F32) for _ in DILATIONS]
    return pl.pallas_call(
        _attn_kernel,
        grid=(nt, N_HEADS),
        in_specs=in_specs,
        out_specs=pl.BlockSpec((ATT_TILE, HEAD_DIM), lambda i, h: (i, h)),
        out_shape=jax.ShapeDtypeStruct((s, N_HEADS * HEAD_DIM), BF16),
        scratch_shapes=kv_bufs + kv_bufs + acc_bufs + acc_bufs,
        compiler_params=_cparams(("arbitrary", "arbitrary")),
        name="attn",
    )(q1, q4, q16, k1, k4, k16, k1, k4, k16, v1, v4, v16, v1, v4, v16, out_norm)


def _wout_kernel(a1_ref, a2_ref, w_ref, x_ref, g_ref, o_ref):
    half = a1_ref.shape[1]
    acc = jnp.dot(a1_ref[...], w_ref[0:half, :], preferred_element_type=F32)
    acc = acc + jnp.dot(a2_ref[...], w_ref[half:2 * half, :], preferred_element_type=F32)
    o_ref[...] = x_ref[...] + g_ref[...] * acc


def _wout(y_sgu, y_att, w_bf, x, gate):
    s, d = x.shape
    half = y_sgu.shape[1]
    tm, tn = 1024, 1024
    return pl.pallas_call(
        _wout_kernel,
        grid=(s // tm, d // tn),
        in_specs=[pl.BlockSpec((tm, half), lambda i, j: (i, 0)),
                  pl.BlockSpec((tm, half), lambda i, j: (i, 0)),
                  pl.BlockSpec((2 * half, tn), lambda i, j: (0, j)),
                  pl.BlockSpec((tm, tn), lambda i, j: (i, j)),
                  pl.BlockSpec((1, tn), lambda i, j: (0, j))],
        out_specs=pl.BlockSpec((tm, tn), lambda i, j: (i, j)),
        out_shape=jax.ShapeDtypeStruct((s, d), F32),
        compiler_params=_cparams(("arbitrary", "arbitrary")),
        name="wout",
    )(y_sgu, y_att, w_bf, x, gate)


def _router_kernel(x_ref, g_ref, sc_ref, sh_ref, rwt_ref, rb_ref,
                   h_ref, idx_ref, wt_ref, rank_ref, cnt_ref, carry_s):
    tm = x_ref.shape[0]
    gsz = N_EXPERTS // N_GROUPS

    @pl.when(pl.program_id(0) == 0)
    def _():
        carry_s[...] = jnp.zeros_like(carry_s)

    h = (_rms(x_ref[...]) * g_ref[...]) * (1.0 + sc_ref[...]) + sh_ref[...]
    h_ref[...] = h.astype(BF16)

    hh = h.astype(BF16)
    hl = (h - hh.astype(F32)).astype(BF16)
    rw = rwt_ref[...]
    rh = rw.astype(BF16)
    rl = (rw - rh.astype(F32)).astype(BF16)
    dg = functools.partial(lax.dot_general, dimension_numbers=NT_DIMS, preferred_element_type=F32)
    logits = dg(rh, hh) + dg(rl, hh) + dg(rh, hl)
    scores = jax.nn.sigmoid(logits)
    sel = scores + rb_ref[...]

    io8 = lax.broadcasted_iota(I32, (gsz, tm), 0)
    grp, gscore = [], []
    for g in range(N_GROUPS):
        sg = sel[g * gsz:(g + 1) * gsz, :]
        m1 = jnp.max(sg, axis=0, keepdims=True)
        i1 = jnp.min(jnp.where(sg == m1, io8, gsz), axis=0, keepdims=True)
        m2 = jnp.max(jnp.where(io8 == i1, -jnp.inf, sg), axis=0, keepdims=True)
        grp.append(sg)
        gscore.append(m1 + m2)

    parts = []
    for g in range(N_GROUPS):
        beaten = jnp.zeros((1, tm), I32)
        for g2 in range(N_GROUPS):
            if g2 == g:
                continue
            b = (gscore[g2] >= gscore[g]) if g2 < g else (gscore[g2] > gscore[g])
            beaten = beaten + b.astype(I32)
        parts.append(jnp.where(beaten < TOPK_GROUPS, grp[g], -jnp.inf))
    masked = jnp.concatenate(parts, axis=0)

    io = lax.broadcasted_iota(I32, (N_EXPERTS, tm), 0)
    chosen = jnp.zeros((N_EXPERTS, tm), F32)
    idxs, wts = [], []
    for _ in range(TOP_K):
        m = jnp.max(masked, axis=0, keepdims=True)
        am = jnp.min(jnp.where(masked == m, io, N_EXPERTS), axis=0, keepdims=True)
        hit = io == am
        wts.append(jnp.sum(jnp.where(hit, scores, 0.0), axis=0, keepdims=True))
        idxs.append(am)
        chosen = chosen + hit.astype(F32)
        masked = jnp.where(hit, -jnp.inf, masked)
    wsum = wts[0]
    for k in range(1, TOP_K):
        wsum = wsum + wts[k]
    for k in range(TOP_K):
        idx_ref[k:k + 1, :] = idxs[k]
        wt_ref[k:k + 1, :] = wts[k] / wsum * ROUTED_SCALE

    t0 = lax.broadcasted_iota(I32, (tm, tm), 0)
    t1 = lax.broadcasted_iota(I32, (tm, tm), 1)
    before = (t0 < t1).astype(BF16)
    pre = jnp.dot(chosen.astype(BF16), before, preferred_element_type=F32) + carry_s[...]
    for k in range(TOP_K):
        rank_ref[k:k + 1, :] = jnp.sum(jnp.where(io == idxs[k], pre, 0.0), axis=0,
                                       keepdims=True).astype(I32)
    carry_s[...] = carry_s[...] + jnp.sum(chosen, axis=1, keepdims=True)
    cnt_ref[...] = jnp.broadcast_to(carry_s[...], cnt_ref.shape)


def _router(x, gain, sc, sh, rw_t, rbias):
    t, d = x.shape
    tm = 512
    vec = lambda i: (0, 0)
    tok = lambda i: (0, i)
    return pl.pallas_call(
        _router_kernel,
        grid=(t // tm,),
        in_specs=[pl.BlockSpec((tm, d), lambda i: (i, 0)), pl.BlockSpec((1, d), vec),
                  pl.BlockSpec((1, d), vec), pl.BlockSpec((1, d), vec),
                  pl.BlockSpec((N_EXPERTS, d), vec), pl.BlockSpec((N_EXPERTS, 1), vec)],
        out_specs=[pl.BlockSpec((tm, d), lambda i: (i, 0)), pl.BlockSpec((TOP_K, tm), tok),
                   pl.BlockSpec((TOP_K, tm), tok), pl.BlockSpec((TOP_K, tm), tok),
                   pl.BlockSpec((N_EXPERTS, HEAD_DIM), vec)],
        out_shape=[jax.ShapeDtypeStruct((t, d), BF16), jax.ShapeDtypeStruct((TOP_K, t), I32),
                   jax.ShapeDtypeStruct((TOP_K, t), F32), jax.ShapeDtypeStruct((TOP_K, t), I32),
                   jax.ShapeDtypeStruct((N_EXPERTS, HEAD_DIM), F32)],
        scratch_shapes=[pltpu.VMEM((N_EXPERTS, 1), F32)],
        compiler_params=_cparams(("arbitrary",)),
        name="router",
    )(x, gain, sc, sh, rw_t, rbias)


def _dest_kernel(idx_ref, rank_ref, start_ref, o_ref):
    tm = idx_ref.shape[1]
    io = lax.broadcasted_iota(I32, (N_EXPERTS, tm), 0)
    for k in range(TOP_K):
        seg = jnp.sum(jnp.where(io == idx_ref[k:k + 1, :], start_ref[...], 0), axis=0, keepdims=True)
        o_ref[k:k + 1, :] = (seg + rank_ref[k:k + 1, :]) * ROW_SUB


def _dest(idx_t, rank_t, seg_start):
    t = idx_t.shape[1]
    tm = 2048
    tok = lambda i: (0, i)
    return pl.pallas_call(
        _dest_kernel,
        grid=(t // tm,),
        in_specs=[pl.BlockSpec((TOP_K, tm), tok), pl.BlockSpec((TOP_K, tm), tok),
                  pl.BlockSpec((N_EXPERTS, 1), lambda i: (0, 0))],
        out_specs=pl.BlockSpec((TOP_K, tm), tok),
        out_shape=jax.ShapeDtypeStruct((TOP_K, t), I32),
        compiler_params=_cparams(("arbitrary",)),
        name="dest",
    )(idx_t, rank_t, seg_start.reshape(N_EXPERTS, 1))


def _dispatch_kernel(dest_s, h_ref, sg_ref, su_ref, sd_ref, xs_hbm, sh_ref, pk_s, sem_row):
    tm, d = h_ref.shape
    n = TOP_K * tm
    i = pl.program_id(0)
    last = pl.num_programs(0) - 1
    cur = i % 2
    base = i * n

    def row_copy(src_row, dst_row, buf):
        return pltpu.make_async_copy(pk_s.at[buf, pl.ds(src_row, ROW_SUB)],
                                     xs_hbm.at[pl.ds(dst_row, ROW_SUB)], sem_row.at[buf])

    def wait_all(buf):
        def wait(j, c):
            row_copy(0, 0, buf).wait()
            return c
        lax.fori_loop(0, n, wait, 0, unroll=8)

    @pl.when(i >= 2)
    def _():
        wait_all(cur)

    _store_rows(pk_s.at[cur], 0, tm, h_ref[:, 0:d // 2].astype(F32), h_ref[:, d // 2:d].astype(F32))

    vals = {}

    def p_gate():
        vals["gate"] = jnp.dot(h_ref[...], sg_ref[...], preferred_element_type=F32)

    def p_up():
        up = jnp.dot(h_ref[...], su_ref[...], preferred_element_type=F32)
        g = vals["gate"]
        vals["hid"] = (g * jax.nn.sigmoid(g) * up).astype(BF16)

    def p_shared(m):
        def run():
            cols = slice(m * 2 * LANES, (m + 1) * 2 * LANES)
            sh_ref[:, cols] = jnp.dot(vals["hid"], sd_ref[:, cols],
                                      preferred_element_type=F32).astype(sh_ref.dtype)
        return run

    pieces = [p_gate, p_up] + [p_shared(m) for m in range(d // (2 * LANES))]
    j = 0
    for idx, piece in enumerate(pieces):
        for _ in range(n // len(pieces) + (1 if idx < n % len(pieces) else 0)):
            row_copy((j % tm) * ROW_SUB, pl.multiple_of(dest_s[base + j], ROW_SUB), cur).start(priority=j % 2)
            j += 1
        piece()

    @pl.when(i == last)
    def _():
        wait_all(cur)

    @pl.when((i == last) & (i >= 1))
    def _():
        wait_all(1 - cur)

    @pl.when(i == last)
    def _():
        pk_s[...] = jnp.zeros_like(pk_s)
        tail = xs_hbm.shape[0] - 2 * tm * ROW_SUB
        for hb in range(2):
            pad = pltpu.make_async_copy(pk_s.at[hb], xs_hbm.at[pl.ds(tail + hb * tm * ROW_SUB, tm * ROW_SUB)],
                                        sem_row.at[hb])
            pad.start()
            pad.wait()


def _dispatch(dest_flat, h_bf, sg_bf, su_bf, sd_bf):
    t, d = h_bf.shape
    f = sg_bf.shape[1]
    tm = TOKEN_TILE
    assert MOE_BLOCK == 2 * tm
    row = lambda i, dest: (i, 0)
    vec = lambda i, dest: (0, 0)
    grid_spec = pltpu.PrefetchScalarGridSpec(
        num_scalar_prefetch=1,
        grid=(t // tm,),
        in_specs=[pl.BlockSpec((tm, d), row),
                  pl.BlockSpec((d, f), vec), pl.BlockSpec((d, f), vec), pl.BlockSpec((f, d), vec)],
        out_specs=[pl.BlockSpec(memory_space=pl.ANY), pl.BlockSpec((tm, d), row)],
        scratch_shapes=[pltpu.VMEM((2, tm * ROW_SUB, LANES), U32), pltpu.SemaphoreType.DMA((2,))])
    return pl.pallas_call(
        _dispatch_kernel,
        grid_spec=grid_spec,
        out_shape=[jax.ShapeDtypeStruct(((t * TOP_K + MOE_BLOCK) * ROW_SUB, LANES), U32),
                   jax.ShapeDtypeStruct((t, d), BF16)],
        compiler_params=_cparams(("arbitrary",)),
        name="dispatch",
    )(dest_flat, h_bf, sg_bf, su_bf, sd_bf)


def _expert_blocks(counts, m):
    bm = MOE_BLOCK
    n_items = m // bm + N_EXPERTS
    seg_end = jnp.cumsum(counts)
    seg_start = seg_end - counts
    nblk = (counts + bm - 1) // bm
    item_end = jnp.cumsum(nblk)
    w = jnp.arange(n_items, dtype=I32)
    active = (w < item_end[-1]).astype(I32)
    ex = jnp.minimum(jnp.sum((item_end[None, :] <= w[:, None]).astype(I32), axis=1), N_EXPERTS - 1)
    i_in = w - (item_end - nblk)[ex]
    row0 = jnp.where(active == 1, (seg_start[ex] + i_in * bm) * ROW_SUB, 0)
    newe = ((i_in == 0) & (active == 1)).astype(I32)
    cand = jnp.where(counts > 0, jnp.arange(N_EXPERTS, dtype=I32), N_EXPERTS)
    nxt_tab = jnp.concatenate([lax.cummin(cand[::-1])[::-1][1:], jnp.full((1,), N_EXPERTS, I32)])
    nxt = nxt_tab[ex]
    wsel = jnp.where(newe == 1, ex, jnp.where(nxt < N_EXPERTS, nxt, ex))
    return (row0, wsel, active, newe), seg_start


def _moe_kernel(row_s, wsel_s, act_s, newe_s, xs_hbm, wg_ref, wu_ref, wd_ref, ys_hbm,
                x_s, y_s, wg_s, wu_s, wd_s, sem_x, sem_y):
    w = pl.program_id(0)
    n = pl.num_programs(0)
    cur = w % 2
    rows = x_s.shape[1]
    bm = rows // ROW_SUB
    d = wg_s.shape[0]
    half = d // 2
    active = act_s[w] == 1
    prev_active = (w >= 1) & (act_s[jnp.maximum(w - 1, 0)] == 1)
    nxt = jnp.minimum(w + 1, n - 1)

    def x_copy(step, buf):
        r = pl.multiple_of(row_s[step], ROW_SUB)
        return pltpu.make_async_copy(xs_hbm.at[pl.ds(r, rows)], x_s.at[buf], sem_x.at[buf])

    def y_copy(step, buf):
        r = pl.multiple_of(row_s[step], ROW_SUB)
        return pltpu.make_async_copy(y_s.at[buf], ys_hbm.at[pl.ds(r, rows)], sem_y.at[buf])

    @pl.when((w == 0) & active)
    def _():
        x_copy(0, 0).start()

    @pl.when(newe_s[w] == 1)
    def _():
        wg_s[...] = wg_ref[...].astype(BF16)
        wu_s[...] = wu_ref[...].astype(BF16)
        wd_s[...] = wd_ref[...].astype(BF16)

    @pl.when(active)
    def _():
        x_copy(w, cur).wait()

        @pl.when((w + 1 < n) & (act_s[nxt] == 1))
        def _():
            x_copy(nxt, 1 - cur).start()

        xl, xh = _load_rows(x_s.at[cur], 0, bm)
        xl, xh = xl.astype(BF16), xh.astype(BF16)
        gate = (jnp.dot(xl, wg_s[0:half, :], preferred_element_type=F32)
                + jnp.dot(xh, wg_s[half:d, :], preferred_element_type=F32))
        up = (jnp.dot(xl, wu_s[0:half, :], preferred_element_type=F32)
              + jnp.dot(xh, wu_s[half:d, :], preferred_element_type=F32))
        hid = (gate * jax.nn.sigmoid(gate) * up).astype(BF16)
        y = jnp.dot(hid, wd_s[...], preferred_element_type=F32)
        _store_rows(y_s.at[cur], 0, bm, y[:, 0:half], y[:, half:d])

        @pl.when(w >= 1)
        def _():
            y_copy(w - 1, 1 - cur).wait()

        y_copy(w, cur).start()

        @pl.when(w == n - 1)
        def _():
            y_copy(w, cur).wait()

    @pl.when(jnp.logical_not(active) & prev_active)
    def _():
        y_copy(w - 1, 1 - cur).wait()


def _moe(items, xs, layer, w_gate, w_up, w_down):
    _, _, d, f = w_gate.shape
    bm = MOE_BLOCK
    wmap = lambda w, row, ws, act, ne: (layer, ws[w], 0, 0)
    grid_spec = pltpu.PrefetchScalarGridSpec(
        num_scalar_prefetch=4,
        grid=(items[0].shape[0],),
        in_specs=[pl.BlockSpec(memory_space=pl.ANY),
                  pl.BlockSpec((None, None, d, f), wmap),
                  pl.BlockSpec((None, None, d, f), wmap),
                  pl.BlockSpec((None, None, f, d), wmap)],
        out_specs=pl.BlockSpec(memory_space=pl.ANY),
        scratch_shapes=[pltpu.VMEM((2, bm * ROW_SUB, LANES), U32), pltpu.VMEM((2, bm * ROW_SUB, LANES), U32),
                        pltpu.VMEM((d, f), BF16), pltpu.VMEM((d, f), BF16), pltpu.VMEM((f, d), BF16),
                        pltpu.SemaphoreType.DMA((2,)), pltpu.SemaphoreType.DMA((2,))])
    return pl.pallas_call(
        _moe_kernel,
        grid_spec=grid_spec,
        out_shape=jax.ShapeDtypeStruct(xs.shape, U32),
        compiler_params=_cparams(("arbitrary",)),
        name="moe",
    )(*items, xs, w_gate, w_up, w_down)


def _combine_kernel(dest_s, ys_hbm, sh_ref, wt_ref, x_ref, g_ref, o_ref, y_s, sem_row):
    tm, d = x_ref.shape
    half = d // 2
    n = TOP_K * tm
    i = pl.program_id(0)
    last = pl.num_programs(0) - 1
    cur = i % 2
    nxt = 1 - cur
    base_next = jnp.minimum(i + 1, last) * n

    def row_copy(src_row, dst_row, buf):
        return pltpu.make_async_copy(ys_hbm.at[pl.ds(src_row, ROW_SUB)],
                                     y_s.at[buf, pl.ds(dst_row, ROW_SUB)], sem_row.at[buf])

    def wait_all(buf):
        def wait(j, c):
            row_copy(0, 0, buf).wait()
            return c
        lax.fori_loop(0, n, wait, 0, unroll=8)

    @pl.when(i == 0)
    def _():
        def start(j, c):
            row_copy(pl.multiple_of(dest_s[j], ROW_SUB), pl.multiple_of(j * ROW_SUB, ROW_SUB), 0).start()
            return c
        lax.fori_loop(0, n, start, 0, unroll=8)

    wait_all(cur)

    def p_routed(c):
        def run():
            r_lo = jnp.zeros((tm, LANES), F32)
            r_hi = jnp.zeros((tm, LANES), F32)
            for k in range(TOP_K):
                lo, hi = _unpack_pair(y_s[cur, pl.ds(k * tm * ROW_SUB + c, tm, stride=ROW_SUB), :])
                wk = wt_ref[:, k:k + 1]
                r_lo = r_lo + wk * lo
                r_hi = r_hi + wk * hi
            for r, off in ((r_lo, c * LANES), (r_hi, half + c * LANES)):
                cols = slice(off, off + LANES)
                o_ref[:, cols] = x_ref[:, cols] + g_ref[:, cols] * (r + sh_ref[:, cols].astype(F32))
        return run

    pieces = [p_routed(c) for c in range(ROW_SUB)]
    j = 0
    for idx, piece in enumerate(pieces):
        for _ in range(n // len(pieces) + (1 if idx < n % len(pieces) else 0)):
            row_copy(pl.multiple_of(dest_s[base_next + j], ROW_SUB), j * ROW_SUB, nxt).start(priority=j % 2)
            j += 1
        piece()

    @pl.when(i == last)
    def _():
        wait_all(nxt)


def _combine(dest_flat, ys, shared, wt, x, gate):
    t, d = x.shape
    tm = TOKEN_TILE
    n = TOP_K * tm
    row = lambda i, dest: (i, 0)
    vec = lambda i, dest: (0, 0)
    grid_spec = pltpu.PrefetchScalarGridSpec(
        num_scalar_prefetch=1,
        grid=(t // tm,),
        in_specs=[pl.BlockSpec(memory_space=pl.ANY),
                  pl.BlockSpec((tm, d), row), pl.BlockSpec((tm, TOP_K), row),
                  pl.BlockSpec((tm, d), row), pl.BlockSpec((1, d), vec)],
        out_specs=pl.BlockSpec((tm, d), row),
        scratch_shapes=[pltpu.VMEM((2, n * ROW_SUB, LANES), U32), pltpu.SemaphoreType.DMA((2,))])
    return pl.pallas_call(
        _combine_kernel,
        grid_spec=grid_spec,
        out_shape=jax.ShapeDtypeStruct((t, d), F32),
        compiler_params=_cparams(("arbitrary",)),
        name="combine",
    )(dest_flat, ys, shared, wt, x, gate)


def _rope_tables(s):
    half = ROPE_DIM // 2
    inv = ROPE_THETA ** (-jnp.arange(half, dtype=F32) * 2.0 / ROPE_DIM)
    ang = jnp.arange(s, dtype=I32).astype(F32)[:, None] * inv[None, :]
    cos, sin = jnp.cos(ang), jnp.sin(ang)
    pad = HEAD_DIM - ROPE_DIM
    rc = jnp.concatenate([cos, cos, jnp.ones((s, pad), F32)], axis=1)
    ra = jnp.concatenate([jnp.zeros((s, half), F32), sin, jnp.zeros((s, pad), F32)], axis=1)
    rb = jnp.concatenate([-sin, jnp.zeros((s, half + pad), F32)], axis=1)
    return rc, ra, rb


def _mixer(x, mod, rope, mix_norm, w_in, sgu_norm, sgu_w, sgu_b, q_norm, k_norm,
           out_norm_sgu, out_norm_att, w_out):
    d = x.shape[1]
    sh_m, sc_m, g_m = mod[:, 0:d], mod[:, d:2 * d], mod[:, 2 * d:3 * d]
    outs = _proj(x, mix_norm.reshape(1, d), sc_m, sh_m, w_in.astype(BF16),
                 q_norm.reshape(1, HEAD_DIM), k_norm.reshape(1, HEAD_DIM), rope)
    y_sgu = _sgu(outs[0], sgu_norm.reshape(1, -1), sgu_w, sgu_b.T, out_norm_sgu.reshape(1, -1))
    y_att = _attn(outs[1:], out_norm_att.reshape(1, -1))
    return _wout(y_sgu, y_att, w_out.astype(BF16), x, g_m)


def _ffn(x, mod, layer, ffn_norm, router_w, router_bias, exp_gate, exp_up, exp_down,
         shared_gate, shared_up, shared_down):
    t, d = x.shape
    sh_f, sc_f, g_f = mod[:, 3 * d:4 * d], mod[:, 4 * d:5 * d], mod[:, 5 * d:6 * d]
    h_bf, idx_t, wt_t, rank_t, cnt = _router(x, ffn_norm.reshape(1, d), sc_f, sh_f, router_w.T,
                                             router_bias.reshape(N_EXPERTS, 1))
    counts = cnt[:, 0].astype(I32)
    items, seg_start = _expert_blocks(counts, t * TOP_K)
    dest_t = _dest(idx_t, rank_t, seg_start)
    nt = t // TOKEN_TILE
    dest_flat = dest_t.reshape(TOP_K, nt, TOKEN_TILE).transpose(1, 0, 2).reshape(-1)
    xs, shared = _dispatch(dest_flat, h_bf, shared_gate.astype(BF16), shared_up.astype(BF16),
                           shared_down.astype(BF16))
    ys = _moe(items, xs, layer, exp_gate, exp_up, exp_down)
    return _combine(dest_flat, ys, shared, wt_t.T, x, g_f)


def kernel(x, c, ada_w, ada_b, mix_norm, w_in, sgu_norm, sgu_w, sgu_b, q_norm, k_norm, out_norm_sgu,
           out_norm_att, w_out, ffn_norm, router_w, router_bias, exp_gate, exp_up, exp_down,
           shared_gate, shared_up, shared_down):
    b, s, d = x.shape
    assert b == 1 and s % ATT_TILE == 0 and d == 2 * ROW_SUB * LANES
    mods = _ada(c, ada_w, ada_b)
    rope = _rope_tables(s)
    xf = x.reshape(s, d)
    for l in range(ada_w.shape[0]):
        xf = _mixer(xf, mods[l], rope, mix_norm[l], w_in[l], sgu_norm[l], sgu_w[l], sgu_b[l], q_norm[l],
                    k_norm[l], out_norm_sgu[l], out_norm_att[l], w_out[l])
        xf = _ffn(xf, mods[l], l, ffn_norm[l], router_w[l], router_bias[l], exp_gate, exp_up,
                  exp_down, shared_gate[l], shared_up[l], shared_down[l])
    return xf.reshape(b, s, d)
```

```python
import functools

import jax
import jax.numpy as jnp
from jax import lax
from jax.experimental import pallas as pl
from jax.experimental.pallas import tpu as pltpu

F32 = jnp.float32
BF16 = jnp.bfloat16
U32 = jnp.uint32
I32 = jnp.int32

HEAD_DIM = 128
N_HEADS = 8
SGU_CHUNK = 128
DILATIONS = (1, 4, 16)
ATT_SPAN = 128
ATT_TILE = 2048
LOG2_E = 1.4426950408889634
ROPE_THETA = 500000.0
ROPE_DIM = HEAD_DIM // 4
N_EXPERTS = 64
TOP_K = 8
N_GROUPS = 8
TOPK_GROUPS = 4
ROUTED_SCALE = 2.5
NORM_EPS = 1e-6
N_MOD = 6

MOE_BLOCK = 512
TOKEN_TILE = 256
HI_MASK = 0xFFFF0000
LANES = 128
ROW_SUB = 8

VMEM_LIMIT = 56 * 1024 * 1024

NT_DIMS = (((1,), (1,)), ((), ()))


def _cparams(sem):
    return pltpu.CompilerParams(dimension_semantics=sem, vmem_limit_bytes=VMEM_LIMIT)


def _rms(x):
    return x * lax.rsqrt(jnp.mean(x * x, axis=-1, keepdims=True) + NORM_EPS)


def _gelu(x):
    return 0.5 * x * (1.0 + lax.erf(x * 0.7071067811865476))


def _pack_pair(lo_f32, hi_f32):
    lo = pltpu.bitcast(lo_f32.astype(BF16).astype(F32), U32)
    hi = pltpu.bitcast(hi_f32.astype(BF16).astype(F32), U32)
    return (lo >> 16) | (hi & jnp.uint32(HI_MASK))


def _unpack_pair(u):
    lo = pltpu.bitcast(u << 16, F32)
    hi = pltpu.bitcast(u & jnp.uint32(HI_MASK), F32)
    return lo, hi


def _store_rows(ref, base, n, lo_f32, hi_f32):
    for c in range(ROW_SUB):
        cols = slice(c * LANES, (c + 1) * LANES)
        ref[pl.ds(base + c, n, stride=ROW_SUB), :] = _pack_pair(lo_f32[:, cols], hi_f32[:, cols])


def _load_rows(ref, base, n):
    los, his = [], []
    for c in range(ROW_SUB):
        lo, hi = _unpack_pair(ref[pl.ds(base + c, n, stride=ROW_SUB), :])
        los.append(lo)
        his.append(hi)
    return jnp.concatenate(los, axis=1), jnp.concatenate(his, axis=1)


def _ada_kernel(c_ref, w_ref, b_ref, o_ref):
    d, tn = w_ref.shape

    def body(i, acc):
        r = pl.multiple_of(i * 8, 8)
        cc = c_ref[pl.ds(r, 8), :]
        return acc + (cc * jax.nn.sigmoid(cc)) * w_ref[pl.ds(r, 8), :]

    acc = lax.fori_loop(0, d // 8, body, jnp.zeros((8, tn), F32), unroll=8)
    o_ref[...] = jnp.sum(acc, axis=0, keepdims=True) + b_ref[...]


def _ada(c, ada_w, ada_b):
    nl, d, n = ada_w.shape
    tn = 1536
    return pl.pallas_call(
        _ada_kernel,
        grid=(nl, n // tn),
        in_specs=[pl.BlockSpec((d, 1), lambda l, j: (0, 0)),
                  pl.BlockSpec((None, d, tn), lambda l, j: (l, 0, j)),
                  pl.BlockSpec((None, 1, tn), lambda l, j: (l, 0, j))],
        out_specs=pl.BlockSpec((None, 1, tn), lambda l, j: (l, 0, j)),
        out_shape=jax.ShapeDtypeStruct((nl, 1, n), F32),
        compiler_params=_cparams(("arbitrary", "arbitrary")),
        name="ada",
    )(c.reshape(d, 1), ada_w, ada_b.reshape(nl, 1, n))


def _proj_kernel(x_ref, g_ref, sc_ref, sh_ref, w_ref, qg_ref, kg_ref, rc_ref, ra_ref, rb_ref,
                 uv_ref, q1_ref, q4_ref, q16_ref, k1_ref, k4_ref, k16_ref, v1_ref, v4_ref, v16_ref,
                 h_s, st_s):
    tm = x_ref.shape[0]
    pair = 2 * HEAD_DIM
    aw = N_HEADS * HEAD_DIM
    y = _rms(x_ref[...]) * g_ref[...]
    h_s[...] = (y * (1.0 + sc_ref[...]) + sh_ref[...]).astype(BF16)

    for p in range(uv_ref.shape[1] // pair):
        cols = slice(p * pair, (p + 1) * pair)
        uv_ref[:, cols] = jnp.dot(h_s[...], w_ref[:, cols], preferred_element_type=F32).astype(BF16)

    groups = ((q1_ref, q4_ref, q16_ref, qg_ref, LOG2_E * HEAD_DIM ** -0.5),
              (k1_ref, k4_ref, k16_ref, kg_ref, 1.0),
              (v1_ref, v4_ref, v16_ref, None, 1.0))
    for gi, (o1, o4, o16, gain_ref, scale) in enumerate(groups):
        for p in range(N_HEADS // 2):
            c0 = uv_ref.shape[1] + gi * aw + p * pair
            acc = jnp.dot(h_s[...], w_ref[:, c0:c0 + pair], preferred_element_type=F32)
            for hh in range(2):
                h = 2 * p + hh
                a = acc[:, hh * HEAD_DIM:(hh + 1) * HEAD_DIM]
                if gain_ref is not None:
                    a = _rms(a) * gain_ref[...]
                    a = (a * rc_ref[...] + pltpu.roll(a, ROPE_DIM // 2, 1) * ra_ref[...]
                         + pltpu.roll(a, HEAD_DIM - ROPE_DIM // 2, 1) * rb_ref[...]) * scale
                o1[:, h * HEAD_DIM:(h + 1) * HEAD_DIM] = a.astype(BF16)
                stage = st_s.at[gi * N_HEADS + h]
                stage[...] = a
                for d, o in ((4, o4), (16, o16)):
                    for r in range(d):
                        c = (h * d + r) * HEAD_DIM
                        o[:, c:c + HEAD_DIM] = stage[pl.ds(r, tm // d, stride=d), :].astype(BF16)


def _proj(x, gain, sc, sh, w_bf, q_gain, k_gain, rope):
    s, d = x.shape
    n = w_bf.shape[1]
    tm = 256
    aw = N_HEADS * HEAD_DIM
    row = lambda i: (i, 0)
    vec = lambda i: (0, 0)
    out_shape = [jax.ShapeDtypeStruct((s, 2 * aw), BF16)]
    out_specs = [pl.BlockSpec((tm, 2 * aw), row)]
    for _ in range(3):
        for dil in DILATIONS:
            out_shape.append(jax.ShapeDtypeStruct((s // dil, dil * aw), BF16))
            out_specs.append(pl.BlockSpec((tm // dil, dil * aw), row))
    return pl.pallas_call(
        _proj_kernel,
        grid=(s // tm,),
        in_specs=[pl.BlockSpec((tm, d), row), pl.BlockSpec((1, d), vec), pl.BlockSpec((1, d), vec),
                  pl.BlockSpec((1, d), vec),
                  pl.BlockSpec((d, n), vec, pipeline_mode=pl.Buffered(1)),
                  pl.BlockSpec((1, HEAD_DIM), vec), pl.BlockSpec((1, HEAD_DIM), vec),
                  pl.BlockSpec((tm, HEAD_DIM), row), pl.BlockSpec((tm, HEAD_DIM), row),
                  pl.BlockSpec((tm, HEAD_DIM), row)],
        out_specs=out_specs,
        out_shape=out_shape,
        scratch_shapes=[pltpu.VMEM((tm, d), BF16), pltpu.VMEM((3 * N_HEADS, tm, HEAD_DIM), F32)],
        compiler_params=_cparams(("arbitrary",)),
        name="proj",
    )(x, gain, sc, sh, w_bf, q_gain, k_gain, *rope)


def _sgu_kernel(uv_ref, gn_ref, w_ref, b_ref, go_ref, o_ref):
    tm = uv_ref.shape[0]
    width = N_HEADS * HEAD_DIM
    ii = lax.broadcasted_iota(I32, (SGU_CHUNK, SGU_CHUNK), 0)
    jj = lax.broadcasted_iota(I32, (SGU_CHUNK, SGU_CHUNK), 1)
    causal = jj <= ii
    for g in range(N_HEADS):
        cols = slice(g * HEAD_DIM, (g + 1) * HEAD_DIM)
        wg = jnp.where(causal, w_ref[g], 0.0).astype(BF16)
        u = _gelu(uv_ref[:, cols].astype(F32))
        v = _gelu(uv_ref[:, width + g * HEAD_DIM:width + (g + 1) * HEAD_DIM].astype(F32))
        vn = (_rms(v) * gn_ref[:, cols]).astype(BF16)
        bias = b_ref[:, g:g + 1]
        for n in range(tm // SGU_CHUNK):
            rows = slice(n * SGU_CHUNK, (n + 1) * SGU_CHUNK)
            mixed = jnp.dot(wg, vn[rows], preferred_element_type=F32) + bias
            y = u[rows] * mixed
            o_ref[rows, cols] = (_rms(y) * go_ref[:, cols]).astype(BF16)


def _sgu(uv, sgu_norm, sgu_w, sgu_b_t, out_norm):
    s = uv.shape[0]
    width = N_HEADS * HEAD_DIM
    tm = 512
    vec = lambda i: (0, 0)
    return pl.pallas_call(
        _sgu_kernel,
        grid=(s // tm,),
        in_specs=[pl.BlockSpec((tm, 2 * width), lambda i: (i, 0)),
                  pl.BlockSpec((1, width), vec),
                  pl.BlockSpec((N_HEADS, SGU_CHUNK, SGU_CHUNK), lambda i: (0, 0, 0)),
                  pl.BlockSpec((SGU_CHUNK, N_HEADS), vec),
                  pl.BlockSpec((1, width), vec)],
        out_specs=pl.BlockSpec((tm, width), lambda i: (i, 0)),
        out_shape=jax.ShapeDtypeStruct((s, width), BF16),
        compiler_params=_cparams(("arbitrary",)),
        name="sgu",
    )(uv, sgu_norm, sgu_w, sgu_b_t, out_norm)


def _attn_kernel(q1, q4, q16, k1, k4, k16, k1p, k4p, k16p, v1, v4, v16, v1p, v4p, v16p, go_ref,
                 o_ref, kb1, kb4, kb16, vb1, vb4, vb16, os1, os4, os16, ls1, ls4, ls16):
    first = pl.program_id(0) == 0
    blk = ATT_SPAN
    for buf, prev, cur in ((kb1, k1p, k1), (kb4, k4p, k4), (kb16, k16p, k16),
                           (vb1, v1p, v1), (vb4, v4p, v4), (vb16, v16p, v16)):
        buf[0:blk, :] = prev[...]
        buf[blk:, :] = cur[...]

    qi = lax.broadcasted_iota(I32, (blk, 2 * blk), 0)
    kj = lax.broadcasted_iota(I32, (blk, 2 * blk), 1)
    dist = qi + blk - kj
    band = (dist >= 0) & (dist <= ATT_SPAN)
    neg = jnp.where(band, 0.0, -jnp.inf)
    neg0 = jnp.where(first, jnp.where(band & (kj >= blk), 0.0, -jnp.inf), neg)

    for d, q_ref, kb, vb, o_s, l_s in ((1, q1, kb1, vb1, os1, ls1), (4, q4, kb4, vb4, os4, ls4),
                                       (16, q16, kb16, vb16, os16, ls16)):
        nb = ATT_TILE // (blk * d)
        for r in range(d):
            cols = slice(r * HEAD_DIM, (r + 1) * HEAD_DIM)
            for b in range(nb):
                q = q_ref[b * blk:(b + 1) * blk, cols]
                kk = kb[b * blk:(b + 2) * blk, cols]
                vv = vb[b * blk:(b + 2) * blk, cols]
                s = lax.dot_general(q, kk, NT_DIMS, preferred_element_type=F32)
                s = s + (neg0 if b == 0 else neg)
                m = jnp.max(s, axis=-1, keepdims=True)
                e = jnp.exp2(s - m)
                l = jnp.sum(e, axis=-1, keepdims=True)
                o = jnp.dot(e.astype(BF16), vv, preferred_element_type=F32) / l
                lse = jnp.broadcast_to(m + jnp.log2(l), (blk, HEAD_DIM))
                if d == 1:
                    o_s[b * blk:(b + 1) * blk, :] = o
                    l_s[b * blk:(b + 1) * blk, :] = lse
                else:
                    o_s[pl.ds(d * b * blk + r, blk, stride=d), :] = o
                    l_s[pl.ds(d * b * blk + r, blk, stride=d), :] = lse

    step = 256
    for c in range(ATT_TILE // step):
        rows = slice(c * step, (c + 1) * step)
        l1, l4, l16 = ls1[rows, :], ls4[rows, :], ls16[rows, :]
        mx = jnp.maximum(l1, jnp.maximum(l4, l16))
        w1, w4, w16 = jnp.exp2(l1 - mx), jnp.exp2(l4 - mx), jnp.exp2(l16 - mx)
        o = (w1 * os1[rows, :] + w4 * os4[rows, :] + w16 * os16[rows, :]) / (w1 + w4 + w16)
        o_ref[rows, :] = (_rms(o) * go_ref[...]).astype(BF16)


def _attn(qkv, out_norm):
    q1, q4, q16, k1, k4, k16, v1, v4, v16 = qkv
    s = q1.shape[0]
    nt = s // ATT_TILE
    blk = ATT_SPAN

    def cur(d):
        return pl.BlockSpec((ATT_TILE // d, d * HEAD_DIM), lambda i, h: (i, h))

    def prev(d):
        per = ATT_TILE // (d * blk)
        return pl.BlockSpec((blk, d * HEAD_DIM), lambda i, h: (jnp.maximum(i * per - 1, 0), h))

    in_specs = ([cur(d) for d in DILATIONS] + [cur(d) for d in DILATIONS] + [prev(d) for d in DILATIONS]
                + [cur(d) for d in DILATIONS] + [prev(d) for d in DILATIONS]
                + [pl.BlockSpec((1, HEAD_DIM), lambda i, h: (0, h))])
    kv_bufs = [pltpu.VMEM((blk + ATT_TILE // d, d * HEAD_DIM), BF16) for d in DILATIONS]
    acc_bufs = [pltpu.VMEM((ATT_TILE, HEAD_DIM), F32) for _ in DILATIONS]
    return pl.pallas_call(
        _attn_kernel,
        grid=(nt, N_HEADS),
        in_specs=in_specs,
        out_specs=pl.BlockSpec((ATT_TILE, HEAD_DIM), lambda i, h: (i, h)),
        out_shape=jax.ShapeDtypeStruct((s, N_HEADS * HEAD_DIM), BF16),
        scratch_shapes=kv_bufs + kv_bufs + acc_bufs + acc_bufs,
        compiler_params=_cparams(("arbitrary", "arbitrary")),
        name="attn",
    )(q1, q4, q16, k1, k4, k16, k1, k4, k16, v1, v4, v16, v1, v4, v16, out_norm)


def _wout_kernel(a1_ref, a2_ref, w_ref, x_ref, g_ref, o_ref):
    half = a1_ref.shape[1]
    acc = jnp.dot(a1_ref[...], w_ref[0:half, :], preferred_element_type=F32)
    acc = acc + jnp.dot(a2_ref[...], w_ref[half:2 * half, :], preferred_element_type=F32)
    o_ref[...] = x_ref[...] + g_ref[...] * acc


def _wout(y_sgu, y_att, w_bf, x, gate):
    s, d = x.shape
    half = y_sgu.shape[1]
    tm, tn = 1024, 1024
    return pl.pallas_call(
        _wout_kernel,
        grid=(s // tm, d // tn),
        in_specs=[pl.BlockSpec((tm, half), lambda i, j: (i, 0)),
                  pl.BlockSpec((tm, half), lambda i, j: (i, 0)),
                  pl.BlockSpec((2 * half, tn), lambda i, j: (0, j)),
                  pl.BlockSpec((tm, tn), lambda i, j: (i, j)),
                  pl.BlockSpec((1, tn), lambda i, j: (0, j))],
        out_specs=pl.BlockSpec((tm, tn), lambda i, j: (i, j)),
        out_shape=jax.ShapeDtypeStruct((s, d), F32),
        compiler_params=_cparams(("arbitrary", "arbitrary")),
        name="wout",
    )(y_sgu, y_att, w_bf, x, gate)


def _router_kernel(x_ref, g_ref, sc_ref, sh_ref, rwt_ref, rb_ref,
                   h_ref, idx_ref, wt_ref, rank_ref, cnt_ref, carry_s):
    tm = x_ref.shape[0]
    gsz = N_EXPERTS // N_GROUPS

    @pl.when(pl.program_id(0) == 0)
    def _():
        carry_s[...] = jnp.zeros_like(carry_s)

    h = (_rms(x_ref[...]) * g_ref[...]) * (1.0 + sc_ref[...]) + sh_ref[...]
    h_ref[...] = h.astype(BF16)

    hh = h.astype(BF16)
    hl = (h - hh.astype(F32)).astype(BF16)
    rw = rwt_ref[...]
    rh = rw.astype(BF16)
    rl = (rw - rh.astype(F32)).astype(BF16)
    dg = functools.partial(lax.dot_general, dimension_numbers=NT_DIMS, preferred_element_type=F32)
    logits = dg(rh, hh) + dg(rl, hh) + dg(rh, hl)
    scores = jax.nn.sigmoid(logits)
    sel = scores + rb_ref[...]

    io8 = lax.broadcasted_iota(I32, (gsz, tm), 0)
    grp, gscore = [], []
    for g in range(N_GROUPS):
        sg = sel[g * gsz:(g + 1) * gsz, :]
        m1 = jnp.max(sg, axis=0, keepdims=True)
        i1 = jnp.min(jnp.where(sg == m1, io8, gsz), axis=0, keepdims=True)
        m2 = jnp.max(jnp.where(io8 == i1, -jnp.inf, sg), axis=0, keepdims=True)
        grp.append(sg)
        gscore.append(m1 + m2)

    parts = []
    for g in range(N_GROUPS):
        beaten = jnp.zeros((1, tm), I32)
        for g2 in range(N_GROUPS):
            if g2 == g:
                continue
            b = (gscore[g2] >= gscore[g]) if g2 < g else (gscore[g2] > gscore[g])
            beaten = beaten + b.astype(I32)
        parts.append(jnp.where(beaten < TOPK_GROUPS, grp[g], -jnp.inf))
    masked = jnp.concatenate(parts, axis=0)

    io = lax.broadcasted_iota(I32, (N_EXPERTS, tm), 0)
    chosen = jnp.zeros((N_EXPERTS, tm), F32)
    idxs, wts = [], []
    for _ in range(TOP_K):
        m = jnp.max(masked, axis=0, keepdims=True)
        am = jnp.min(jnp.where(masked == m, io, N_EXPERTS), axis=0, keepdims=True)
        hit = io == am
        wts.append(jnp.sum(jnp.where(hit, scores, 0.0), axis=0, keepdims=True))
        idxs.append(am)
        chosen = chosen + hit.astype(F32)
        masked = jnp.where(hit, -jnp.inf, masked)
    wsum = wts[0]
    for k in range(1, TOP_K):
        wsum = wsum + wts[k]
    for k in range(TOP_K):
        idx_ref[k:k + 1, :] = idxs[k]
        wt_ref[k:k + 1, :] = wts[k] / wsum * ROUTED_SCALE

    t0 = lax.broadcasted_iota(I32, (tm, tm), 0)
    t1 = lax.broadcasted_iota(I32, (tm, tm), 1)
    before = (t0 < t1).astype(BF16)
    pre = jnp.dot(chosen.astype(BF16), before, preferred_element_type=F32) + carry_s[...]
    for k in range(TOP_K):
        rank_ref[k:k + 1, :] = jnp.sum(jnp.where(io == idxs[k], pre, 0.0), axis=0,
                                       keepdims=True).astype(I32)
    carry_s[...] = carry_s[...] + jnp.sum(chosen, axis=1, keepdims=True)
    cnt_ref[...] = jnp.broadcast_to(carry_s[...], cnt_ref.shape)


def _router(x, gain, sc, sh, rw_t, rbias):
    t, d = x.shape
    tm = 512
    vec = lambda i: (0, 0)
    tok = lambda i: (0, i)
    return pl.pallas_call(
        _router_kernel,
        grid=(t // tm,),
        in_specs=[pl.BlockSpec((tm, d), lambda i: (i, 0)), pl.BlockSpec((1, d), vec),
                  pl.BlockSpec((1, d), vec), pl.BlockSpec((1, d), vec),
                  pl.BlockSpec((N_EXPERTS, d), vec), pl.BlockSpec((N_EXPERTS, 1), vec)],
        out_specs=[pl.BlockSpec((tm, d), lambda i: (i, 0)), pl.BlockSpec((TOP_K, tm), tok),
                   pl.BlockSpec((TOP_K, tm), tok), pl.BlockSpec((TOP_K, tm), tok),
                   pl.BlockSpec((N_EXPERTS, HEAD_DIM), vec)],
        out_shape=[jax.ShapeDtypeStruct((t, d), BF16), jax.ShapeDtypeStruct((TOP_K, t), I32),
                   jax.ShapeDtypeStruct((TOP_K, t), F32), jax.ShapeDtypeStruct((TOP_K, t), I32),
                   jax.ShapeDtypeStruct((N_EXPERTS, HEAD_DIM), F32)],
        scratch_shapes=[pltpu.VMEM((N_EXPERTS, 1), F32)],
        compiler_params=_cparams(("arbitrary",)),
        name="router",
    )(x, gain, sc, sh, rw_t, rbias)


def _dest_kernel(idx_ref, rank_ref, start_ref, o_ref):
    tm = idx_ref.shape[1]
    io = lax.broadcasted_iota(I32, (N_EXPERTS, tm), 0)
    for k in range(TOP_K):
        seg = jnp.sum(jnp.where(io == idx_ref[k:k + 1, :], start_ref[...], 0), axis=0, keepdims=True)
        o_ref[k:k + 1, :] = (seg + rank_ref[k:k + 1, :]) * ROW_SUB


def _dest(idx_t, rank_t, seg_start):
    t = idx_t.shape[1]
    tm = 2048
    tok = lambda i: (0, i)
    return pl.pallas_call(
        _dest_kernel,
        grid=(t // tm,),
        in_specs=[pl.BlockSpec((TOP_K, tm), tok), pl.BlockSpec((TOP_K, tm), tok),
                  pl.BlockSpec((N_EXPERTS, 1), lambda i: (0, 0))],
        out_specs=pl.BlockSpec((TOP_K, tm), tok),
        out_shape=jax.ShapeDtypeStruct((TOP_K, t), I32),
        compiler_params=_cparams(("arbitrary",)),
        name="dest",
    )(idx_t, rank_t, seg_start.reshape(N_EXPERTS, 1))


def _dispatch_kernel(dest_s, h_ref, sg_ref, su_ref, sd_ref, xs_hbm, sh_ref, pk_s, sem_row):
    tm, d = h_ref.shape
    n = TOP_K * tm
    i = pl.program_id(0)
    last = pl.num_programs(0) - 1
    cur = i % 2
    base = i * n

    def row_copy(src_row, dst_row, buf):
        return pltpu.make_async_copy(pk_s.at[buf, pl.ds(src_row, ROW_SUB)],
                                     xs_hbm.at[pl.ds(dst_row, ROW_SUB)], sem_row.at[buf])

    def wait_all(buf):
        def wait(j, c):
            row_copy(0, 0, buf).wait()
            return c
        lax.fori_loop(0, n, wait, 0, unroll=8)

    @pl.when(i >= 2)
    def _():
        wait_all(cur)

    _store_rows(pk_s.at[cur], 0, tm, h_ref[:, 0:d // 2].astype(F32), h_ref[:, d // 2:d].astype(F32))

    vals = {}

    def p_gate():
        vals["gate"] = jnp.dot(h_ref[...], sg_ref[...], preferred_element_type=F32)

    def p_up():
        up = jnp.dot(h_ref[...], su_ref[...], preferred_element_type=F32)
        g = vals["gate"]
        vals["hid"] = (g * jax.nn.sigmoid(g) * up).astype(BF16)

    def p_shared(m):
        def run():
            cols = slice(m * 2 * LANES, (m + 1) * 2 * LANES)
            sh_ref[:, cols] = jnp.dot(vals["hid"], sd_ref[:, cols],
                                      preferred_element_type=F32).astype(sh_ref.dtype)
        return run

    pieces = [p_gate, p_up] + [p_shared(m) for m in range(d // (2 * LANES))]
    j = 0
    for idx, piece in enumerate(pieces):
        for _ in range(n // len(pieces) + (1 if idx < n % len(pieces) else 0)):
            row_copy((j % tm) * ROW_SUB, pl.multiple_of(dest_s[base + j], ROW_SUB), cur).start(priority=j % 2)
            j += 1
        piece()

    @pl.when(i == last)
    def _():
        wait_all(cur)

    @pl.when((i == last) & (i >= 1))
    def _():
        wait_all(1 - cur)

    @pl.when(i == last)
    def _():
        pk_s[...] = jnp.zeros_like(pk_s)
        tail = xs_hbm.shape[0] - 2 * tm * ROW_SUB
        for hb in range(2):
            pad = pltpu.make_async_copy(pk_s.at[hb], xs_hbm.at[pl.ds(tail + hb * tm * ROW_SUB, tm * ROW_SUB)],
                                        sem_row.at[hb])
            pad.start()
            pad.wait()


def _dispatch(dest_flat, h_bf, sg_bf, su_bf, sd_bf):
    t, d = h_bf.shape
    f = sg_bf.shape[1]
    tm = TOKEN_TILE
    assert MOE_BLOCK == 2 * tm
    row = lambda i, dest: (i, 0)
    vec = lambda i, dest: (0, 0)
    grid_spec = pltpu.PrefetchScalarGridSpec(
        num_scalar_prefetch=1,
        grid=(t // tm,),
        in_specs=[pl.BlockSpec((tm, d), row),
                  pl.BlockSpec((d, f), vec), pl.BlockSpec((d, f), vec), pl.BlockSpec((f, d), vec)],
        out_specs=[pl.BlockSpec(memory_space=pl.ANY), pl.BlockSpec((tm, d), row)],
        scratch_shapes=[pltpu.VMEM((2, tm * ROW_SUB, LANES), U32), pltpu.SemaphoreType.DMA((2,))])
    return pl.pallas_call(
        _dispatch_kernel,
        grid_spec=grid_spec,
        out_shape=[jax.ShapeDtypeStruct(((t * TOP_K + MOE_BLOCK) * ROW_SUB, LANES), U32),
                   jax.ShapeDtypeStruct((t, d), BF16)],
        compiler_params=_cparams(("arbitrary",)),
        name="dispatch",
    )(dest_flat, h_bf, sg_bf, su_bf, sd_bf)


def _expert_blocks(counts, m):
    bm = MOE_BLOCK
    n_items = m // bm + N_EXPERTS
    seg_end = jnp.cumsum(counts)
    seg_start = seg_end - counts
    nblk = (counts + bm - 1) // bm
    item_end = jnp.cumsum(nblk)
    w = jnp.arange(n_items, dtype=I32)
    active = (w < item_end[-1]).astype(I32)
    ex = jnp.minimum(jnp.sum((item_end[None, :] <= w[:, None]).astype(I32), axis=1), N_EXPERTS - 1)
    onehot = ex[:, None] == jnp.arange(N_EXPERTS, dtype=I32)[None, :]

    def pick(table):
        return jnp.sum(jnp.where(onehot, table[None, :], 0), axis=1)

    i_in = w - pick(item_end - nblk)
    row0 = jnp.where(active == 1, (pick(seg_start) + i_in * bm) * ROW_SUB, 0)
    newe = ((i_in == 0) & (active == 1)).astype(I32)
    cand = jnp.where(counts > 0, jnp.arange(N_EXPERTS, dtype=I32), N_EXPERTS)
    nxt_tab = jnp.concatenate([lax.cummin(cand[::-1])[::-1][1:], jnp.full((1,), N_EXPERTS, I32)])
    nxt = pick(nxt_tab)
    wsel = jnp.where(newe == 1, ex, jnp.where(nxt < N_EXPERTS, nxt, ex))
    return (row0, wsel, active, newe), seg_start


def _moe_kernel(row_s, wsel_s, act_s, newe_s, xs_hbm, wg_ref, wu_ref, wd_ref, ys_hbm,
                x_s, y_s, wg_s, wu_s, wd_s, sem_x, sem_y):
    w = pl.program_id(0)
    n = pl.num_programs(0)
    cur = w % 2
    rows = x_s.shape[1]
    bm = rows // ROW_SUB
    d = wg_s.shape[0]
    half = d // 2
    active = act_s[w] == 1
    prev_active = (w >= 1) & (act_s[jnp.maximum(w - 1, 0)] == 1)
    nxt = jnp.minimum(w + 1, n - 1)

    def x_copy(step, buf):
        r = pl.multiple_of(row_s[step], ROW_SUB)
        return pltpu.make_async_copy(xs_hbm.at[pl.ds(r, rows)], x_s.at[buf], sem_x.at[buf])

    def y_copy(step, buf):
        r = pl.multiple_of(row_s[step], ROW_SUB)
        return pltpu.make_async_copy(y_s.at[buf], ys_hbm.at[pl.ds(r, rows)], sem_y.at[buf])

    @pl.when((w == 0) & active)
    def _():
        x_copy(0, 0).start()

    @pl.when(newe_s[w] == 1)
    def _():
        wg_s[...] = wg_ref[...].astype(BF16)
        wu_s[...] = wu_ref[...].astype(BF16)
        wd_s[...] = wd_ref[...].astype(BF16)

    @pl.when(active)
    def _():
        x_copy(w, cur).wait()

        @pl.when((w + 1 < n) & (act_s[nxt] == 1))
        def _():
            x_copy(nxt, 1 - cur).start()

        xl, xh = _load_rows(x_s.at[cur], 0, bm)
        xl, xh = xl.astype(BF16), xh.astype(BF16)
        gate = (jnp.dot(xl, wg_s[0:half, :], preferred_element_type=F32)
                + jnp.dot(xh, wg_s[half:d, :], preferred_element_type=F32))
        up = (jnp.dot(xl, wu_s[0:half, :], preferred_element_type=F32)
              + jnp.dot(xh, wu_s[half:d, :], preferred_element_type=F32))
        hid = (gate * jax.nn.sigmoid(gate) * up).astype(BF16)
        y = jnp.dot(hid, wd_s[...], preferred_element_type=F32)
        _store_rows(y_s.at[cur], 0, bm, y[:, 0:half], y[:, half:d])

        @pl.when(w >= 1)
        def _():
            y_copy(w - 1, 1 - cur).wait()

        y_copy(w, cur).start()

        @pl.when(w == n - 1)
        def _():
            y_copy(w, cur).wait()

    @pl.when(jnp.logical_not(active) & prev_active)
    def _():
        y_copy(w - 1, 1 - cur).wait()


def _moe(items, xs, layer, w_gate, w_up, w_down):
    _, _, d, f = w_gate.shape
    bm = MOE_BLOCK
    wmap = lambda w, row, ws, act, ne: (layer, ws[w], 0, 0)
    grid_spec = pltpu.PrefetchScalarGridSpec(
        num_scalar_prefetch=4,
        grid=(items[0].shape[0],),
        in_specs=[pl.BlockSpec(memory_space=pl.ANY),
                  pl.BlockSpec((None, None, d, f), wmap),
                  pl.BlockSpec((None, None, d, f), wmap),
                  pl.BlockSpec((None, None, f, d), wmap)],
        out_specs=pl.BlockSpec(memory_space=pl.ANY),
        scratch_shapes=[pltpu.VMEM((2, bm * ROW_SUB, LANES), U32), pltpu.VMEM((2, bm * ROW_SUB, LANES), U32),
                        pltpu.VMEM((d, f), BF16), pltpu.VMEM((d, f), BF16), pltpu.VMEM((f, d), BF16),
                        pltpu.SemaphoreType.DMA((2,)), pltpu.SemaphoreType.DMA((2,))])
    return pl.pallas_call(
        _moe_kernel,
        grid_spec=grid_spec,
        out_shape=jax.ShapeDtypeStruct(xs.shape, U32),
        compiler_params=_cparams(("arbitrary",)),
        name="moe",
    )(*items, xs, w_gate, w_up, w_down)


def _combine_kernel(dest_s, ys_hbm, sh_ref, wt_ref, x_ref, g_ref, o_ref, y_s, sem_row):
    tm, d = x_ref.shape
    half = d // 2
    n = TOP_K * tm
    i = pl.program_id(0)
    last = pl.num_programs(0) - 1
    cur = i % 2
    nxt = 1 - cur
    base_next = jnp.minimum(i + 1, last) * n

    def row_copy(src_row, dst_row, buf):
        return pltpu.make_async_copy(ys_hbm.at[pl.ds(src_row, ROW_SUB)],
                                     y_s.at[buf, pl.ds(dst_row, ROW_SUB)], sem_row.at[buf])

    def wait_all(buf):
        def wait(j, c):
            row_copy(0, 0, buf).wait()
            return c
        lax.fori_loop(0, n, wait, 0, unroll=8)

    @pl.when(i == 0)
    def _():
        def start(j, c):
            row_copy(pl.multiple_of(dest_s[j], ROW_SUB), pl.multiple_of(j * ROW_SUB, ROW_SUB), 0).start()
            return c
        lax.fori_loop(0, n, start, 0, unroll=8)

    wait_all(cur)

    def p_routed(c):
        def run():
            r_lo = jnp.zeros((tm, LANES), F32)
            r_hi = jnp.zeros((tm, LANES), F32)
            for k in range(TOP_K):
                lo, hi = _unpack_pair(y_s[cur, pl.ds(k * tm * ROW_SUB + c, tm, stride=ROW_SUB), :])
                wk = wt_ref[:, k:k + 1]
                r_lo = r_lo + wk * lo
                r_hi = r_hi + wk * hi
            for r, off in ((r_lo, c * LANES), (r_hi, half + c * LANES)):
                cols = slice(off, off + LANES)
                o_ref[:, cols] = x_ref[:, cols] + g_ref[:, cols] * (r + sh_ref[:, cols].astype(F32))
        return run

    pieces = [p_routed(c) for c in range(ROW_SUB)]
    j = 0
    for idx, piece in enumerate(pieces):
        for _ in range(n // len(pieces) + (1 if idx < n % len(pieces) else 0)):
            row_copy(pl.multiple_of(dest_s[base_next + j], ROW_SUB), j * ROW_SUB, nxt).start(priority=j % 2)
            j += 1
        piece()

    @pl.when(i == last)
    def _():
        wait_all(nxt)


def _combine(dest_flat, ys, shared, wt, x, gate):
    t, d = x.shape
    tm = TOKEN_TILE
    n = TOP_K * tm
    row = lambda i, dest: (i, 0)
    vec = lambda i, dest: (0, 0)
    grid_spec = pltpu.PrefetchScalarGridSpec(
        num_scalar_prefetch=1,
        grid=(t // tm,),
        in_specs=[pl.BlockSpec(memory_space=pl.ANY),
                  pl.BlockSpec((tm, d), row), pl.BlockSpec((tm, TOP_K), row),
                  pl.BlockSpec((tm, d), row), pl.BlockSpec((1, d), vec)],
        out_specs=pl.BlockSpec((tm, d), row),
        scratch_shapes=[pltpu.VMEM((2, n * ROW_SUB, LANES), U32), pltpu.SemaphoreType.DMA((2,))])
    return pl.pallas_call(
        _combine_kernel,
        grid_spec=grid_spec,
        out_shape=jax.ShapeDtypeStruct((t, d), F32),
        compiler_params=_cparams(("arbitrary",)),
        name="combine",
    )(dest_flat, ys, shared, wt, x, gate)


def _rope_tables(s):
    half = ROPE_DIM // 2
    inv = ROPE_THETA ** (-jnp.arange(half, dtype=F32) * 2.0 / ROPE_DIM)
    ang = jnp.arange(s, dtype=I32).astype(F32)[:, None] * inv[None, :]
    cos, sin = jnp.cos(ang), jnp.sin(ang)
    pad = HEAD_DIM - ROPE_DIM
    rc = jnp.concatenate([cos, cos, jnp.ones((s, pad), F32)], axis=1)
    ra = jnp.concatenate([jnp.zeros((s, half), F32), sin, jnp.zeros((s, pad), F32)], axis=1)
    rb = jnp.concatenate([-sin, jnp.zeros((s, half + pad), F32)], axis=1)
    return rc, ra, rb


def _mixer(x, mod, rope, mix_norm, w_in, sgu_norm, sgu_w, sgu_b, q_norm, k_norm,
           out_norm_sgu, out_norm_att, w_out):
    d = x.shape[1]
    sh_m, sc_m, g_m = mod[:, 0:d], mod[:, d:2 * d], mod[:, 2 * d:3 * d]
    outs = _proj(x, mix_norm.reshape(1, d), sc_m, sh_m, w_in.astype(BF16),
                 q_norm.reshape(1, HEAD_DIM), k_norm.reshape(1, HEAD_DIM), rope)
    y_sgu = _sgu(outs[0], sgu_norm.reshape(1, -1), sgu_w, sgu_b.T, out_norm_sgu.reshape(1, -1))
    y_att = _attn(outs[1:], out_norm_att.reshape(1, -1))
    return _wout(y_sgu, y_att, w_out.astype(BF16), x, g_m)


def _ffn(x, mod, layer, ffn_norm, router_w, router_bias, exp_gate, exp_up, exp_down,
         shared_gate, shared_up, shared_down):
    t, d = x.shape
    sh_f, sc_f, g_f = mod[:, 3 * d:4 * d], mod[:, 4 * d:5 * d], mod[:, 5 * d:6 * d]
    h_bf, idx_t, wt_t, rank_t, cnt = _router(x, ffn_norm.reshape(1, d), sc_f, sh_f, router_w.T,
                                             router_bias.reshape(N_EXPERTS, 1))
    counts = cnt[:, 0].astype(I32)
    items, seg_start = _expert_blocks(counts, t * TOP_K)
    dest_t = _dest(idx_t, rank_t, seg_start)
    nt = t // TOKEN_TILE
    dest_flat = dest_t.reshape(TOP_K, nt, TOKEN_TILE).transpose(1, 0, 2).reshape(-1)
    xs, shared = _dispatch(dest_flat, h_bf, shared_gate.astype(BF16), shared_up.astype(BF16),
                           shared_down.astype(BF16))
    ys = _moe(items, xs, layer, exp_gate, exp_up, exp_down)
    return _combine(dest_flat, ys, shared, wt_t.T, x, g_f)


def kernel(x, c, ada_w, ada_b, mix_norm, w_in, sgu_norm, sgu_w, sgu_b, q_norm, k_norm, out_norm_sgu,
           out_norm_att, w_out, ffn_norm, router_w, router_bias, exp_gate, exp_up, exp_down,
           shared_gate, shared_up, shared_down):
    b, s, d = x.shape
    assert b == 1 and s % ATT_TILE == 0 and d == 2 * ROW_SUB * LANES
    mods = _ada(c, ada_w, ada_b)
    rope = _rope_tables(s)
    xf = x.reshape(s, d)
    for l in range(ada_w.shape[0]):
        xf = _mixer(xf, mods[l], rope, mix_norm[l], w_in[l], sgu_norm[l], sgu_w[l], sgu_b[l], q_norm[l],
                    k_norm[l], out_norm_sgu[l], out_norm_att[l], w_out[l])
        xf = _ffn(xf, mods[l], l, ffn_norm[l], router_w[l], router_bias[l], exp_gate, exp_up,
                  exp_down, shared_gate[l], shared_up[l], shared_down[l])
    return xf.reshape(b, s, d)
```

```python
import functools

import jax
import jax.numpy as jnp
from jax import lax
from jax.experimental import pallas as pl
from jax.experimental.pallas import tpu as pltpu

F32 = jnp.float32
BF16 = jnp.bfloat16
U32 = jnp.uint32
I32 = jnp.int32

HEAD_DIM = 128
N_HEADS = 8
SGU_CHUNK = 128
DILATIONS = (1, 4, 16)
ATT_SPAN = 128
ATT_TILE = 2048
LOG2_E = 1.4426950408889634
ROPE_THETA = 500000.0
ROPE_DIM = HEAD_DIM // 4
N_EXPERTS = 64
TOP_K = 8
N_GROUPS = 8
TOPK_GROUPS = 4
ROUTED_SCALE = 2.5
NORM_EPS = 1e-6
N_MOD = 6

MOE_BLOCK = 512
TOKEN_TILE = 256
HI_MASK = 0xFFFF0000
LANES = 128
ROW_SUB = 8

VMEM_LIMIT = 56 * 1024 * 1024

NT_DIMS = (((1,), (1,)), ((), ()))


def _cparams(sem):
    return pltpu.CompilerParams(dimension_semantics=sem, vmem_limit_bytes=VMEM_LIMIT)


def _rms(x):
    return x * lax.rsqrt(jnp.mean(x * x, axis=-1, keepdims=True) + NORM_EPS)


def _gelu(x):
    return 0.5 * x * (1.0 + lax.erf(x * 0.7071067811865476))


def _pack_pair(lo_f32, hi_f32):
    lo = pltpu.bitcast(lo_f32.astype(BF16).astype(F32), U32)
    hi = pltpu.bitcast(hi_f32.astype(BF16).astype(F32), U32)
    return (lo >> 16) | (hi & jnp.uint32(HI_MASK))


def _unpack_pair(u):
    lo = pltpu.bitcast(u << 16, F32)
    hi = pltpu.bitcast(u & jnp.uint32(HI_MASK), F32)
    return lo, hi


def _store_rows(ref, base, n, lo_f32, hi_f32):
    for c in range(ROW_SUB):
        cols = slice(c * LANES, (c + 1) * LANES)
        ref[pl.ds(base + c, n, stride=ROW_SUB), :] = _pack_pair(lo_f32[:, cols], hi_f32[:, cols])


def _load_rows(ref, base, n):
    los, his = [], []
    for c in range(ROW_SUB):
        lo, hi = _unpack_pair(ref[pl.ds(base + c, n, stride=ROW_SUB), :])
        los.append(lo)
        his.append(hi)
    return jnp.concatenate(los, axis=1), jnp.concatenate(his, axis=1)


def _ada_kernel(c_ref, w_ref, b_ref, o_ref):
    d, tn = w_ref.shape

    def body(i, acc):
        r = pl.multiple_of(i * 8, 8)
        cc = c_ref[pl.ds(r, 8), :]
        return acc + (cc * jax.nn.sigmoid(cc)) * w_ref[pl.ds(r, 8), :]

    acc = lax.fori_loop(0, d // 8, body, jnp.zeros((8, tn), F32), unroll=8)
    o_ref[...] = jnp.sum(acc, axis=0, keepdims=True) + b_ref[...]


def _ada(c, ada_w, ada_b):
    nl, d, n = ada_w.shape
    tn = 1536
    return pl.pallas_call(
        _ada_kernel,
        grid=(nl, n // tn),
        in_specs=[pl.BlockSpec((d, 1), lambda l, j: (0, 0)),
                  pl.BlockSpec((None, d, tn), lambda l, j: (l, 0, j)),
                  pl.BlockSpec((None, 1, tn), lambda l, j: (l, 0, j))],
        out_specs=pl.BlockSpec((None, 1, tn), lambda l, j: (l, 0, j)),
        out_shape=jax.ShapeDtypeStruct((nl, 1, n), F32),
        compiler_params=_cparams(("arbitrary", "arbitrary")),
        name="ada",
    )(c.reshape(d, 1), ada_w, ada_b.reshape(nl, 1, n))


def _proj_kernel(x_ref, g_ref, sc_ref, sh_ref, w_ref, qg_ref, kg_ref, rc_ref, ra_ref, rb_ref,
                 uv_ref, q1_ref, q4_ref, q16_ref, k1_ref, k4_ref, k16_ref, v1_ref, v4_ref, v16_ref,
                 h_s, st_s):
    tm = x_ref.shape[0]
    pair = 2 * HEAD_DIM
    aw = N_HEADS * HEAD_DIM
    y = _rms(x_ref[...]) * g_ref[...]
    h_s[...] = (y * (1.0 + sc_ref[...]) + sh_ref[...]).astype(BF16)

    for p in range(uv_ref.shape[1] // pair):
        cols = slice(p * pair, (p + 1) * pair)
        uv_ref[:, cols] = jnp.dot(h_s[...], w_ref[:, cols], preferred_element_type=F32).astype(BF16)

    groups = ((q1_ref, q4_ref, q16_ref, qg_ref, LOG2_E * HEAD_DIM ** -0.5),
              (k1_ref, k4_ref, k16_ref, kg_ref, 1.0),
              (v1_ref, v4_ref, v16_ref, None, 1.0))
    for gi, (o1, o4, o16, gain_ref, scale) in enumerate(groups):
        for p in range(N_HEADS // 2):
            c0 = uv_ref.shape[1] + gi * aw + p * pair
            acc = jnp.dot(h_s[...], w_ref[:, c0:c0 + pair], preferred_element_type=F32)
            for hh in range(2):
                h = 2 * p + hh
                a = acc[:, hh * HEAD_DIM:(hh + 1) * HEAD_DIM]
                if gain_ref is not None:
                    a = _rms(a) * gain_ref[...]
                    a = (a * rc_ref[...] + pltpu.roll(a, ROPE_DIM // 2, 1) * ra_ref[...]
                         + pltpu.roll(a, HEAD_DIM - ROPE_DIM // 2, 1) * rb_ref[...]) * scale
                o1[:, h * HEAD_DIM:(h + 1) * HEAD_DIM] = a.astype(BF16)
                stage = st_s.at[gi * N_HEADS + h]
                stage[...] = a
                for d, o in ((4, o4), (16, o16)):
                    for r in range(d):
                        c = (h * d + r) * HEAD_DIM
                        o[:, c:c + HEAD_DIM] = stage[pl.ds(r, tm // d, stride=d), :].astype(BF16)


def _proj(x, gain, sc, sh, w_bf, q_gain, k_gain, rope):
    s, d = x.shape
    n = w_bf.shape[1]
    tm = 256
    aw = N_HEADS * HEAD_DIM
    row = lambda i: (i, 0)
    vec = lambda i: (0, 0)
    out_shape = [jax.ShapeDtypeStruct((s, 2 * aw), BF16)]
    out_specs = [pl.BlockSpec((tm, 2 * aw), row)]
    for _ in range(3):
        for dil in DILATIONS:
            out_shape.append(jax.ShapeDtypeStruct((s // dil, dil * aw), BF16))
            out_specs.append(pl.BlockSpec((tm // dil, dil * aw), row))
    return pl.pallas_call(
        _proj_kernel,
        grid=(s // tm,),
        in_specs=[pl.BlockSpec((tm, d), row), pl.BlockSpec((1, d), vec), pl.BlockSpec((1, d), vec),
                  pl.BlockSpec((1, d), vec),
                  pl.BlockSpec((d, n), vec, pipeline_mode=pl.Buffered(1)),
                  pl.BlockSpec((1, HEAD_DIM), vec), pl.BlockSpec((1, HEAD_DIM), vec),
                  pl.BlockSpec((tm, HEAD_DIM), row), pl.BlockSpec((tm, HEAD_DIM), row),
                  pl.BlockSpec((tm, HEAD_DIM), row)],
        out_specs=out_specs,
        out_shape=out_shape,
        scratch_shapes=[pltpu.VMEM((tm, d), BF16), pltpu.VMEM((3 * N_HEADS, tm, HEAD_DIM), F32)],
        compiler_params=_cparams(("arbitrary",)),
        name="proj",
    )(x, gain, sc, sh, w_bf, q_gain, k_gain, *rope)


def _sgu_kernel(uv_ref, gn_ref, w_ref, b_ref, go_ref, o_ref):
    tm = uv_ref.shape[0]
    width = N_HEADS * HEAD_DIM
    ii = lax.broadcasted_iota(I32, (SGU_CHUNK, SGU_CHUNK), 0)
    jj = lax.broadcasted_iota(I32, (SGU_CHUNK, SGU_CHUNK), 1)
    causal = jj <= ii
    for g in range(N_HEADS):
        cols = slice(g * HEAD_DIM, (g + 1) * HEAD_DIM)
        wg = jnp.where(causal, w_ref[g], 0.0).astype(BF16)
        u = _gelu(uv_ref[:, cols].astype(F32))
        v = _gelu(uv_ref[:, width + g * HEAD_DIM:width + (g + 1) * HEAD_DIM].astype(F32))
        vn = (_rms(v) * gn_ref[:, cols]).astype(BF16)
        bias = b_ref[:, g:g + 1]
        for n in range(tm // SGU_CHUNK):
            rows = slice(n * SGU_CHUNK, (n + 1) * SGU_CHUNK)
            mixed = jnp.dot(wg, vn[rows], preferred_element_type=F32) + bias
            y = u[rows] * mixed
            o_ref[rows, cols] = (_rms(y) * go_ref[:, cols]).astype(BF16)


def _sgu(uv, sgu_norm, sgu_w, sgu_b_t, out_norm):
    s = uv.shape[0]
    width = N_HEADS * HEAD_DIM
    tm = 512
    vec = lambda i: (0, 0)
    return pl.pallas_call(
        _sgu_kernel,
        grid=(s // tm,),
        in_specs=[pl.BlockSpec((tm, 2 * width), lambda i: (i, 0)),
                  pl.BlockSpec((1, width), vec),
                  pl.BlockSpec((N_HEADS, SGU_CHUNK, SGU_CHUNK), lambda i: (0, 0, 0)),
                  pl.BlockSpec((SGU_CHUNK, N_HEADS), vec),
                  pl.BlockSpec((1, width), vec)],
        out_specs=pl.BlockSpec((tm, width), lambda i: (i, 0)),
        out_shape=jax.ShapeDtypeStruct((s, width), BF16),
        compiler_params=_cparams(("arbitrary",)),
        name="sgu",
    )(uv, sgu_norm, sgu_w, sgu_b_t, out_norm)


def _attn_kernel(q1, q4, q16, k1, k4, k16, k1p, k4p, k16p, v1, v4, v16, v1p, v4p, v16p, go_ref,
                 o_ref, kb1, kb4, kb16, vb1, vb4, vb16, os1, os4, os16, ls1, ls4, ls16):
    first = pl.program_id(0) == 0
    blk = ATT_SPAN
    for buf, prev, cur in ((kb1, k1p, k1), (kb4, k4p, k4), (kb16, k16p, k16),
                           (vb1, v1p, v1), (vb4, v4p, v4), (vb16, v16p, v16)):
        buf[0:blk, :] = prev[...]
        buf[blk:, :] = cur[...]

    qi = lax.broadcasted_iota(I32, (blk, 2 * blk), 0)
    kj = lax.broadcasted_iota(I32, (blk, 2 * blk), 1)
    dist = qi + blk - kj
    band = (dist >= 0) & (dist <= ATT_SPAN)
    neg = jnp.where(band, 0.0, -jnp.inf)
    neg0 = jnp.where(first, jnp.where(band & (kj >= blk), 0.0, -jnp.inf), neg)

    for d, q_ref, kb, vb, o_s, l_s in ((1, q1, kb1, vb1, os1, ls1), (4, q4, kb4, vb4, os4, ls4),
                                       (16, q16, kb16, vb16, os16, ls16)):
        nb = ATT_TILE // (blk * d)
        for r in range(d):
            cols = slice(r * HEAD_DIM, (r + 1) * HEAD_DIM)
            for b in range(nb):
                q = q_ref[b * blk:(b + 1) * blk, cols]
                kk = kb[b * blk:(b + 2) * blk, cols]
                vv = vb[b * blk:(b + 2) * blk, cols]
                s = lax.dot_general(q, kk, NT_DIMS, preferred_element_type=F32)
                s = s + (neg0 if b == 0 else neg)
                m = jnp.max(s, axis=-1, keepdims=True)
                e = jnp.exp2(s - m)
                l = jnp.sum(e, axis=-1, keepdims=True)
                o = jnp.dot(e.astype(BF16), vv, preferred_element_type=F32) / l
                lse = jnp.broadcast_to(m + jnp.log2(l), (blk, HEAD_DIM))
                if d == 1:
                    o_s[b * blk:(b + 1) * blk, :] = o
                    l_s[b * blk:(b + 1) * blk, :] = lse
                else:
                    o_s[pl.ds(d * b * blk + r, blk, stride=d), :] = o
                    l_s[pl.ds(d * b * blk + r, blk, stride=d), :] = lse

    step = 256
    for c in range(ATT_TILE // step):
        rows = slice(c * step, (c + 1) * step)
        l1, l4, l16 = ls1[rows, :], ls4[rows, :], ls16[rows, :]
        mx = jnp.maximum(l1, jnp.maximum(l4, l16))
        w1, w4, w16 = jnp.exp2(l1 - mx), jnp.exp2(l4 - mx), jnp.exp2(l16 - mx)
        o = (w1 * os1[rows, :] + w4 * os4[rows, :] + w16 * os16[rows, :]) / (w1 + w4 + w16)
        o_ref[rows, :] = (_rms(o) * go_ref[...]).astype(BF16)


def _attn(qkv, out_norm):
    q1, q4, q16, k1, k4, k16, v1, v4, v16 = qkv
    s = q1.shape[0]
    nt = s // ATT_TILE
    blk = ATT_SPAN

    def cur(d):
        return pl.BlockSpec((ATT_TILE // d, d * HEAD_DIM), lambda i, h: (i, h))

    def prev(d):
        per = ATT_TILE // (d * blk)
        return pl.BlockSpec((blk, d * HEAD_DIM), lambda i, h: (jnp.maximum(i * per - 1, 0), h))

    in_specs = ([cur(d) for d in DILATIONS] + [cur(d) for d in DILATIONS] + [prev(d) for d in DILATIONS]
                + [cur(d) for d in DILATIONS] + [prev(d) for d in DILATIONS]
                + [pl.BlockSpec((1, HEAD_DIM), lambda i, h: (0, h))])
    kv_bufs = [pltpu.VMEM((blk + ATT_TILE // d, d * HEAD_DIM), BF16) for d in DILATIONS]
    acc_bufs = [pltpu.VMEM((ATT_TILE, HEAD_DIM), F32) for _ in DILATIONS]
    return pl.pallas_call(
        _attn_kernel,
        grid=(nt, N_HEADS),
        in_specs=in_specs,
        out_specs=pl.BlockSpec((ATT_TILE, HEAD_DIM), lambda i, h: (i, h)),
        out_shape=jax.ShapeDtypeStruct((s, N_HEADS * HEAD_DIM), BF16),
        scratch_shapes=kv_bufs + kv_bufs + acc_bufs + acc_bufs,
        compiler_params=_cparams(("arbitrary", "arbitrary")),
        name="attn",
    )(q1, q4, q16, k1, k4, k16, k1, k4, k16, v1, v4, v16, v1, v4, v16, out_norm)


def _wout_kernel(a1_ref, a2_ref, w_ref, x_ref, g_ref, o_ref):
    half = a1_ref.shape[1]
    acc = jnp.dot(a1_ref[...], w_ref[0:half, :], preferred_element_type=F32)
    acc = acc + jnp.dot(a2_ref[...], w_ref[half:2 * half, :], preferred_element_type=F32)
    o_ref[...] = x_ref[...] + g_ref[...] * acc


def _wout(y_sgu, y_att, w_bf, x, gate):
    s, d = x.shape
    half = y_sgu.shape[1]
    tm = 512
    row = lambda i: (i, 0)
    vec = lambda i: (0, 0)
    return pl.pallas_call(
        _wout_kernel,
        grid=(s // tm,),
        in_specs=[pl.BlockSpec((tm, half), row), pl.BlockSpec((tm, half), row),
                  pl.BlockSpec((2 * half, d), vec, pipeline_mode=pl.Buffered(1)),
                  pl.BlockSpec((tm, d), row), pl.BlockSpec((1, d), vec)],
        out_specs=pl.BlockSpec((tm, d), row),
        out_shape=jax.ShapeDtypeStruct((s, d), F32),
        compiler_params=_cparams(("arbitrary",)),
        name="wout",
    )(y_sgu, y_att, w_bf, x, gate)


def _router_kernel(x_ref, g_ref, sc_ref, sh_ref, rwt_ref, rb_ref,
                   h_ref, idx_ref, wt_ref, rank_ref, cnt_ref, carry_s):
    tm = x_ref.shape[0]
    gsz = N_EXPERTS // N_GROUPS

    @pl.when(pl.program_id(0) == 0)
    def _():
        carry_s[...] = jnp.zeros_like(carry_s)

    h = (_rms(x_ref[...]) * g_ref[...]) * (1.0 + sc_ref[...]) + sh_ref[...]
    h_ref[...] = h.astype(BF16)

    hh = h.astype(BF16)
    hl = (h - hh.astype(F32)).astype(BF16)
    rw = rwt_ref[...]
    rh = rw.astype(BF16)
    rl = (rw - rh.astype(F32)).astype(BF16)
    dg = functools.partial(lax.dot_general, dimension_numbers=NT_DIMS, preferred_element_type=F32)
    logits = dg(rh, hh) + dg(rl, hh) + dg(rh, hl)
    scores = jax.nn.sigmoid(logits)
    sel = scores + rb_ref[...]

    io8 = lax.broadcasted_iota(I32, (gsz, tm), 0)
    grp, gscore = [], []
    for g in range(N_GROUPS):
        sg = sel[g * gsz:(g + 1) * gsz, :]
        m1 = jnp.max(sg, axis=0, keepdims=True)
        i1 = jnp.min(jnp.where(sg == m1, io8, gsz), axis=0, keepdims=True)
        m2 = jnp.max(jnp.where(io8 == i1, -jnp.inf, sg), axis=0, keepdims=True)
        grp.append(sg)
        gscore.append(m1 + m2)

    parts = []
    for g in range(N_GROUPS):
        beaten = jnp.zeros((1, tm), I32)
        for g2 in range(N_GROUPS):
            if g2 == g:
                continue
            b = (gscore[g2] >= gscore[g]) if g2 < g else (gscore[g2] > gscore[g])
            beaten = beaten + b.astype(I32)
        parts.append(jnp.where(beaten < TOPK_GROUPS, grp[g], -jnp.inf))
    masked = jnp.concatenate(parts, axis=0)

    io = lax.broadcasted_iota(I32, (N_EXPERTS, tm), 0)
    chosen = jnp.zeros((N_EXPERTS, tm), F32)
    idxs, wts = [], []
    for _ in range(TOP_K):
        m = jnp.max(masked, axis=0, keepdims=True)
        am = jnp.min(jnp.where(masked == m, io, N_EXPERTS), axis=0, keepdims=True)
        hit = io == am
        wts.append(jnp.sum(jnp.where(hit, scores, 0.0), axis=0, keepdims=True))
        idxs.append(am)
        chosen = chosen + hit.astype(F32)
        masked = jnp.where(hit, -jnp.inf, masked)
    wsum = wts[0]
    for k in range(1, TOP_K):
        wsum = wsum + wts[k]
    for k in range(TOP_K):
        idx_ref[k:k + 1, :] = idxs[k]
        wt_ref[k:k + 1, :] = wts[k] / wsum * ROUTED_SCALE

    t0 = lax.broadcasted_iota(I32, (tm, tm), 0)
    t1 = lax.broadcasted_iota(I32, (tm, tm), 1)
    before = (t0 < t1).astype(BF16)
    pre = jnp.dot(chosen.astype(BF16), before, preferred_element_type=F32) + carry_s[...]
    for k in range(TOP_K):
        rank_ref[k:k + 1, :] = jnp.sum(jnp.where(io == idxs[k], pre, 0.0), axis=0,
                                       keepdims=True).astype(I32)
    carry_s[...] = carry_s[...] + jnp.sum(chosen, axis=1, keepdims=True)
    cnt_ref[...] = jnp.broadcast_to(carry_s[...], cnt_ref.shape)


def _router(x, gain, sc, sh, rw_t, rbias):
    t, d = x.shape
    tm = 512
    vec = lambda i: (0, 0)
    tok = lambda i: (0, i)
    return pl.pallas_call(
        _router_kernel,
        grid=(t // tm,),
        in_specs=[pl.BlockSpec((tm, d), lambda i: (i, 0)), pl.BlockSpec((1, d), vec),
                  pl.BlockSpec((1, d), vec), pl.BlockSpec((1, d), vec),
                  pl.BlockSpec((N_EXPERTS, d), vec), pl.BlockSpec((N_EXPERTS, 1), vec)],
        out_specs=[pl.BlockSpec((tm, d), lambda i: (i, 0)), pl.BlockSpec((TOP_K, tm), tok),
                   pl.BlockSpec((TOP_K, tm), tok), pl.BlockSpec((TOP_K, tm), tok),
                   pl.BlockSpec((N_EXPERTS, HEAD_DIM), vec)],
        out_shape=[jax.ShapeDtypeStruct((t, d), BF16), jax.ShapeDtypeStruct((TOP_K, t), I32),
                   jax.ShapeDtypeStruct((TOP_K, t), F32), jax.ShapeDtypeStruct((TOP_K, t), I32),
                   jax.ShapeDtypeStruct((N_EXPERTS, HEAD_DIM), F32)],
        scratch_shapes=[pltpu.VMEM((N_EXPERTS, 1), F32)],
        compiler_params=_cparams(("arbitrary",)),
        name="router",
    )(x, gain, sc, sh, rw_t, rbias)


def _dest_kernel(idx_ref, rank_ref, start_ref, o_ref):
    tm = idx_ref.shape[1]
    io = lax.broadcasted_iota(I32, (N_EXPERTS, tm), 0)
    for k in range(TOP_K):
        seg = jnp.sum(jnp.where(io == idx_ref[k:k + 1, :], start_ref[...], 0), axis=0, keepdims=True)
        o_ref[k:k + 1, :] = (seg + rank_ref[k:k + 1, :]) * ROW_SUB


def _dest(idx_t, rank_t, seg_start):
    t = idx_t.shape[1]
    tm = 2048
    tok = lambda i: (0, i)
    return pl.pallas_call(
        _dest_kernel,
        grid=(t // tm,),
        in_specs=[pl.BlockSpec((TOP_K, tm), tok), pl.BlockSpec((TOP_K, tm), tok),
                  pl.BlockSpec((N_EXPERTS, 1), lambda i: (0, 0))],
        out_specs=pl.BlockSpec((TOP_K, tm), tok),
        out_shape=jax.ShapeDtypeStruct((TOP_K, t), I32),
        compiler_params=_cparams(("arbitrary",)),
        name="dest",
    )(idx_t, rank_t, seg_start.reshape(N_EXPERTS, 1))


def _dispatch_kernel(dest_s, h_ref, sg_ref, su_ref, sd_ref, xs_hbm, sh_ref, pk_s, sem_row):
    tm, d = h_ref.shape
    n = TOP_K * tm
    i = pl.program_id(0)
    last = pl.num_programs(0) - 1
    cur = i % 2
    base = i * n

    def row_copy(src_row, dst_row, buf):
        return pltpu.make_async_copy(pk_s.at[buf, pl.ds(src_row, ROW_SUB)],
                                     xs_hbm.at[pl.ds(dst_row, ROW_SUB)], sem_row.at[buf])

    def wait_all(buf):
        def wait(j, c):
            row_copy(0, 0, buf).wait()
            return c
        lax.fori_loop(0, n, wait, 0, unroll=8)

    @pl.when(i >= 2)
    def _():
        wait_all(cur)

    _store_rows(pk_s.at[cur], 0, tm, h_ref[:, 0:d // 2].astype(F32), h_ref[:, d // 2:d].astype(F32))

    vals = {}

    def p_gate():
        vals["gate"] = jnp.dot(h_ref[...], sg_ref[...], preferred_element_type=F32)

    def p_up():
        up = jnp.dot(h_ref[...], su_ref[...], preferred_element_type=F32)
        g = vals["gate"]
        vals["hid"] = (g * jax.nn.sigmoid(g) * up).astype(BF16)

    def p_shared(m):
        def run():
            cols = slice(m * 2 * LANES, (m + 1) * 2 * LANES)
            sh_ref[:, cols] = jnp.dot(vals["hid"], sd_ref[:, cols],
                                      preferred_element_type=F32).astype(sh_ref.dtype)
        return run

    pieces = [p_gate, p_up] + [p_shared(m) for m in range(d // (2 * LANES))]
    j = 0
    for idx, piece in enumerate(pieces):
        for _ in range(n // len(pieces) + (1 if idx < n % len(pieces) else 0)):
            row_copy((j % tm) * ROW_SUB, pl.multiple_of(dest_s[base + j], ROW_SUB), cur).start(priority=j % 2)
            j += 1
        piece()

    @pl.when(i == last)
    def _():
        wait_all(cur)

    @pl.when((i == last) & (i >= 1))
    def _():
        wait_all(1 - cur)

    @pl.when(i == last)
    def _():
        pk_s[...] = jnp.zeros_like(pk_s)
        tail = xs_hbm.shape[0] - 2 * tm * ROW_SUB
        for hb in range(2):
            pad = pltpu.make_async_copy(pk_s.at[hb], xs_hbm.at[pl.ds(tail + hb * tm * ROW_SUB, tm * ROW_SUB)],
                                        sem_row.at[hb])
            pad.start()
            pad.wait()


def _dispatch(dest_flat, h_bf, sg_bf, su_bf, sd_bf):
    t, d = h_bf.shape
    f = sg_bf.shape[1]
    tm = TOKEN_TILE
    assert MOE_BLOCK == 2 * tm
    row = lambda i, dest: (i, 0)
    vec = lambda i, dest: (0, 0)
    grid_spec = pltpu.PrefetchScalarGridSpec(
        num_scalar_prefetch=1,
        grid=(t // tm,),
        in_specs=[pl.BlockSpec((tm, d), row),
                  pl.BlockSpec((d, f), vec), pl.BlockSpec((d, f), vec), pl.BlockSpec((f, d), vec)],
        out_specs=[pl.BlockSpec(memory_space=pl.ANY), pl.BlockSpec((tm, d), row)],
        scratch_shapes=[pltpu.VMEM((2, tm * ROW_SUB, LANES), U32), pltpu.SemaphoreType.DMA((2,))])
    return pl.pallas_call(
        _dispatch_kernel,
        grid_spec=grid_spec,
        out_shape=[jax.ShapeDtypeStruct(((t * TOP_K + MOE_BLOCK) * ROW_SUB, LANES), U32),
                   jax.ShapeDtypeStruct((t, d), BF16)],
        compiler_params=_cparams(("arbitrary",)),
        name="dispatch",
    )(dest_flat, h_bf, sg_bf, su_bf, sd_bf)


def _expert_blocks(counts, m):
    bm = MOE_BLOCK
    n_items = m // bm + N_EXPERTS
    seg_end = jnp.cumsum(counts)
    seg_start = seg_end - counts
    nblk = (counts + bm - 1) // bm
    item_end = jnp.cumsum(nblk)
    w = jnp.arange(n_items, dtype=I32)
    active = (w < item_end[-1]).astype(I32)
    ex = jnp.minimum(jnp.sum((item_end[None, :] <= w[:, None]).astype(I32), axis=1), N_EXPERTS - 1)
    onehot = ex[:, None] == jnp.arange(N_EXPERTS, dtype=I32)[None, :]

    def pick(table):
        return jnp.sum(jnp.where(onehot, table[None, :], 0), axis=1)

    i_in = w - pick(item_end - nblk)
    row0 = jnp.where(active == 1, (pick(seg_start) + i_in * bm) * ROW_SUB, 0)
    newe = ((i_in == 0) & (active == 1)).astype(I32)
    cand = jnp.where(counts > 0, jnp.arange(N_EXPERTS, dtype=I32), N_EXPERTS)
    nxt_tab = jnp.concatenate([lax.cummin(cand[::-1])[::-1][1:], jnp.full((1,), N_EXPERTS, I32)])
    nxt = pick(nxt_tab)
    wsel = jnp.where(newe == 1, ex, jnp.where(nxt < N_EXPERTS, nxt, ex))
    return (row0, wsel, active, newe), seg_start


def _moe_kernel(row_s, wsel_s, act_s, newe_s, xs_hbm, wg_ref, wu_ref, wd_ref, ys_hbm,
                x_s, y_s, wg_s, wu_s, wd_s, sem_x, sem_y):
    w = pl.program_id(0)
    n = pl.num_programs(0)
    cur = w % 2
    rows = x_s.shape[1]
    bm = rows // ROW_SUB
    d = wg_s.shape[0]
    half = d // 2
    active = act_s[w] == 1
    prev_active = (w >= 1) & (act_s[jnp.maximum(w - 1, 0)] == 1)
    nxt = jnp.minimum(w + 1, n - 1)

    def x_copy(step, buf):
        r = pl.multiple_of(row_s[step], ROW_SUB)
        return pltpu.make_async_copy(xs_hbm.at[pl.ds(r, rows)], x_s.at[buf], sem_x.at[buf])

    def y_copy(step, buf):
        r = pl.multiple_of(row_s[step], ROW_SUB)
        return pltpu.make_async_copy(y_s.at[buf], ys_hbm.at[pl.ds(r, rows)], sem_y.at[buf])

    @pl.when((w == 0) & active)
    def _():
        x_copy(0, 0).start()

    @pl.when(newe_s[w] == 1)
    def _():
        wg_s[...] = wg_ref[...].astype(BF16)
        wu_s[...] = wu_ref[...].astype(BF16)
        wd_s[...] = wd_ref[...].astype(BF16)

    @pl.when(active)
    def _():
        x_copy(w, cur).wait()

        @pl.when((w + 1 < n) & (act_s[nxt] == 1))
        def _():
            x_copy(nxt, 1 - cur).start()

        xl, xh = _load_rows(x_s.at[cur], 0, bm)
        xl, xh = xl.astype(BF16), xh.astype(BF16)
        gate = (jnp.dot(xl, wg_s[0:half, :], preferred_element_type=F32)
                + jnp.dot(xh, wg_s[half:d, :], preferred_element_type=F32))
        up = (jnp.dot(xl, wu_s[0:half, :], preferred_element_type=F32)
              + jnp.dot(xh, wu_s[half:d, :], preferred_element_type=F32))
        hid = (gate * jax.nn.sigmoid(gate) * up).astype(BF16)
        y = jnp.dot(hid, wd_s[...], preferred_element_type=F32)
        _store_rows(y_s.at[cur], 0, bm, y[:, 0:half], y[:, half:d])

        @pl.when(w >= 1)
        def _():
            y_copy(w - 1, 1 - cur).wait()

        y_copy(w, cur).start()

        @pl.when(w == n - 1)
        def _():
            y_copy(w, cur).wait()

    @pl.when(jnp.logical_not(active) & prev_active)
    def _():
        y_copy(w - 1, 1 - cur).wait()


def _moe(items, xs, layer, w_gate, w_up, w_down):
    _, _, d, f = w_gate.shape
    bm = MOE_BLOCK
    wmap = lambda w, row, ws, act, ne: (layer, ws[w], 0, 0)
    grid_spec = pltpu.PrefetchScalarGridSpec(
        num_scalar_prefetch=4,
        grid=(items[0].shape[0],),
        in_specs=[pl.BlockSpec(memory_space=pl.ANY),
                  pl.BlockSpec((None, None, d, f), wmap),
                  pl.BlockSpec((None, None, d, f), wmap),
                  pl.BlockSpec((None, None, f, d), wmap)],
        out_specs=pl.BlockSpec(memory_space=pl.ANY),
        scratch_shapes=[pltpu.VMEM((2, bm * ROW_SUB, LANES), U32), pltpu.VMEM((2, bm * ROW_SUB, LANES), U32),
                        pltpu.VMEM((d, f), BF16), pltpu.VMEM((d, f), BF16), pltpu.VMEM((f, d), BF16),
                        pltpu.SemaphoreType.DMA((2,)), pltpu.SemaphoreType.DMA((2,))])
    return pl.pallas_call(
        _moe_kernel,
        grid_spec=grid_spec,
        out_shape=jax.ShapeDtypeStruct(xs.shape, U32),
        compiler_params=_cparams(("arbitrary",)),
        name="moe",
    )(*items, xs, w_gate, w_up, w_down)


def _combine_kernel(dest_s, ys_hbm, sh_ref, wt_ref, x_ref, g_ref, o_ref, y_s, sem_row):
    tm, d = x_ref.shape
    half = d // 2
    n = TOP_K * tm
    i = pl.program_id(0)
    last = pl.num_programs(0) - 1
    cur = i % 2
    nxt = 1 - cur
    base_next = jnp.minimum(i + 1, last) * n

    def row_copy(src_row, dst_row, buf):
        return pltpu.make_async_copy(ys_hbm.at[pl.ds(src_row, ROW_SUB)],
                                     y_s.at[buf, pl.ds(dst_row, ROW_SUB)], sem_row.at[buf])

    def wait_all(buf):
        def wait(j, c):
            row_copy(0, 0, buf).wait()
            return c
        lax.fori_loop(0, n, wait, 0, unroll=8)

    @pl.when(i == 0)
    def _():
        def start(j, c):
            row_copy(pl.multiple_of(dest_s[j], ROW_SUB), pl.multiple_of(j * ROW_SUB, ROW_SUB), 0).start()
            return c
        lax.fori_loop(0, n, start, 0, unroll=8)

    wait_all(cur)

    def p_routed(c):
        def run():
            r_lo = jnp.zeros((tm, LANES), F32)
            r_hi = jnp.zeros((tm, LANES), F32)
            for k in range(TOP_K):
                lo, hi = _unpack_pair(y_s[cur, pl.ds(k * tm * ROW_SUB + c, tm, stride=ROW_SUB), :])
                wk = wt_ref[:, k:k + 1]
                r_lo = r_lo + wk * lo
                r_hi = r_hi + wk * hi
            for r, off in ((r_lo, c * LANES), (r_hi, half + c * LANES)):
                cols = slice(off, off + LANES)
                o_ref[:, cols] = x_ref[:, cols] + g_ref[:, cols] * (r + sh_ref[:, cols].astype(F32))
        return run

    pieces = [p_routed(c) for c in range(ROW_SUB)]
    j = 0
    for idx, piece in enumerate(pieces):
        for _ in range(n // len(pieces) + (1 if idx < n % len(pieces) else 0)):
            row_copy(pl.multiple_of(dest_s[base_next + j], ROW_SUB), j * ROW_SUB, nxt).start(priority=j % 2)
            j += 1
        piece()

    @pl.when(i == last)
    def _():
        wait_all(nxt)


def _combine(dest_flat, ys, shared, wt, x, gate):
    t, d = x.shape
    tm = TOKEN_TILE
    n = TOP_K * tm
    row = lambda i, dest: (i, 0)
    vec = lambda i, dest: (0, 0)
    grid_spec = pltpu.PrefetchScalarGridSpec(
        num_scalar_prefetch=1,
        grid=(t // tm,),
        in_specs=[pl.BlockSpec(memory_space=pl.ANY),
                  pl.BlockSpec((tm, d), row), pl.BlockSpec((tm, TOP_K), row),
                  pl.BlockSpec((tm, d), row), pl.BlockSpec((1, d), vec)],
        out_specs=pl.BlockSpec((tm, d), row),
        scratch_shapes=[pltpu.VMEM((2, n * ROW_SUB, LANES), U32), pltpu.SemaphoreType.DMA((2,))])
    return pl.pallas_call(
        _combine_kernel,
        grid_spec=grid_spec,
        out_shape=jax.ShapeDtypeStruct((t, d), F32),
        compiler_params=_cparams(("arbitrary",)),
        name="combine",
    )(dest_flat, ys, shared, wt, x, gate)


def _rope_tables(s):
    half = ROPE_DIM // 2
    inv = ROPE_THETA ** (-jnp.arange(half, dtype=F32) * 2.0 / ROPE_DIM)
    ang = jnp.arange(s, dtype=I32).astype(F32)[:, None] * inv[None, :]
    cos, sin = jnp.cos(ang), jnp.sin(ang)
    pad = HEAD_DIM - ROPE_DIM
    rc = jnp.concatenate([cos, cos, jnp.ones((s, pad), F32)], axis=1)
    ra = jnp.concatenate([jnp.zeros((s, half), F32), sin, jnp.zeros((s, pad), F32)], axis=1)
    rb = jnp.concatenate([-sin, jnp.zeros((s, half + pad), F32)], axis=1)
    return rc, ra, rb


def _mixer(x, mod, rope, mix_norm, w_in, sgu_norm, sgu_w, sgu_b, q_norm, k_norm,
           out_norm_sgu, out_norm_att, w_out):
    d = x.shape[1]
    sh_m, sc_m, g_m = mod[:, 0:d], mod[:, d:2 * d], mod[:, 2 * d:3 * d]
    outs = _proj(x, mix_norm.reshape(1, d), sc_m, sh_m, w_in.astype(BF16),
                 q_norm.reshape(1, HEAD_DIM), k_norm.reshape(1, HEAD_DIM), rope)
    y_sgu = _sgu(outs[0], sgu_norm.reshape(1, -1), sgu_w, sgu_b.T, out_norm_sgu.reshape(1, -1))
    y_att = _attn(outs[1:], out_norm_att.reshape(1, -1))
    return _wout(y_sgu, y_att, w_out.astype(BF16), x, g_m)


def _ffn(x, mod, layer, ffn_norm, router_w, router_bias, exp_gate, exp_up, exp_down,
         shared_gate, shared_up, shared_down):
    t, d = x.shape
    sh_f, sc_f, g_f = mod[:, 3 * d:4 * d], mod[:, 4 * d:5 * d], mod[:, 5 * d:6 * d]
    h_bf, idx_t, wt_t, rank_t, cnt = _router(x, ffn_norm.reshape(1, d), sc_f, sh_f, router_w.T,
                                             router_bias.reshape(N_EXPERTS, 1))
    counts = cnt[:, 0].astype(I32)
    items, seg_start = _expert_blocks(counts, t * TOP_K)
    dest_t = _dest(idx_t, rank_t, seg_start)
    nt = t // TOKEN_TILE
    dest_flat = dest_t.reshape(TOP_K, nt, TOKEN_TILE).transpose(1, 0, 2).reshape(-1)
    xs, shared = _dispatch(dest_flat, h_bf, shared_gate.astype(BF16), shared_up.astype(BF16),
                           shared_down.astype(BF16))
    ys = _moe(items, xs, layer, exp_gate, exp_up, exp_down)
    return _combine(dest_flat, ys, shared, wt_t.T, x, g_f)


def kernel(x, c, ada_w, ada_b, mix_norm, w_in, sgu_norm, sgu_w, sgu_b, q_norm, k_norm, out_norm_sgu,
           out_norm_att, w_out, ffn_norm, router_w, router_bias, exp_gate, exp_up, exp_down,
           shared_gate, shared_up, shared_down):
    b, s, d = x.shape
    assert b == 1 and s % ATT_TILE == 0 and d == 2 * ROW_SUB * LANES
    mods = _ada(c, ada_w, ada_b)
    rope = _rope_tables(s)
    xf = x.reshape(s, d)
    for l in range(ada_w.shape[0]):
        xf = _mixer(xf, mods[l], rope, mix_norm[l], w_in[l], sgu_norm[l], sgu_w[l], sgu_b[l], q_norm[l],
                    k_norm[l], out_norm_sgu[l], out_norm_att[l], w_out[l])
        xf = _ffn(xf, mods[l], l, ffn_norm[l], router_w[l], router_bias[l], exp_gate, exp_up,
                  exp_down, shared_gate[l], shared_up[l], shared_down[l])
    return xf.reshape(b, s, d)
```

```python
import functools

import jax
import jax.numpy as jnp
from jax import lax
from jax.experimental import pallas as pl
from jax.experimental.pallas import tpu as pltpu

F32 = jnp.float32
BF16 = jnp.bfloat16
U32 = jnp.uint32
I32 = jnp.int32

HEAD_DIM = 128
N_HEADS = 8
SGU_CHUNK = 128
DILATIONS = (1, 4, 16)
ATT_SPAN = 128
ATT_TILE = 2048
LOG2_E = 1.4426950408889634
ROPE_THETA = 500000.0
ROPE_DIM = HEAD_DIM // 4
N_EXPERTS = 64
TOP_K = 8
N_GROUPS = 8
TOPK_GROUPS = 4
ROUTED_SCALE = 2.5
NORM_EPS = 1e-6
N_MOD = 6

MOE_BLOCK = 512
TOKEN_TILE = 256
HI_MASK = 0xFFFF0000
LANES = 128
ROW_SUB = 8

VMEM_LIMIT = 56 * 1024 * 1024

NT_DIMS = (((1,), (1,)), ((), ()))


def _cparams(sem):
    return pltpu.CompilerParams(dimension_semantics=sem, vmem_limit_bytes=VMEM_LIMIT)


def _rms(x):
    return x * lax.rsqrt(jnp.mean(x * x, axis=-1, keepdims=True) + NORM_EPS)


def _gelu(x):
    return 0.5 * x * (1.0 + lax.erf(x * 0.7071067811865476))


def _pack_pair(lo_f32, hi_f32):
    lo = pltpu.bitcast(lo_f32.astype(BF16).astype(F32), U32)
    hi = pltpu.bitcast(hi_f32.astype(BF16).astype(F32), U32)
    return (lo >> 16) | (hi & jnp.uint32(HI_MASK))


def _unpack_pair(u):
    lo = pltpu.bitcast(u << 16, F32)
    hi = pltpu.bitcast(u & jnp.uint32(HI_MASK), F32)
    return lo, hi


def _store_rows(ref, base, n, lo_f32, hi_f32):
    for c in range(ROW_SUB):
        cols = slice(c * LANES, (c + 1) * LANES)
        ref[pl.ds(base + c, n, stride=ROW_SUB), :] = _pack_pair(lo_f32[:, cols], hi_f32[:, cols])


def _load_rows(ref, base, n):
    los, his = [], []
    for c in range(ROW_SUB):
        lo, hi = _unpack_pair(ref[pl.ds(base + c, n, stride=ROW_SUB), :])
        los.append(lo)
        his.append(hi)
    return jnp.concatenate(los, axis=1), jnp.concatenate(his, axis=1)


def _ada_kernel(c_ref, w_ref, b_ref, o_ref):
    d, tn = w_ref.shape

    def body(i, acc):
        r = pl.multiple_of(i * 8, 8)
        cc = c_ref[pl.ds(r, 8), :]
        return acc + (cc * jax.nn.sigmoid(cc)) * w_ref[pl.ds(r, 8), :]

    acc = lax.fori_loop(0, d // 8, body, jnp.zeros((8, tn), F32), unroll=8)
    o_ref[...] = jnp.sum(acc, axis=0, keepdims=True) + b_ref[...]


def _ada(c, ada_w, ada_b):
    nl, d, n = ada_w.shape
    tn = 1536
    return pl.pallas_call(
        _ada_kernel,
        grid=(nl, n // tn),
        in_specs=[pl.BlockSpec((d, 1), lambda l, j: (0, 0)),
                  pl.BlockSpec((None, d, tn), lambda l, j: (l, 0, j)),
                  pl.BlockSpec((None, 1, tn), lambda l, j: (l, 0, j))],
        out_specs=pl.BlockSpec((None, 1, tn), lambda l, j: (l, 0, j)),
        out_shape=jax.ShapeDtypeStruct((nl, 1, n), F32),
        compiler_params=_cparams(("arbitrary", "arbitrary")),
        name="ada",
    )(c.reshape(d, 1), ada_w, ada_b.reshape(nl, 1, n))


def _proj_kernel(x_ref, g_ref, sc_ref, sh_ref, w_ref, qg_ref, kg_ref, rc_ref, ra_ref, rb_ref,
                 uv_ref, q1_ref, q4_ref, q16_ref, k1_ref, k4_ref, k16_ref, v1_ref, v4_ref, v16_ref,
                 h_s, st_s):
    tm = x_ref.shape[0]
    pair = 2 * HEAD_DIM
    aw = N_HEADS * HEAD_DIM
    y = _rms(x_ref[...]) * g_ref[...]
    h_s[...] = (y * (1.0 + sc_ref[...]) + sh_ref[...]).astype(BF16)

    for p in range(uv_ref.shape[1] // pair):
        cols = slice(p * pair, (p + 1) * pair)
        uv_ref[:, cols] = jnp.dot(h_s[...], w_ref[:, cols], preferred_element_type=F32).astype(BF16)

    groups = ((q1_ref, q4_ref, q16_ref, qg_ref, LOG2_E * HEAD_DIM ** -0.5),
              (k1_ref, k4_ref, k16_ref, kg_ref, 1.0),
              (v1_ref, v4_ref, v16_ref, None, 1.0))
    for gi, (o1, o4, o16, gain_ref, scale) in enumerate(groups):
        for p in range(N_HEADS // 2):
            c0 = uv_ref.shape[1] + gi * aw + p * pair
            acc = jnp.dot(h_s[...], w_ref[:, c0:c0 + pair], preferred_element_type=F32)
            for hh in range(2):
                h = 2 * p + hh
                a = acc[:, hh * HEAD_DIM:(hh + 1) * HEAD_DIM]
                if gain_ref is not None:
                    a = _rms(a) * gain_ref[...]
                    a = (a * rc_ref[...] + pltpu.roll(a, ROPE_DIM // 2, 1) * ra_ref[...]
                         + pltpu.roll(a, HEAD_DIM - ROPE_DIM // 2, 1) * rb_ref[...]) * scale
                o1[:, h * HEAD_DIM:(h + 1) * HEAD_DIM] = a.astype(BF16)
                stage = st_s.at[gi * N_HEADS + h]
                stage[...] = a
                for d, o in ((4, o4), (16, o16)):
                    for r in range(d):
                        c = (h * d + r) * HEAD_DIM
                        o[:, c:c + HEAD_DIM] = stage[pl.ds(r, tm // d, stride=d), :].astype(BF16)


def _proj(x, gain, sc, sh, w_bf, q_gain, k_gain, rope):
    s, d = x.shape
    n = w_bf.shape[1]
    tm = 256
    aw = N_HEADS * HEAD_DIM
    row = lambda i: (i, 0)
    vec = lambda i: (0, 0)
    out_shape = [jax.ShapeDtypeStruct((s, 2 * aw), BF16)]
    out_specs = [pl.BlockSpec((tm, 2 * aw), row)]
    for _ in range(3):
        for dil in DILATIONS:
            out_shape.append(jax.ShapeDtypeStruct((s // dil, dil * aw), BF16))
            out_specs.append(pl.BlockSpec((tm // dil, dil * aw), row))
    return pl.pallas_call(
        _proj_kernel,
        grid=(s // tm,),
        in_specs=[pl.BlockSpec((tm, d), row), pl.BlockSpec((1, d), vec), pl.BlockSpec((1, d), vec),
                  pl.BlockSpec((1, d), vec),
                  pl.BlockSpec((d, n), vec, pipeline_mode=pl.Buffered(1)),
                  pl.BlockSpec((1, HEAD_DIM), vec), pl.BlockSpec((1, HEAD_DIM), vec),
                  pl.BlockSpec((tm, HEAD_DIM), row), pl.BlockSpec((tm, HEAD_DIM), row),
                  pl.BlockSpec((tm, HEAD_DIM), row)],
        out_specs=out_specs,
        out_shape=out_shape,
        scratch_shapes=[pltpu.VMEM((tm, d), BF16), pltpu.VMEM((3 * N_HEADS, tm, HEAD_DIM), F32)],
        compiler_params=_cparams(("arbitrary",)),
        name="proj",
    )(x, gain, sc, sh, w_bf, q_gain, k_gain, *rope)


def _sgu_kernel(uv_ref, gn_ref, w_ref, b_ref, go_ref, o_ref):
    tm = uv_ref.shape[0]
    width = N_HEADS * HEAD_DIM
    ii = lax.broadcasted_iota(I32, (SGU_CHUNK, SGU_CHUNK), 0)
    jj = lax.broadcasted_iota(I32, (SGU_CHUNK, SGU_CHUNK), 1)
    causal = jj <= ii
    for g in range(N_HEADS):
        cols = slice(g * HEAD_DIM, (g + 1) * HEAD_DIM)
        wg = jnp.where(causal, w_ref[g], 0.0).astype(BF16)
        u = _gelu(uv_ref[:, cols].astype(F32))
        v = _gelu(uv_ref[:, width + g * HEAD_DIM:width + (g + 1) * HEAD_DIM].astype(F32))
        vn = (_rms(v) * gn_ref[:, cols]).astype(BF16)
        bias = b_ref[:, g:g + 1]
        for n in range(tm // SGU_CHUNK):
            rows = slice(n * SGU_CHUNK, (n + 1) * SGU_CHUNK)
            mixed = jnp.dot(wg, vn[rows], preferred_element_type=F32) + bias
            y = u[rows] * mixed
            o_ref[rows, cols] = (_rms(y) * go_ref[:, cols]).astype(BF16)


def _sgu(uv, sgu_norm, sgu_w, sgu_b_t, out_norm):
    s = uv.shape[0]
    width = N_HEADS * HEAD_DIM
    tm = 512
    vec = lambda i: (0, 0)
    return pl.pallas_call(
        _sgu_kernel,
        grid=(s // tm,),
        in_specs=[pl.BlockSpec((tm, 2 * width), lambda i: (i, 0)),
                  pl.BlockSpec((1, width), vec),
                  pl.BlockSpec((N_HEADS, SGU_CHUNK, SGU_CHUNK), lambda i: (0, 0, 0)),
                  pl.BlockSpec((SGU_CHUNK, N_HEADS), vec),
                  pl.BlockSpec((1, width), vec)],
        out_specs=pl.BlockSpec((tm, width), lambda i: (i, 0)),
        out_shape=jax.ShapeDtypeStruct((s, width), BF16),
        compiler_params=_cparams(("arbitrary",)),
        name="sgu",
    )(uv, sgu_norm, sgu_w, sgu_b_t, out_norm)


def _attn_kernel(q1, q4, q16, k1, k4, k16, k1p, k4p, k16p, v1, v4, v16, v1p, v4p, v16p, go_ref,
                 o_ref, kb1, kb4, kb16, vb1, vb4, vb16, os1, os4, os16, ls1, ls4, ls16):
    first = pl.program_id(0) == 0
    blk = ATT_SPAN
    for buf, prev, cur in ((kb1, k1p, k1), (kb4, k4p, k4), (kb16, k16p, k16),
                           (vb1, v1p, v1), (vb4, v4p, v4), (vb16, v16p, v16)):
        buf[0:blk, :] = prev[...]
        buf[blk:, :] = cur[...]

    qi = lax.broadcasted_iota(I32, (blk, 2 * blk), 0)
    kj = lax.broadcasted_iota(I32, (blk, 2 * blk), 1)
    dist = qi + blk - kj
    band = (dist >= 0) & (dist <= ATT_SPAN)
    neg = jnp.where(band, 0.0, -jnp.inf)
    neg0 = jnp.where(first, jnp.where(band & (kj >= blk), 0.0, -jnp.inf), neg)

    for d, q_ref, kb, vb, o_s, l_s in ((1, q1, kb1, vb1, os1, ls1), (4, q4, kb4, vb4, os4, ls4),
                                       (16, q16, kb16, vb16, os16, ls16)):
        nb = ATT_TILE // (blk * d)
        for r in range(d):
            cols = slice(r * HEAD_DIM, (r + 1) * HEAD_DIM)
            for b in range(nb):
                q = q_ref[b * blk:(b + 1) * blk, cols]
                kk = kb[b * blk:(b + 2) * blk, cols]
                vv = vb[b * blk:(b + 2) * blk, cols]
                s = lax.dot_general(q, kk, NT_DIMS, preferred_element_type=F32)
                s = s + (neg0 if b == 0 else neg)
                m = jnp.max(s, axis=-1, keepdims=True)
                e = jnp.exp2(s - m)
                l = jnp.sum(e, axis=-1, keepdims=True)
                o = jnp.dot(e.astype(BF16), vv, preferred_element_type=F32) / l
                lse = jnp.broadcast_to(m + jnp.log2(l), (blk, HEAD_DIM))
                if d == 1:
                    o_s[b * blk:(b + 1) * blk, :] = o
                    l_s[b * blk:(b + 1) * blk, :] = lse
                else:
                    o_s[pl.ds(d * b * blk + r, blk, stride=d), :] = o
                    l_s[pl.ds(d * b * blk + r, blk, stride=d), :] = lse

    step = 256
    for c in range(ATT_TILE // step):
        rows = slice(c * step, (c + 1) * step)
        l1, l4, l16 = ls1[rows, :], ls4[rows, :], ls16[rows, :]
        mx = jnp.maximum(l1, jnp.maximum(l4, l16))
        w1, w4, w16 = jnp.exp2(l1 - mx), jnp.exp2(l4 - mx), jnp.exp2(l16 - mx)
        o = (w1 * os1[rows, :] + w4 * os4[rows, :] + w16 * os16[rows, :]) / (w1 + w4 + w16)
        o_ref[rows, :] = (_rms(o) * go_ref[...]).astype(BF16)


def _attn(qkv, out_norm):
    q1, q4, q16, k1, k4, k16, v1, v4, v16 = qkv
    s = q1.shape[0]
    nt = s // ATT_TILE
    blk = ATT_SPAN

    def cur(d):
        return pl.BlockSpec((ATT_TILE // d, d * HEAD_DIM), lambda i, h: (i, h))

    def prev(d):
        per = ATT_TILE // (d * blk)
        return pl.BlockSpec((blk, d * HEAD_DIM), lambda i, h: (jnp.maximum(i * per - 1, 0), h))

    in_specs = ([cur(d) for d in DILATIONS] + [cur(d) for d in DILATIONS] + [prev(d) for d in DILATIONS]
                + [cur(d) for d in DILATIONS] + [prev(d) for d in DILATIONS]
                + [pl.BlockSpec((1, HEAD_DIM), lambda i, h: (0, h))])
    kv_bufs = [pltpu.VMEM((blk + ATT_TILE // d, d * HEAD_DIM), BF16) for d in DILATIONS]
    acc_bufs = [pltpu.VMEM((ATT_TILE, HEAD_DIM), F32) for _ in DILATIONS]
    return pl.pallas_call(
        _attn_kernel,
        grid=(nt, N_HEADS),
        in_specs=in_specs,
        out_specs=pl.BlockSpec((ATT_TILE, HEAD_DIM), lambda i, h: (i, h)),
        out_shape=jax.ShapeDtypeStruct((s, N_HEADS * HEAD_DIM), BF16),
        scratch_shapes=kv_bufs + kv_bufs + acc_bufs + acc_bufs,
        compiler_params=_cparams(("arbitrary", "arbitrary")),
        name="attn",
    )(q1, q4, q16, k1, k4, k16, k1, k4, k16, v1, v4, v16, v1, v4, v16, out_norm)


def _wout_kernel(a1_ref, a2_ref, w_ref, x_ref, g_ref, o_ref):
    half = a1_ref.shape[1]
    acc = jnp.dot(a1_ref[...], w_ref[0:half, :], preferred_element_type=F32)
    acc = acc + jnp.dot(a2_ref[...], w_ref[half:2 * half, :], preferred_element_type=F32)
    o_ref[...] = x_ref[...] + g_ref[...] * acc


def _wout(y_sgu, y_att, w_bf, x, gate):
    s, d = x.shape
    half = y_sgu.shape[1]
    tm = 512
    row = lambda i: (i, 0)
    vec = lambda i: (0, 0)
    return pl.pallas_call(
        _wout_kernel,
        grid=(s // tm,),
        in_specs=[pl.BlockSpec((tm, half), row), pl.BlockSpec((tm, half), row),
                  pl.BlockSpec((2 * half, d), vec, pipeline_mode=pl.Buffered(1)),
                  pl.BlockSpec((tm, d), row), pl.BlockSpec((1, d), vec)],
        out_specs=pl.BlockSpec((tm, d), row),
        out_shape=jax.ShapeDtypeStruct((s, d), F32),
        compiler_params=_cparams(("arbitrary",)),
        name="wout",
    )(y_sgu, y_att, w_bf, x, gate)


def _router_kernel(x_ref, g_ref, sc_ref, sh_ref, rwt_ref, rb_ref,
                   h_ref, idx_ref, wt_ref, rank_ref, cnt_ref, carry_s):
    tm = x_ref.shape[0]
    gsz = N_EXPERTS // N_GROUPS

    @pl.when(pl.program_id(0) == 0)
    def _():
        carry_s[...] = jnp.zeros_like(carry_s)

    h = (_rms(x_ref[...]) * g_ref[...]) * (1.0 + sc_ref[...]) + sh_ref[...]
    h_ref[...] = h.astype(BF16)

    hh = h.astype(BF16)
    hl = (h - hh.astype(F32)).astype(BF16)
    rw = rwt_ref[...]
    rh = rw.astype(BF16)
    rl = (rw - rh.astype(F32)).astype(BF16)
    dg = functools.partial(lax.dot_general, dimension_numbers=NT_DIMS, preferred_element_type=F32)
    logits = dg(rh, hh) + dg(rl, hh) + dg(rh, hl)
    scores = jax.nn.sigmoid(logits)
    sel = scores + rb_ref[...]

    io8 = lax.broadcasted_iota(I32, (gsz, tm), 0)
    grp, gscore = [], []
    for g in range(N_GROUPS):
        sg = sel[g * gsz:(g + 1) * gsz, :]
        m1 = jnp.max(sg, axis=0, keepdims=True)
        i1 = jnp.min(jnp.where(sg == m1, io8, gsz), axis=0, keepdims=True)
        m2 = jnp.max(jnp.where(io8 == i1, -jnp.inf, sg), axis=0, keepdims=True)
        grp.append(sg)
        gscore.append(m1 + m2)

    parts = []
    for g in range(N_GROUPS):
        beaten = jnp.zeros((1, tm), I32)
        for g2 in range(N_GROUPS):
            if g2 == g:
                continue
            b = (gscore[g2] >= gscore[g]) if g2 < g else (gscore[g2] > gscore[g])
            beaten = beaten + b.astype(I32)
        parts.append(jnp.where(beaten < TOPK_GROUPS, grp[g], -jnp.inf))
    masked = jnp.concatenate(parts, axis=0)

    io = lax.broadcasted_iota(I32, (N_EXPERTS, tm), 0)
    chosen = jnp.zeros((N_EXPERTS, tm), F32)
    idxs, wts = [], []
    for _ in range(TOP_K):
        m = jnp.max(masked, axis=0, keepdims=True)
        am = jnp.min(jnp.where(masked == m, io, N_EXPERTS), axis=0, keepdims=True)
        hit = io == am
        wts.append(jnp.sum(jnp.where(hit, scores, 0.0), axis=0, keepdims=True))
        idxs.append(am)
        chosen = chosen + hit.astype(F32)
        masked = jnp.where(hit, -jnp.inf, masked)
    wsum = wts[0]
    for k in range(1, TOP_K):
        wsum = wsum + wts[k]
    for k in range(TOP_K):
        idx_ref[k:k + 1, :] = idxs[k]
        wt_ref[k:k + 1, :] = wts[k] / wsum * ROUTED_SCALE

    t0 = lax.broadcasted_iota(I32, (tm, tm), 0)
    t1 = lax.broadcasted_iota(I32, (tm, tm), 1)
    before = (t0 < t1).astype(BF16)
    pre = jnp.dot(chosen.astype(BF16), before, preferred_element_type=F32) + carry_s[...]
    for k in range(TOP_K):
        rank_ref[k:k + 1, :] = jnp.sum(jnp.where(io == idxs[k], pre, 0.0), axis=0,
                                       keepdims=True).astype(I32)
    carry_s[...] = carry_s[...] + jnp.sum(chosen, axis=1, keepdims=True)
    cnt_ref[...] = jnp.broadcast_to(carry_s[...], cnt_ref.shape)


def _router(x, gain, sc, sh, rw_t, rbias):
    t, d = x.shape
    tm = 512
    vec = lambda i: (0, 0)
    tok = lambda i: (0, i)
    return pl.pallas_call(
        _router_kernel,
        grid=(t // tm,),
        in_specs=[pl.BlockSpec((tm, d), lambda i: (i, 0)), pl.BlockSpec((1, d), vec),
                  pl.BlockSpec((1, d), vec), pl.BlockSpec((1, d), vec),
                  pl.BlockSpec((N_EXPERTS, d), vec), pl.BlockSpec((N_EXPERTS, 1), vec)],
        out_specs=[pl.BlockSpec((tm, d), lambda i: (i, 0)), pl.BlockSpec((TOP_K, tm), tok),
                   pl.BlockSpec((TOP_K, tm), tok), pl.BlockSpec((TOP_K, tm), tok),
                   pl.BlockSpec((N_EXPERTS, HEAD_DIM), vec)],
        out_shape=[jax.ShapeDtypeStruct((t, d), BF16), jax.ShapeDtypeStruct((TOP_K, t), I32),
                   jax.ShapeDtypeStruct((TOP_K, t), F32), jax.ShapeDtypeStruct((TOP_K, t), I32),
                   jax.ShapeDtypeStruct((N_EXPERTS, HEAD_DIM), F32)],
        scratch_shapes=[pltpu.VMEM((N_EXPERTS, 1), F32)],
        compiler_params=_cparams(("arbitrary",)),
        name="router",
    )(x, gain, sc, sh, rw_t, rbias)


def _dest_kernel(idx_ref, rank_ref, start_ref, o_ref):
    tm = idx_ref.shape[1]
    io = lax.broadcasted_iota(I32, (N_EXPERTS, tm), 0)
    for k in range(TOP_K):
        seg = jnp.sum(jnp.where(io == idx_ref[k:k + 1, :], start_ref[...], 0), axis=0, keepdims=True)
        o_ref[k:k + 1, :] = (seg + rank_ref[k:k + 1, :]) * ROW_SUB


def _dest(idx_t, rank_t, seg_start):
    t = idx_t.shape[1]
    tm = 2048
    tok = lambda i: (0, i)
    return pl.pallas_call(
        _dest_kernel,
        grid=(t // tm,),
        in_specs=[pl.BlockSpec((TOP_K, tm), tok), pl.BlockSpec((TOP_K, tm), tok),
                  pl.BlockSpec((N_EXPERTS, 1), lambda i: (0, 0))],
        out_specs=pl.BlockSpec((TOP_K, tm), tok),
        out_shape=jax.ShapeDtypeStruct((TOP_K, t), I32),
        compiler_params=_cparams(("arbitrary",)),
        name="dest",
    )(idx_t, rank_t, seg_start.reshape(N_EXPERTS, 1))


def _dispatch_kernel(dest_s, h_ref, sg_ref, su_ref, sd_ref, xs_hbm, sh_ref, pk_s, sem_row):
    tm, d = h_ref.shape
    n = TOP_K * tm
    i = pl.program_id(0)
    last = pl.num_programs(0) - 1
    cur = i % 2
    base = i * n

    def row_copy(src_row, dst_row, buf):
        return pltpu.make_async_copy(pk_s.at[buf, pl.ds(src_row, ROW_SUB)],
                                     xs_hbm.at[pl.ds(dst_row, ROW_SUB)], sem_row.at[buf])

    def wait_all(buf):
        def wait(j, c):
            row_copy(0, 0, buf).wait()
            return c
        lax.fori_loop(0, n, wait, 0, unroll=8)

    @pl.when(i >= 2)
    def _():
        wait_all(cur)

    _store_rows(pk_s.at[cur], 0, tm, h_ref[:, 0:d // 2].astype(F32), h_ref[:, d // 2:d].astype(F32))

    vals = {}

    def p_gate():
        vals["gate"] = jnp.dot(h_ref[...], sg_ref[...], preferred_element_type=F32)

    def p_up():
        up = jnp.dot(h_ref[...], su_ref[...], preferred_element_type=F32)
        g = vals["gate"]
        vals["hid"] = (g * jax.nn.sigmoid(g) * up).astype(BF16)

    def p_shared(m):
        def run():
            cols = slice(m * 2 * LANES, (m + 1) * 2 * LANES)
            sh_ref[:, cols] = jnp.dot(vals["hid"], sd_ref[:, cols],
                                      preferred_element_type=F32).astype(sh_ref.dtype)
        return run

    pieces = [p_gate, p_up] + [p_shared(m) for m in range(d // (2 * LANES))]
    j = 0
    for idx, piece in enumerate(pieces):
        for _ in range(n // len(pieces) + (1 if idx < n % len(pieces) else 0)):
            row_copy((j % tm) * ROW_SUB, pl.multiple_of(dest_s[base + j], ROW_SUB), cur).start(priority=j % 2)
            j += 1
        piece()

    @pl.when(i == last)
    def _():
        wait_all(cur)

    @pl.when((i == last) & (i >= 1))
    def _():
        wait_all(1 - cur)

    @pl.when(i == last)
    def _():
        pk_s[...] = jnp.zeros_like(pk_s)
        tail = xs_hbm.shape[0] - 2 * tm * ROW_SUB
        for hb in range(2):
            pad = pltpu.make_async_copy(pk_s.at[hb], xs_hbm.at[pl.ds(tail + hb * tm * ROW_SUB, tm * ROW_SUB)],
                                        sem_row.at[hb])
            pad.start()
            pad.wait()


def _dispatch(dest_flat, h_bf, sg_bf, su_bf, sd_bf):
    t, d = h_bf.shape
    f = sg_bf.shape[1]
    tm = TOKEN_TILE
    assert MOE_BLOCK == 2 * tm
    row = lambda i, dest: (i, 0)
    vec = lambda i, dest: (0, 0)
    grid_spec = pltpu.PrefetchScalarGridSpec(
        num_scalar_prefetch=1,
        grid=(t // tm,),
        in_specs=[pl.BlockSpec((tm, d), row),
                  pl.BlockSpec((d, f), vec), pl.BlockSpec((d, f), vec), pl.BlockSpec((f, d), vec)],
        out_specs=[pl.BlockSpec(memory_space=pl.ANY), pl.BlockSpec((tm, d), row)],
        scratch_shapes=[pltpu.VMEM((2, tm * ROW_SUB, LANES), U32), pltpu.SemaphoreType.DMA((2,))])
    return pl.pallas_call(
        _dispatch_kernel,
        grid_spec=grid_spec,
        out_shape=[jax.ShapeDtypeStruct(((t * TOP_K + MOE_BLOCK) * ROW_SUB, LANES), U32),
                   jax.ShapeDtypeStruct((t, d), BF16)],
        compiler_params=_cparams(("arbitrary",)),
        name="dispatch",
    )(dest_flat, h_bf, sg_bf, su_bf, sd_bf)


def _expert_blocks(counts, m):
    bm = MOE_BLOCK
    n_items = m // bm + N_EXPERTS
    seg_end = jnp.cumsum(counts)
    seg_start = seg_end - counts
    nblk = (counts + bm - 1) // bm
    item_end = jnp.cumsum(nblk)
    w = jnp.arange(n_items, dtype=I32)
    active = (w < item_end[-1]).astype(I32)
    ex = jnp.minimum(jnp.sum((item_end[None, :] <= w[:, None]).astype(I32), axis=1), N_EXPERTS - 1)
    onehot = ex[:, None] == jnp.arange(N_EXPERTS, dtype=I32)[None, :]

    def pick(table):
        return jnp.sum(jnp.where(onehot, table[None, :], 0), axis=1)

    i_in = w - pick(item_end - nblk)
    row0 = jnp.where(active == 1, (pick(seg_start) + i_in * bm) * ROW_SUB, 0)
    newe = ((i_in == 0) & (active == 1)).astype(I32)
    small = ((pick(counts) - i_in * bm <= bm // 2) & (active == 1)).astype(I32)
    cand = jnp.where(counts > 0, jnp.arange(N_EXPERTS, dtype=I32), N_EXPERTS)
    nxt_tab = jnp.concatenate([lax.cummin(cand[::-1])[::-1][1:], jnp.full((1,), N_EXPERTS, I32)])
    nxt = pick(nxt_tab)
    wsel = jnp.where(newe == 1, ex, jnp.where(nxt < N_EXPERTS, nxt, ex))
    return (row0, wsel, active, newe, small), seg_start


def _moe_kernel(row_s, wsel_s, act_s, newe_s, small_s, xs_hbm, wg_ref, wu_ref, wd_ref, ys_hbm,
                x_s, y_s, wg_s, wu_s, wd_s, sem_x, sem_y):
    w = pl.program_id(0)
    n = pl.num_programs(0)
    cur = w % 2
    rows = x_s.shape[1]
    bm = rows // ROW_SUB
    d = wg_s.shape[0]
    half = d // 2
    prev = jnp.maximum(w - 1, 0)
    active = act_s[w] == 1
    small = small_s[w] == 1
    prev_active = (w >= 1) & (act_s[prev] == 1)
    prev_small = small_s[prev] == 1
    nxt = jnp.minimum(w + 1, n - 1)

    def x_copy(step, buf):
        r = pl.multiple_of(row_s[step], ROW_SUB)
        return pltpu.make_async_copy(xs_hbm.at[pl.ds(r, rows)], x_s.at[buf], sem_x.at[buf])

    def y_copy(step, buf, n_rows):
        r = pl.multiple_of(row_s[step], ROW_SUB)
        return pltpu.make_async_copy(y_s.at[buf, pl.ds(0, n_rows)], ys_hbm.at[pl.ds(r, n_rows)],
                                     sem_y.at[buf])

    def wait_prev_result():
        @pl.when(prev_active & prev_small)
        def _():
            y_copy(prev, 1 - cur, rows // 2).wait()

        @pl.when(prev_active & jnp.logical_not(prev_small))
        def _():
            y_copy(prev, 1 - cur, rows).wait()

    @pl.when((w == 0) & active)
    def _():
        x_copy(0, 0).start()

    @pl.when(newe_s[w] == 1)
    def _():
        wg_s[...] = wg_ref[...].astype(BF16)
        wu_s[...] = wu_ref[...].astype(BF16)
        wd_s[...] = wd_ref[...].astype(BF16)

    def block(m_rows):
        x_copy(w, cur).wait()

        @pl.when((w + 1 < n) & (act_s[nxt] == 1))
        def _():
            x_copy(nxt, 1 - cur).start()

        xl, xh = _load_rows(x_s.at[cur], 0, m_rows)
        xl, xh = xl.astype(BF16), xh.astype(BF16)
        gate = (jnp.dot(xl, wg_s[0:half, :], preferred_element_type=F32)
                + jnp.dot(xh, wg_s[half:d, :], preferred_element_type=F32))
        up = (jnp.dot(xl, wu_s[0:half, :], preferred_element_type=F32)
              + jnp.dot(xh, wu_s[half:d, :], preferred_element_type=F32))
        hid = (gate * jax.nn.sigmoid(gate) * up).astype(BF16)
        y = jnp.dot(hid, wd_s[...], preferred_element_type=F32)
        _store_rows(y_s.at[cur], 0, m_rows, y[:, 0:half], y[:, half:d])
        wait_prev_result()
        y_copy(w, cur, m_rows * ROW_SUB).start()

        @pl.when(w == n - 1)
        def _():
            y_copy(w, cur, m_rows * ROW_SUB).wait()

    @pl.when(active & jnp.logical_not(small))
    def _():
        block(bm)

    @pl.when(active & small)
    def _():
        block(bm // 2)

    @pl.when(jnp.logical_not(active))
    def _():
        wait_prev_result()


def _moe(items, xs, layer, w_gate, w_up, w_down):
    _, _, d, f = w_gate.shape
    bm = MOE_BLOCK
    wmap = lambda w, row, ws, act, ne, sm: (layer, ws[w], 0, 0)
    grid_spec = pltpu.PrefetchScalarGridSpec(
        num_scalar_prefetch=5,
        grid=(items[0].shape[0],),
        in_specs=[pl.BlockSpec(memory_space=pl.ANY),
                  pl.BlockSpec((None, None, d, f), wmap),
                  pl.BlockSpec((None, None, d, f), wmap),
                  pl.BlockSpec((None, None, f, d), wmap)],
        out_specs=pl.BlockSpec(memory_space=pl.ANY),
        scratch_shapes=[pltpu.VMEM((2, bm * ROW_SUB, LANES), U32), pltpu.VMEM((2, bm * ROW_SUB, LANES), U32),
                        pltpu.VMEM((d, f), BF16), pltpu.VMEM((d, f), BF16), pltpu.VMEM((f, d), BF16),
                        pltpu.SemaphoreType.DMA((2,)), pltpu.SemaphoreType.DMA((2,))])
    return pl.pallas_call(
        _moe_kernel,
        grid_spec=grid_spec,
        out_shape=jax.ShapeDtypeStruct(xs.shape, U32),
        compiler_params=_cparams(("arbitrary",)),
        name="moe",
    )(*items, xs, w_gate, w_up, w_down)


def _combine_kernel(dest_s, ys_hbm, sh_ref, wt_ref, x_ref, g_ref, o_ref, y_s, sem_row):
    tm, d = x_ref.shape
    half = d // 2
    n = TOP_K * tm
    i = pl.program_id(0)
    last = pl.num_programs(0) - 1
    cur = i % 2
    nxt = 1 - cur
    base_next = jnp.minimum(i + 1, last) * n

    def row_copy(src_row, dst_row, buf):
        return pltpu.make_async_copy(ys_hbm.at[pl.ds(src_row, ROW_SUB)],
                                     y_s.at[buf, pl.ds(dst_row, ROW_SUB)], sem_row.at[buf])

    def wait_all(buf):
        def wait(j, c):
            row_copy(0, 0, buf).wait()
            return c
        lax.fori_loop(0, n, wait, 0, unroll=8)

    @pl.when(i == 0)
    def _():
        def start(j, c):
            row_copy(pl.multiple_of(dest_s[j], ROW_SUB), pl.multiple_of(j * ROW_SUB, ROW_SUB), 0).start()
            return c
        lax.fori_loop(0, n, start, 0, unroll=8)

    wait_all(cur)

    def p_routed(c):
        def run():
            r_lo = jnp.zeros((tm, LANES), F32)
            r_hi = jnp.zeros((tm, LANES), F32)
            for k in range(TOP_K):
                lo, hi = _unpack_pair(y_s[cur, pl.ds(k * tm * ROW_SUB + c, tm, stride=ROW_SUB), :])
                wk = wt_ref[:, k:k + 1]
                r_lo = r_lo + wk * lo
                r_hi = r_hi + wk * hi
            for r, off in ((r_lo, c * LANES), (r_hi, half + c * LANES)):
                cols = slice(off, off + LANES)
                o_ref[:, cols] = x_ref[:, cols] + g_ref[:, cols] * (r + sh_ref[:, cols].astype(F32))
        return run

    pieces = [p_routed(c) for c in range(ROW_SUB)]
    j = 0
    for idx, piece in enumerate(pieces):
        for _ in range(n // len(pieces) + (1 if idx < n % len(pieces) else 0)):
            row_copy(pl.multiple_of(dest_s[base_next + j], ROW_SUB), j * ROW_SUB, nxt).start(priority=j % 2)
            j += 1
        piece()

    @pl.when(i == last)
    def _():
        wait_all(nxt)


def _combine(dest_flat, ys, shared, wt, x, gate):
    t, d = x.shape
    tm = TOKEN_TILE
    n = TOP_K * tm
    row = lambda i, dest: (i, 0)
    vec = lambda i, dest: (0, 0)
    grid_spec = pltpu.PrefetchScalarGridSpec(
        num_scalar_prefetch=1,
        grid=(t // tm,),
        in_specs=[pl.BlockSpec(memory_space=pl.ANY),
                  pl.BlockSpec((tm, d), row), pl.BlockSpec((tm, TOP_K), row),
                  pl.BlockSpec((tm, d), row), pl.BlockSpec((1, d), vec)],
        out_specs=pl.BlockSpec((tm, d), row),
        scratch_shapes=[pltpu.VMEM((2, n * ROW_SUB, LANES), U32), pltpu.SemaphoreType.DMA((2,))])
    return pl.pallas_call(
        _combine_kernel,
        grid_spec=grid_spec,
        out_shape=jax.ShapeDtypeStruct((t, d), F32),
        compiler_params=_cparams(("arbitrary",)),
        name="combine",
    )(dest_flat, ys, shared, wt, x, gate)


def _rope_tables(s):
    half = ROPE_DIM // 2
    inv = ROPE_THETA ** (-jnp.arange(half, dtype=F32) * 2.0 / ROPE_DIM)
    ang = jnp.arange(s, dtype=I32).astype(F32)[:, None] * inv[None, :]
    cos, sin = jnp.cos(ang), jnp.sin(ang)
    pad = HEAD_DIM - ROPE_DIM
    rc = jnp.concatenate([cos, cos, jnp.ones((s, pad), F32)], axis=1)
    ra = jnp.concatenate([jnp.zeros((s, half), F32), sin, jnp.zeros((s, pad), F32)], axis=1)
    rb = jnp.concatenate([-sin, jnp.zeros((s, half + pad), F32)], axis=1)
    return rc, ra, rb


def _mixer(x, mod, rope, mix_norm, w_in, sgu_norm, sgu_w, sgu_b, q_norm, k_norm,
           out_norm_sgu, out_norm_att, w_out):
    d = x.shape[1]
    sh_m, sc_m, g_m = mod[:, 0:d], mod[:, d:2 * d], mod[:, 2 * d:3 * d]
    outs = _proj(x, mix_norm.reshape(1, d), sc_m, sh_m, w_in.astype(BF16),
                 q_norm.reshape(1, HEAD_DIM), k_norm.reshape(1, HEAD_DIM), rope)
    y_sgu = _sgu(outs[0], sgu_norm.reshape(1, -1), sgu_w, sgu_b.T, out_norm_sgu.reshape(1, -1))
    y_att = _attn(outs[1:], out_norm_att.reshape(1, -1))
    return _wout(y_sgu, y_att, w_out.astype(BF16), x, g_m)


def _ffn(x, mod, layer, ffn_norm, router_w, router_bias, exp_gate, exp_up, exp_down,
         shared_gate, shared_up, shared_down):
    t, d = x.shape
    sh_f, sc_f, g_f = mod[:, 3 * d:4 * d], mod[:, 4 * d:5 * d], mod[:, 5 * d:6 * d]
    h_bf, idx_t, wt_t, rank_t, cnt = _router(x, ffn_norm.reshape(1, d), sc_f, sh_f, router_w.T,
                                             router_bias.reshape(N_EXPERTS, 1))
    counts = cnt[:, 0].astype(I32)
    items, seg_start = _expert_blocks(counts, t * TOP_K)
    dest_t = _dest(idx_t, rank_t, seg_start)
    nt = t // TOKEN_TILE
    dest_flat = dest_t.reshape(TOP_K, nt, TOKEN_TILE).transpose(1, 0, 2).reshape(-1)
    xs, shared = _dispatch(dest_flat, h_bf, shared_gate.astype(BF16), shared_up.astype(BF16),
                           shared_down.astype(BF16))
    ys = _moe(items, xs, layer, exp_gate, exp_up, exp_down)
    return _combine(dest_flat, ys, shared, wt_t.T, x, g_f)


def kernel(x, c, ada_w, ada_b, mix_norm, w_in, sgu_norm, sgu_w, sgu_b, q_norm, k_norm, out_norm_sgu,
           out_norm_att, w_out, ffn_norm, router_w, router_bias, exp_gate, exp_up, exp_down,
           shared_gate, shared_up, shared_down):
    b, s, d = x.shape
    assert b == 1 and s % ATT_TILE == 0 and d == 2 * ROW_SUB * LANES
    mods = _ada(c, ada_w, ada_b)
    rope = _rope_tables(s)
    xf = x.reshape(s, d)
    for l in range(ada_w.shape[0]):
        xf = _mixer(xf, mods[l], rope, mix_norm[l], w_in[l], sgu_norm[l], sgu_w[l], sgu_b[l], q_norm[l],
                    k_norm[l], out_norm_sgu[l], out_norm_att[l], w_out[l])
        xf = _ffn(xf, mods[l], l, ffn_norm[l], router_w[l], router_bias[l], exp_gate, exp_up,
                  exp_down, shared_gate[l], shared_up[l], shared_down[l])
    return xf.reshape(b, s, d)
```
